```python
import jax, jax.numpy as jnp
from jax import lax
import numpy as np

D_MODEL = 1024
BATCH = 2
SEQ = 8192
DEPTH = 4

GRID_W = 64
CTX_LEN = 256
HEAD_DIM = 64
BLOCK = 128
MLP_GROUPS = 4
MLP_WIDTH = MLP_GROUPS * HEAD_DIM
WIN_HEADS = 4
WIN_KV = 2
WINDOW = 128
GLB_HEADS = 4
GLB_KV = 2
RET_HEADS = 4
RET_VDIM = HEAD_DIM
N_BRANCH = 4
BRANCH_W = 256
ROPE_BASE = 10000.0
D_FF = 2816
N_EXPERTS = 8
TOP_K = 2
D_FF_EXPERT = 1792
N_DENSE = (DEPTH + 1) // 2
N_MOE = DEPTH // 2
EPS = 1e-6
NEG_INF = -1e30
IN_SPLITS = (2 * MLP_WIDTH,
             WIN_HEADS * HEAD_DIM, WIN_KV * HEAD_DIM, WIN_KV * HEAD_DIM,
             GLB_HEADS * HEAD_DIM, GLB_KV * HEAD_DIM, GLB_KV * HEAD_DIM,
             RET_HEADS * HEAD_DIM, RET_HEADS * HEAD_DIM, RET_HEADS * RET_VDIM, RET_HEADS * RET_VDIM,
             N_BRANCH * D_MODEL)
IN_WIDTH = sum(IN_SPLITS)

kernel_name = 'hybrid_parallel_mixer_dit'


def rms_norm(x, g):
    xf = x.astype(jnp.float32)
    y = xf * lax.rsqrt(jnp.mean(xf * xf, axis=-1, keepdims=True) + EPS)
    return (y * g.astype(jnp.float32)).astype(x.dtype)


def group_norm(x, g):
    xf = x.astype(jnp.float32)
    xc = xf - jnp.mean(xf, axis=-1, keepdims=True)
    var = jnp.mean(xc * xc, axis=-1, keepdims=True)
    return xc * lax.rsqrt(var + EPS) * g.astype(jnp.float32)


def heads(t, n, dh=HEAD_DIM):
    return t.reshape(t.shape[0], t.shape[1], n, dh)


def split_cols(p):
    parts = []
    off = 0
    for n in IN_SPLITS:
        parts.append(p[..., off:off + n])
        off += n
    return parts


def rope(x, tables):
    cos, sin = tables
    half = x.shape[-1] // 2
    x1 = x[..., :half].astype(jnp.float32)
    x2 = x[..., half:].astype(jnp.float32)
    c = cos[:, None, :]
    s = sin[:, None, :]
    return jnp.concatenate([x1 * c - x2 * s, x1 * s + x2 * c], axis=-1).astype(x.dtype)


def axial_rope_tables(n_rows):
    row = jnp.repeat(jnp.arange(n_rows, dtype=jnp.float32), GRID_W)
    col = jnp.tile(jnp.arange(GRID_W, dtype=jnp.float32), n_rows)
    n_freq = HEAD_DIM // 4
    inv = ROPE_BASE ** (-jnp.arange(n_freq, dtype=jnp.float32) / n_freq)
    ang = jnp.concatenate([row[:, None] * inv, col[:, None] * inv], axis=-1)
    return jnp.cos(ang), jnp.sin(ang)


def seq_rope_tables(pos):
    n_freq = HEAD_DIM // 2
    inv = ROPE_BASE ** (-jnp.arange(n_freq, dtype=jnp.float32) / n_freq)
    ang = pos[:, None] * inv
    return jnp.cos(ang), jnp.sin(ang)


def chunk_mlp(uv, w_s, b_s, g_v):
    Bn, N, _ = uv.shape
    u, v = jnp.split(uv, 2, axis=-1)
    v = rms_norm(heads(v, MLP_GROUPS), g_v)
    v = v.reshape(Bn, N // BLOCK, BLOCK, MLP_GROUPS, HEAD_DIM)
    mixed = jnp.einsum('gij,bcjgd->bcigd', w_s, v) + b_s.T[None, None, :, :, None]
    return u * mixed.reshape(Bn, N, MLP_WIDTH)


def dense_attn(q, k, v, sink):
    Bn, N, Hq, d = q.shape
    Hkv = k.shape[2]
    G = Hq // Hkv
    qg = q.reshape(Bn, N, Hkv, G, d)
    s = jnp.einsum('bqhgd,bkhd->bhgqk', qg, k).astype(jnp.float32) * (d ** -0.5)
    if sink is not None:
        sk = jnp.broadcast_to(sink.astype(jnp.float32).reshape(1, Hkv, G, 1, 1), s.shape[:-1] + (1,))
        p = jax.nn.softmax(jnp.concatenate([s, sk], axis=-1), axis=-1)[..., :-1]
    else:
        p = jax.nn.softmax(s, axis=-1)
    o = jnp.einsum('bhgqk,bkhd->bqhgd', p.astype(v.dtype), v)
    return o.reshape(Bn, N, Hq * d)


def window_attn_latent(q, k, v, k_c, v_c, sink):
    Bn, S, Hq, d = q.shape
    Hkv = k.shape[2]
    G = Hq // Hkv
    nb = S // BLOCK
    L = k_c.shape[1]
    scale = d ** -0.5
    qb = q.reshape(Bn, nb, BLOCK, Hkv, G, d)
    pad = jnp.zeros((Bn, BLOCK, Hkv, d), k.dtype)
    kp = jnp.concatenate([pad, k, pad], axis=1).reshape(Bn, nb + 2, BLOCK, Hkv, d)
    vp = jnp.concatenate([pad, v, pad], axis=1).reshape(Bn, nb + 2, BLOCK, Hkv, d)
    kw = jnp.concatenate([kp[:, :-2], kp[:, 1:-1], kp[:, 2:]], axis=2)
    vw = jnp.concatenate([vp[:, :-2], vp[:, 1:-1], vp[:, 2:]], axis=2)
    qpos = jnp.arange(nb)[:, None] * BLOCK + jnp.arange(BLOCK)[None, :]
    kpos = jnp.arange(nb)[:, None] * BLOCK - BLOCK + jnp.arange(3 * BLOCK)[None, :]
    valid = (jnp.abs(qpos[:, :, None] - kpos[:, None, :]) <= WINDOW) & (kpos[:, None, :] >= 0) & (kpos[:, None, :] < S)
    s_loc = jnp.einsum('bnqhgd,bnkhd->bnhgqk', qb, kw).astype(jnp.float32) * scale
    s_loc = jnp.where(valid[None, :, None, None], s_loc, NEG_INF)
    s_ctx = jnp.einsum('bnqhgd,bkhd->bnhgqk', qb, k_c).astype(jnp.float32) * scale
    s_sink = jnp.broadcast_to(sink.astype(jnp.float32).reshape(1, 1, Hkv, G, 1, 1), s_ctx.shape[:-1] + (1,))
    p = jax.nn.softmax(jnp.concatenate([s_ctx, s_loc, s_sink], axis=-1), axis=-1).astype(v.dtype)
    o = (jnp.einsum('bnhgqk,bkhd->bnqhgd', p[..., :L], v_c)
         + jnp.einsum('bnhgqk,bnkhd->bnqhgd', p[..., L:L + 3 * BLOCK], vw))
    return o.reshape(Bn, S, Hq * d)


def global_attn_latent(q, k_all, v_all):
    Bn, S, Hq, d = q.shape
    Hkv = k_all.shape[2]
    G = Hq // Hkv
    nb = S // BLOCK
    scale = d ** -0.5
    qb = q.reshape(Bn, nb, BLOCK, Hkv, G, d).transpose(1, 0, 2, 3, 4, 5)

    def one_block(qblk):
        s = jnp.einsum('bqhgd,bkhd->bhgqk', qblk, k_all).astype(jnp.float32) * scale
        p = jax.nn.softmax(s, axis=-1).astype(v_all.dtype)
        return jnp.einsum('bhgqk,bkhd->bqhgd', p, v_all)

    o = lax.map(one_block, qb)
    return o.transpose(1, 0, 2, 3, 4, 5).reshape(Bn, S, Hq * d)


def retention_scan(q, k, v, log_gamma, s0, shift):
    Bn, N, H, dk = q.shape
    dv = v.shape[-1]
    nc = N // BLOCK
    idx = jnp.arange(BLOCK, dtype=jnp.float32)
    rel = idx[:, None] - idx[None, :] - shift
    mask = rel >= 0
    decay_in = jnp.where(mask[None], jnp.exp(log_gamma[:, None, None] * jnp.where(mask, rel, 0.0)[None]), 0.0)
    q_dec = jnp.exp(log_gamma[:, None] * (idx + 1.0 - shift)[None, :])
    k_dec = jnp.exp(log_gamma[:, None] * (BLOCK - 1.0 - idx)[None, :])
    chunk_dec = jnp.exp(log_gamma * BLOCK)[None, :, None, None]

    def to_chunks(t):
        return t.astype(jnp.float32).reshape(Bn, nc, BLOCK, H, t.shape[-1]).transpose(1, 0, 3, 2, 4)

    def step(state, inp):
        qc, kc, vc = inp
        qk = jnp.einsum('bhqd,bhkd->bhqk', qc, kc) * decay_in
        o = jnp.einsum('bhqk,bhkv->bhqv', qk, vc) + jnp.einsum('bhqd,bhdv->bhqv', qc * q_dec[..., None], state)
        state = chunk_dec * state + jnp.einsum('bhkd,bhkv->bhdv', kc * k_dec[..., None], vc)
        return state, o

    state, o = lax.scan(step, s0, (to_chunks(q), to_chunks(k), to_chunks(v)))
    return o.transpose(1, 0, 3, 2, 4).reshape(Bn, N, H, dv), state


def retention_out(o, g, gate_cols):
    y = group_norm(o, g)
    y = y.reshape(o.shape[0], o.shape[1], -1) * jax.nn.silu(gate_cols.astype(jnp.float32))
    return y.astype(gate_cols.dtype)


def merge(branches, gate_cols, w_branch, w_out):
    gates = jax.nn.sigmoid(gate_cols.reshape(gate_cols.shape[:-1] + (N_BRANCH, D_MODEL)))
    acc = gates[..., 0, :] * (branches[0] @ w_branch[0])
    for i in range(1, N_BRANCH):
        acc = acc + gates[..., i, :] * (branches[i] @ w_branch[i])
    return acc @ w_out


def token_mixers(h_ctx, h_lat, w_in, mlp_g_v, mlp_w_s, mlp_b_s, win_q_norm, win_k_norm, win_sink,
                 glb_q_norm, glb_k_norm, ret_decay, ret_gn, w_branch, w_out, axial, rot_ctx, rot_lat, need_ctx):
    pl = split_cols(h_lat @ w_in)
    pc = split_cols(h_ctx @ w_in)
    a_lat = chunk_mlp(jax.nn.gelu(pl[0]), mlp_w_s, mlp_b_s, mlp_g_v)
    wk_c = rms_norm(heads(pc[2], WIN_KV), win_k_norm)
    wv_c = heads(pc[3], WIN_KV)
    wq_l = rope(rms_norm(heads(pl[1], WIN_HEADS), win_q_norm), axial)
    wk_l = rope(rms_norm(heads(pl[2], WIN_KV), win_k_norm), axial)
    b_lat = window_attn_latent(wq_l, wk_l, heads(pl[3], WIN_KV), wk_c, wv_c, win_sink)
    gk_c = rms_norm(heads(pc[5], GLB_KV), glb_k_norm)
    gv_c = heads(pc[6], GLB_KV)
    gq_l = rope(rms_norm(heads(pl[4], GLB_HEADS), glb_q_norm), axial)
    gk_l = rope(rms_norm(heads(pl[5], GLB_KV), glb_k_norm), axial)
    c_lat = global_attn_latent(gq_l, jnp.concatenate([gk_c, gk_l], axis=1),
                               jnp.concatenate([gv_c, heads(pl[6], GLB_KV)], axis=1))
    k_scale = HEAD_DIM ** -0.5
    rq_c = rope(heads(pc[7], RET_HEADS), rot_ctx)
    rk_c = rope(heads(pc[8], RET_HEADS), rot_ctx) * k_scale
    rv_c = heads(pc[9], RET_HEADS, RET_VDIM)
    rq_l = rope(heads(pl[7], RET_HEADS), rot_lat)
    rk_l = rope(heads(pl[8], RET_HEADS), rot_lat) * k_scale
    rv_l = heads(pl[9], RET_HEADS, RET_VDIM)
    log_g = jax.nn.log_sigmoid(ret_decay.astype(jnp.float32))
    s0 = jnp.zeros((h_lat.shape[0], RET_HEADS, HEAD_DIM, RET_VDIM), jnp.float32)
    flip = lambda t: jnp.flip(t, axis=1)
    oc_f, st_f = retention_scan(rq_c, rk_c, rv_c, log_g[0], s0, 0)
    oc_b, st_b = retention_scan(flip(rq_c), flip(rk_c), flip(rv_c), log_g[1], s0, 1)
    ol_f, _ = retention_scan(rq_l, rk_l, rv_l, log_g[0], st_f, 0)
    ol_b, _ = retention_scan(flip(rq_l), flip(rk_l), flip(rv_l), log_g[1], st_b, 1)
    d_lat = retention_out(ol_f + flip(ol_b), ret_gn, pl[10])
    m_lat = merge((a_lat, b_lat, c_lat, d_lat), pl[11], w_branch, w_out)
    if not need_ctx:
        return None, m_lat
    a_ctx = chunk_mlp(jax.nn.gelu(pc[0]), mlp_w_s, mlp_b_s, mlp_g_v)
    b_ctx = dense_attn(rms_norm(heads(pc[1], WIN_HEADS), win_q_norm), wk_c, wv_c, win_sink)
    c_ctx_out = dense_attn(rms_norm(heads(pc[4], GLB_HEADS), glb_q_norm), gk_c, gv_c, None)
    d_ctx = retention_out(oc_f + flip(oc_b), ret_gn, pc[10])
    m_ctx = merge((a_ctx, b_ctx, c_ctx_out, d_ctx), pc[11], w_branch, w_out)
    return m_ctx, m_lat


def swiglu(h, w1, w3, w2):
    return (jax.nn.silu(h @ w1) * (h @ w3)) @ w2


def moe_swiglu(h, router_w, router_b, w1, w3, w2):
    logits = (h @ router_w + router_b).astype(jnp.float32)
    top_v, top_i = lax.top_k(logits, TOP_K)
    p = jax.nn.softmax(top_v, axis=-1)
    combine = jnp.sum(jax.nn.one_hot(top_i, N_EXPERTS, dtype=jnp.float32) * p[..., None], axis=-2)
    out = combine[..., 0:1].astype(h.dtype) * swiglu(h, w1[0], w3[0], w2[0])
    for e in range(1, N_EXPERTS):
        out = out + combine[..., e:e + 1].astype(h.dtype) * swiglu(h, w1[e], w3[e], w2[e])
    return out


def channel_mixer(h, layer, ffn_w1, ffn_w3, ffn_w2, router_w, router_b, moe_w1, moe_w3, moe_w2):
    j = layer // 2
    if layer % 2 == 0:
        return swiglu(h, ffn_w1[j], ffn_w3[j], ffn_w2[j])
    return moe_swiglu(h, router_w[j], router_b[j], moe_w1[j], moe_w3[j], moe_w2[j])


def setup_inputs(seed: int = 0) -> dict:
    key = jax.random.key(seed)
    ks = jax.random.split(key, 32)
    D = D_MODEL

    def nrm(k, shape, s):
        return jax.random.normal(k, shape, jnp.float32) * s

    gam = 1.0 - 2.0 ** (-5.0 - jnp.arange(RET_HEADS, dtype=jnp.float32))
    decay_logit = jnp.log(gam) - jnp.log1p(-gam)
    return {
        'x': nrm(ks[0], (BATCH, SEQ, D), 1.0),
        'c': nrm(ks[1], (BATCH, D), 1.0),
        'ctx': nrm(ks[2], (BATCH, CTX_LEN, D), 1.0),
        'c_ctx': nrm(ks[3], (D,), 1.0),
        'w_mod': nrm(ks[4], (DEPTH, D, 6 * D), 0.5 * D ** -0.5),
        'b_mod': nrm(ks[5], (DEPTH, 6 * D), 0.02),
        'g_norm1': 1.0 + nrm(ks[6], (DEPTH, D), 0.02),
        'g_norm2': 1.0 + nrm(ks[7], (DEPTH, D), 0.02),
        'w_in': nrm(ks[8], (DEPTH, D, IN_WIDTH), D ** -0.5),
        'mlp_g_v': 1.0 + nrm(ks[9], (DEPTH, MLP_GROUPS, HEAD_DIM), 0.02),
        'mlp_w_s': nrm(ks[10], (DEPTH, MLP_GROUPS, BLOCK, BLOCK), BLOCK ** -0.5),
        'mlp_b_s': 1.0 + nrm(ks[11], (DEPTH, MLP_GROUPS, BLOCK), 0.1),
        'win_q_norm': 1.0 + nrm(ks[12], (DEPTH, HEAD_DIM), 0.02),
        'win_k_norm': 1.0 + nrm(ks[13], (DEPTH, HEAD_DIM), 0.02),
        'win_sink': nrm(ks[14], (DEPTH, WIN_HEADS), 0.5),
        'glb_q_norm': 1.0 + nrm(ks[15], (DEPTH, HEAD_DIM), 0.02),
        'glb_k_norm': 1.0 + nrm(ks[16], (DEPTH, HEAD_DIM), 0.02),
        'ret_decay': decay_logit[None, None, :] + nrm(ks[17], (DEPTH, 2, RET_HEADS), 0.1),
        'ret_gn': 1.0 + nrm(ks[18], (DEPTH, RET_HEADS, RET_VDIM), 0.02),
        'w_branch': nrm(ks[19], (DEPTH, N_BRANCH, BRANCH_W, D), BRANCH_W ** -0.5),
        'w_out': nrm(ks[20], (DEPTH, D, D), D ** -0.5),
        'ffn_w1': nrm(ks[21], (N_DENSE, D, D_FF), D ** -0.5),
        'ffn_w3': nrm(ks[22], (N_DENSE, D, D_FF), D ** -0.5),
        'ffn_w2': nrm(ks[23], (N_DENSE, D_FF, D), D_FF ** -0.5),
        'router_w': nrm(ks[24], (N_MOE, D, N_EXPERTS), D ** -0.5),
        'router_b': nrm(ks[25], (N_MOE, N_EXPERTS), 0.01),
        'moe_w1': nrm(ks[26], (N_MOE, N_EXPERTS, D, D_FF_EXPERT), D ** -0.5),
        'moe_w3': nrm(ks[27], (N_MOE, N_EXPERTS, D, D_FF_EXPERT), D ** -0.5),
        'moe_w2': nrm(ks[28], (N_MOE, N_EXPERTS, D_FF_EXPERT, D), D_FF_EXPERT ** -0.5),
    }


def reference(x, c, ctx, c_ctx, w_mod, b_mod, g_norm1, g_norm2, w_in, mlp_g_v, mlp_w_s, mlp_b_s,
              win_q_norm, win_k_norm, win_sink, glb_q_norm, glb_k_norm, ret_decay, ret_gn, w_branch, w_out,
              ffn_w1, ffn_w3, ffn_w2, router_w, router_b, moe_w1, moe_w3, moe_w2):
    S = x.shape[1]
    L = ctx.shape[1]
    n_rows = S // GRID_W
    axial = axial_rope_tables(n_rows)
    rot_ctx = seq_rope_tables(jnp.arange(L, dtype=jnp.float32))
    rot_lat = seq_rope_tables(L + jnp.arange(S, dtype=jnp.float32))
    silu_c = jax.nn.silu(c)
    silu_cc = jax.nn.silu(c_ctx)[None, :]
    for layer in range(DEPTH):
        need_ctx = layer < DEPTH - 1
        mod_l = (silu_c @ w_mod[layer] + b_mod[layer])[:, None, :]
        mod_c = (silu_cc @ w_mod[layer] + b_mod[layer])[:, None, :]
        sh1_l, sc1_l, g1_l, sh2_l, sc2_l, g2_l = jnp.split(mod_l, 6, axis=-1)
        sh1_c, sc1_c, g1_c, sh2_c, sc2_c, g2_c = jnp.split(mod_c, 6, axis=-1)
        h_l = rms_norm(x, g_norm1[layer]) * (1.0 + sc1_l) + sh1_l
        h_c = rms_norm(ctx, g_norm1[layer]) * (1.0 + sc1_c) + sh1_c
        m_c, m_l = token_mixers(h_c, h_l, w_in[layer], mlp_g_v[layer], mlp_w_s[layer], mlp_b_s[layer],
                                win_q_norm[layer], win_k_norm[layer], win_sink[layer],
                                glb_q_norm[layer], glb_k_norm[layer], ret_decay[layer], ret_gn[layer],
                                w_branch[layer], w_out[layer], axial, rot_ctx, rot_lat, need_ctx)
        x = x + g1_l * m_l
        h_l = rms_norm(x, g_norm2[layer]) * (1.0 + sc2_l) + sh2_l
        x = x + g2_l * channel_mixer(h_l, layer, ffn_w1, ffn_w3, ffn_w2, router_w, router_b, moe_w1, moe_w3, moe_w2)
        if need_ctx:
            ctx = ctx + g1_c * m_c
            h_c = rms_norm(ctx, g_norm2[layer]) * (1.0 + sc2_c) + sh2_c
            ctx = ctx + g2_c * channel_mixer(h_c, layer, ffn_w1, ffn_w3, ffn_w2, router_w, router_b, moe_w1, moe_w3, moe_w2)
    return x
```

```python
import functools

import jax
import jax.numpy as jnp
from jax import lax
from jax.experimental import pallas as pl
from jax.experimental.pallas import tpu as pltpu

F32 = jnp.float32
BF16 = jnp.bfloat16

HEAD_DIM = 64
BLOCK = 128
GRID_W = 64
ROPE_BASE = 10000.0
N_BRANCH = 4
BRANCH_W = 256
N_EXPERTS = 8
EPS = 1e-6
NEG_INF = -1e30
IN_TILE = 512
N_MIX_TILES = 5
VMEM_LIMIT = 56 * 1024 * 1024


def _cparams(sem):
    return pltpu.CompilerParams(dimension_semantics=sem, vmem_limit_bytes=VMEM_LIMIT)


def _dot(a, b):
    return jnp.dot(a, b, preferred_element_type=F32)


def _dot_nt(a, b):
    return lax.dot_general(a, b, (((1,), (1,)), ((), ())), preferred_element_type=F32)


def _split_dot(a, b_bf16):
    hi = a.astype(BF16)
    lo = (a - hi.astype(F32)).astype(BF16)
    return _dot(hi, b_bf16) + _dot(lo, b_bf16)


def _pick_tile(n, candidates):
    for c in candidates:
        if n % c == 0:
            return c
    raise ValueError(f"no tile for {n} in {candidates}")


def _modulated_norm(x, g, sh_l, sc_l, sh_c, sc_c, row0, n_ctx):
    tm = x.shape[0]
    ms = jnp.mean(x * x, axis=-1, keepdims=True)
    y = x * lax.rsqrt(ms + EPS) * g
    row = row0 + lax.broadcasted_iota(jnp.int32, (tm, 1), 0)
    is_ctx = row < n_ctx
    sh = jnp.where(is_ctx, sh_c, sh_l)
    sc = jnp.where(is_ctx, sc_c, sc_l)
    return y * (1.0 + sc) + sh


def _mod_kernel(c_ref, w_ref, b_ref, o_ref):
    c = c_ref[...]
    s = c * jax.nn.sigmoid(c)
    s_hi = s.astype(BF16)
    s_lo = (s - s_hi.astype(F32)).astype(BF16)
    w = w_ref[0]
    w_hi = w.astype(BF16)
    w_lo = (w - w_hi.astype(F32)).astype(BF16)
    o_ref[0] = _dot(s_hi, w_hi) + _dot(s_hi, w_lo) + _dot(s_lo, w_hi) + b_ref[0]


def _modulation(cvec, w_mod, b_mod):
    depth, d, n = w_mod.shape
    tn = _pick_tile(n, (1536, 1024, 512, 128))
    return pl.pallas_call(
        _mod_kernel,
        grid=(depth, n // tn),
        in_specs=[
            pl.BlockSpec((8, d), lambda l, j: (0, 0)),
            pl.BlockSpec((1, d, tn), lambda l, j: (l, 0, j)),
            pl.BlockSpec((1, 1, tn), lambda l, j: (l, 0, j)),
        ],
        out_specs=pl.BlockSpec((1, 8, tn), lambda l, j: (l, 0, j)),
        out_shape=jax.ShapeDtypeStruct((depth, 8, n), F32),
        compiler_params=_cparams(("parallel", "parallel")),
        name="modulation",
    )(cvec, w_mod, b_mod.reshape(depth, 1, n))


def _group_rms(v, bd, gain, nmask):
    gs = _split_dot(v * v, bd)
    n = v * lax.rsqrt(gs * (1.0 / HEAD_DIM) + EPS) * gain
    return jnp.where(nmask > 0.0, n, v)


def _rope(v, cos, sin):
    w = v.shape[-1]
    lane = lax.broadcasted_iota(jnp.int32, v.shape, 1)
    first = (lane & 32) == 0
    partner = jnp.where(first, pltpu.roll(v, w - 32, 1), pltpu.roll(v, 32, 1))
    return v * cos + partner * sin


def _tile4(t):
    return jnp.concatenate([t, t, t, t], axis=1)


def _inproj_kernel(x_ref, modl_ref, modc_ref, g_ref, w_ref, gain_ref, nmask_ref, post_ref, bd_ref,
                   cax_ref, sax_ref, csq_ref, ssq_ref, p_ref, kv_ref, gate_ref, h_scr, *, tm, n_ctx):
    i = pl.program_id(1)
    j = pl.program_id(2)

    @pl.when(j == 0)
    def _():
        h = _modulated_norm(x_ref[0], g_ref[...], modl_ref[0, 0:1, :], modl_ref[0, 1:2, :],
                            modc_ref[0:1, :], modc_ref[1:2, :], i * tm, n_ctx)
        h_scr[...] = h.astype(BF16)

    acc = _dot(h_scr[...], w_ref[...])

    @pl.when(j == 0)
    def _():
        v = jax.nn.gelu(acc)
        v = _group_rms(v, bd_ref[...], gain_ref[0:1, :], nmask_ref[0:1, :])
        p_ref[0] = v.astype(BF16)

    def attn_tile(row):
        v = _group_rms(acc, bd_ref[...], gain_ref[row:row + 1, :], nmask_ref[row:row + 1, :])
        r = _rope(v, _tile4(cax_ref[...]), _tile4(sax_ref[...]))
        v = jnp.where(nmask_ref[row:row + 1, :] > 0.0, r, v) * post_ref[row:row + 1, :]
        p_ref[0] = v.astype(BF16)
        k = v[:, 256:384]
        vv = v[:, 384:512]
        lo = lax.broadcasted_iota(jnp.int32, k.shape, 1) < HEAD_DIM
        kr = pltpu.roll(k, HEAD_DIM, 1)
        vr = pltpu.roll(vv, HEAD_DIM, 1)
        dup = jnp.concatenate([jnp.where(lo, k, kr), jnp.where(lo, kr, k),
                               jnp.where(lo, vv, vr), jnp.where(lo, vr, vv)], axis=1)
        kv_ref[0] = dup.astype(BF16)

    @pl.when(j == 1)
    def _():
        attn_tile(1)

    @pl.when(j == 2)
    def _():
        attn_tile(2)

    @pl.when(j == 3)
    def _():
        v = _rope(acc, _tile4(csq_ref[...]), _tile4(ssq_ref[...])) * post_ref[3:4, :]
        p_ref[0] = v.astype(BF16)

    @pl.when(j == 4)
    def _():
        p_ref[0] = acc.astype(BF16)

    @pl.when(j >= N_MIX_TILES)
    def _():
        gate_ref[0] = acc.astype(BF16)


def _inproj(xc, modl, modc, g, w_bf, ep, tabs, n_ctx):
    b, t, d = xc.shape
    n_in = w_bf.shape[1]
    n_tiles = n_in // IN_TILE
    tm = _pick_tile(t, (768, 384, 256, 128))
    gain, nmask, post, bd = ep
    cax, sax, csq, ssq = tabs
    n_gate = n_in - N_MIX_TILES * IN_TILE
    tok = lambda bb, i, j: (i, 0)
    full2 = lambda bb, i, j: (0, 0)
    return pl.pallas_call(
        functools.partial(_inproj_kernel, tm=tm, n_ctx=n_ctx),
        grid=(b, t // tm, n_tiles),
        in_specs=[
            pl.BlockSpec((1, tm, d), lambda bb, i, j: (bb, i, 0)),
            pl.BlockSpec((1, 2, d), lambda bb, i, j: (bb, 0, 0)),
            pl.BlockSpec((2, d), full2),
            pl.BlockSpec((1, d), full2),
            pl.BlockSpec((d, IN_TILE), lambda bb, i, j: (0, j)),
            pl.BlockSpec((4, IN_TILE), full2),
            pl.BlockSpec((4, IN_TILE), full2),
            pl.BlockSpec((4, IN_TILE), full2),
            pl.BlockSpec((IN_TILE, IN_TILE), full2),
            pl.BlockSpec((tm, 128), tok),
            pl.BlockSpec((tm, 128), tok),
            pl.BlockSpec((tm, 128), tok),
            pl.BlockSpec((tm, 128), tok),
        ],
        out_specs=[
            pl.BlockSpec((1, tm, IN_TILE), lambda bb, i, j: (bb, i, jnp.minimum(j, N_MIX_TILES - 1))),
            pl.BlockSpec((1, tm, IN_TILE), lambda bb, i, j: (bb, i, jnp.clip(j - 1, 0, 1))),
            pl.BlockSpec((1, tm, IN_TILE), lambda bb, i, j: (bb, i, jnp.maximum(j - N_MIX_TILES, 0))),
        ],
        out_shape=[
            jax.ShapeDtypeStruct((b, t, N_MIX_TILES * IN_TILE), BF16),
            jax.ShapeDtypeStruct((b, t, 2 * IN_TILE), BF16),
            jax.ShapeDtypeStruct((b, t, n_gate), BF16),
        ],
        scratch_shapes=[pltpu.VMEM((tm, d), BF16)],
        compiler_params=_cparams(("parallel", "parallel", "arbitrary")),
        name="inproj",
    )(xc, modl, modc, g, w_bf, gain, nmask, post, bd, cax, sax, csq, ssq)


def _stack_heads(qa):
    qf = qa.astype(F32)
    lo = lax.broadcasted_iota(jnp.int32, qf.shape, 1) < HEAD_DIM
    return jnp.concatenate([jnp.where(lo, qf, 0.0), jnp.where(lo, 0.0, qf)], axis=0).astype(BF16)


def _unstack_heads(o2):
    n = o2.shape[0] // 2
    lo = lax.broadcasted_iota(jnp.int32, (n, o2.shape[1]), 1) < HEAD_DIM
    return jnp.where(lo, o2[:n], o2[n:])


def _win_kernel(sink_ref, q_ref, kc_ref, vc_ref, kp_ref, kx_ref, kn_ref, vp_ref, vx_ref, vn_ref, o_ref,
                *, n_ctx_blocks, n_blocks):
    n = pl.program_id(1)
    lc = n_ctx_blocks * BLOCK
    w = lc + 3 * BLOCK
    is_lat = n >= n_ctx_blocks
    prev_ok = n > n_ctx_blocks
    next_ok = n < n_blocks - 1
    row = lax.broadcasted_iota(jnp.int32, (BLOCK, w), 0)
    col = lax.broadcasted_iota(jnp.int32, (BLOCK, w), 1)
    c = col - lc
    lo_b = jnp.where(prev_ok, row, BLOCK)
    hi_b = jnp.where(next_ok, row + 2 * BLOCK, 2 * BLOCK - 1)
    lat_bias = jnp.where(is_lat, 0.0, NEG_INF)
    bias = jnp.where(col < lc, 0.0, jnp.where(c >= lo_b, jnp.where(c <= hi_b, lat_bias, NEG_INF), NEG_INF))
    bias2 = jnp.concatenate([bias, bias], axis=0)
    rid = lax.broadcasted_iota(jnp.int32, (2 * BLOCK, 1), 0)
    outs = []
    for a in range(2):
        sl = slice(128 * a, 128 * (a + 1))
        q2 = _stack_heads(q_ref[0, :, sl])
        keys = jnp.concatenate([kc_ref[0, :, sl], kp_ref[0, :, sl], kx_ref[0, :, sl], kn_ref[0, :, sl]], axis=0)
        vals = jnp.concatenate([vc_ref[0, :, sl], vp_ref[0, :, sl], vx_ref[0, :, sl], vn_ref[0, :, sl]], axis=0)
        s = _dot_nt(q2, keys) + bias2
        sk = jnp.where(rid < BLOCK, sink_ref[2 * a], sink_ref[2 * a + 1])
        m = jnp.maximum(jnp.max(s, axis=-1, keepdims=True), sk)
        e = jnp.exp(s - m)
        den = jnp.sum(e, axis=-1, keepdims=True) + jnp.exp(sk - m)
        p = (e / den).astype(BF16)
        outs.append(_unstack_heads(_dot(p, vals)))
    o_ref[0] = jnp.concatenate(outs, axis=1).astype(BF16)


def _window_attention(p0, kv, sink, n_ctx):
    b, t, _ = p0.shape
    nb = t // BLOCK
    ncb = n_ctx // BLOCK
    blk = lambda cidx: pl.BlockSpec((1, BLOCK, 256), lambda bb, n: (bb, n, cidx))
    prv = lambda cidx: pl.BlockSpec((1, BLOCK, 256), lambda bb, n: (bb, jnp.maximum(n - 1, 0), cidx))
    nxt = lambda cidx: pl.BlockSpec((1, BLOCK, 256), lambda bb, n: (bb, jnp.minimum(n + 1, nb - 1), cidx))
    ctx = lambda cidx: pl.BlockSpec((1, n_ctx, 256), lambda bb, n: (bb, 0, cidx))
    return pl.pallas_call(
        functools.partial(_win_kernel, n_ctx_blocks=ncb, n_blocks=nb),
        grid=(b, nb),
        in_specs=[
            pl.BlockSpec(memory_space=pltpu.SMEM),
            blk(2),
            ctx(0), ctx(1),
            prv(0), blk(0), nxt(0),
            prv(1), blk(1), nxt(1),
        ],
        out_specs=pl.BlockSpec((1, BLOCK, 256), lambda bb, n: (bb, n, 0)),
        out_shape=jax.ShapeDtypeStruct((b, t, 256), BF16),
        compiler_params=_cparams(("parallel", "parallel")),
        name="window_attn",
    )(sink, p0, kv, kv, kv, kv, kv, kv, kv, kv)


def _glb_kernel(q_ref, k_ref, v_ref, o_ref, *, tq, tk, n_ctx, t_all):
    i = pl.program_id(1)

    def attend(n_chunks, size):
        outs = []
        for a in range(2):
            sl = slice(128 * a, 128 * (a + 1))
            q2 = _stack_heads(q_ref[0, :, sl])

            def body(ci, carry):
                m, l, acc = carry
                start = pl.multiple_of(ci * size, size)
                k = k_ref[0, pl.ds(start, size), sl]
                v = v_ref[0, pl.ds(start, size), sl]
                s = _dot_nt(q2, k)
                m_new = jnp.maximum(m, jnp.max(s, axis=-1, keepdims=True))
                alpha = jnp.exp(m - m_new)
                p = jnp.exp(s - m_new)
                l = alpha * l + jnp.sum(p, axis=-1, keepdims=True)
                acc = alpha * acc + _dot(p.astype(BF16), v)
                return m_new, l, acc

            init = (jnp.full((2 * tq, 1), NEG_INF, F32), jnp.zeros((2 * tq, 1), F32),
                    jnp.zeros((2 * tq, 128), F32))
            m, l, acc = lax.fori_loop(0, n_chunks, body, init)
            outs.append(_unstack_heads(acc / l))
        o_ref[0] = jnp.concatenate(outs, axis=1).astype(BF16)

    @pl.when((i + 1) * tq <= n_ctx)
    def _():
        attend(1, n_ctx)

    @pl.when((i + 1) * tq > n_ctx)
    def _():
        attend(t_all // tk, tk)


def _global_attention(p0, kv, n_ctx):
    b, t, _ = p0.shape
    tq = _pick_tile(n_ctx, (256, 128))
    tk = _pick_tile(t, (768, 384, 256, 128))
    return pl.pallas_call(
        functools.partial(_glb_kernel, tq=tq, tk=tk, n_ctx=n_ctx, t_all=t),
        grid=(b, t // tq),
        in_specs=[
            pl.BlockSpec((1, tq, 256), lambda bb, i: (bb, i, 4)),
            pl.BlockSpec((1, t, 256), lambda bb, i: (bb, 0, 2)),
            pl.BlockSpec((1, t, 256), lambda bb, i: (bb, 0, 3)),
        ],
        out_specs=pl.BlockSpec((1, tq, 256), lambda bb, i: (bb, i, 0)),
        out_shape=jax.ShapeDtypeStruct((b, t, 256), BF16),
        compiler_params=_cparams(("parallel", "arbitrary")),
        name="global_attn",
    )(p0, kv, kv)


_TAB_D, _TAB_Q, _TAB_K, _TAB_C, _TAB_ROWS = 0, 256, 384, 512, 640


def _ret_kernel(dec_ref, qf_ref, kf_ref, vf_ref, qb_ref, kb_ref, vb_ref, of_ref, ob_ref, st_scr, tab_scr,
                *, batch):
    i = pl.program_id(0)
    r = lax.broadcasted_iota(jnp.int32, (BLOCK, BLOCK), 0)
    c = lax.broadcasted_iota(jnp.int32, (BLOCK, BLOCK), 1)
    lane_lo = c < HEAD_DIM
    same_head = (r < HEAD_DIM) == lane_lo

    @pl.when(i == 0)
    def _():
        st_scr[...] = jnp.zeros_like(st_scr)
        x = dec_ref[...]
        lg = jnp.where(x >= 0.0, -jnp.log(1.0 + jnp.exp(-x)), x - jnp.log(1.0 + jnp.exp(x)))
        rf = r.astype(F32)
        cf = c.astype(F32)
        for d in range(2):
            for pair in range(2):
                le = lg[4 * d + 2 * pair:4 * d + 2 * pair + 1, :]
                lo_ = lg[4 * d + 2 * pair + 1:4 * d + 2 * pair + 2, :]
                lp = jnp.where(lane_lo[0:1, :], le, lo_)
                if d == 0:
                    rel = rf - cf
                    qpow = rf + 1.0
                    kpow = (BLOCK - 1.0) - rf
                else:
                    rel = cf - rf - 1.0
                    qpow = (BLOCK - 1.0) - rf
                    kpow = rf
                msk = rel >= 0.0
                relc = jnp.where(msk, rel, 0.0)
                tab_scr[d, pair, _TAB_D:_TAB_D + BLOCK, :] = jnp.where(msk, jnp.exp(le * relc), 0.0)
                tab_scr[d, pair, _TAB_D + BLOCK:_TAB_Q, :] = jnp.where(msk, jnp.exp(lo_ * relc), 0.0)
                tab_scr[d, pair, _TAB_Q:_TAB_K, :] = jnp.exp(lp * qpow)
                tab_scr[d, pair, _TAB_K:_TAB_C, :] = jnp.exp(lp * kpow)
                tab_scr[d, pair, _TAB_C:_TAB_ROWS, :] = jnp.where(
                    r < HEAD_DIM, jnp.exp(le * float(BLOCK)), jnp.exp(lo_ * float(BLOCK)))

    for d, (q_ref, k_ref, v_ref, o_ref) in enumerate(((qf_ref, kf_ref, vf_ref, of_ref),
                                                     (qb_ref, kb_ref, vb_ref, ob_ref))):
        for bb in range(batch):
            for pair in range(2):
                sl = slice(128 * pair, 128 * (pair + 1))
                q = q_ref[bb, :, sl].astype(F32)
                k = k_ref[bb, :, sl].astype(F32)
                v = v_ref[bb, :, sl]
                q2 = jnp.concatenate([jnp.where(lane_lo, q, 0.0), jnp.where(lane_lo, 0.0, q)], axis=0)
                qk = _dot_nt(q2.astype(BF16), k.astype(BF16)) * tab_scr[d, pair, _TAB_D:_TAB_Q, :]
                o_intra = _unstack_heads(_dot(qk.astype(BF16), v))
                s = st_scr[d, bb, pair]
                qd = q * tab_scr[d, pair, _TAB_Q:_TAB_K, :]
                o_ref[bb, :, sl] = o_intra + _dot(qd.astype(BF16), s.astype(BF16))
                kd = k * tab_scr[d, pair, _TAB_K:_TAB_C, :]
                upd = _dot(kd.T.astype(BF16), v)
                st_scr[d, bb, pair] = jnp.where(same_head, tab_scr[d, pair, _TAB_C:_TAB_ROWS, :] * s + upd, 0.0)


def _retention(p0, dec, n_ctx):
    b, t, _ = p0.shape
    nb = t // BLOCK
    ncb = n_ctx // BLOCK
    bwd = lambda i: jnp.where(i < ncb, ncb - 1 - i, nb - 1 + ncb - i)
    fspec = lambda cidx: pl.BlockSpec((b, BLOCK, 256), lambda i: (0, i, cidx))
    bspec = lambda cidx: pl.BlockSpec((b, BLOCK, 256), lambda i: (0, bwd(i), cidx))
    return pl.pallas_call(
        functools.partial(_ret_kernel, batch=b),
        grid=(nb,),
        in_specs=[
            pl.BlockSpec((8, 128), lambda i: (0, 0)),
            fspec(6), fspec(7), fspec(8),
            bspec(6), bspec(7), bspec(8),
        ],
        out_specs=[
            pl.BlockSpec((b, BLOCK, 256), lambda i: (0, i, 0)),
            pl.BlockSpec((b, BLOCK, 256), lambda i: (0, bwd(i), 0)),
        ],
        out_shape=[jax.ShapeDtypeStruct((b, t, 256), F32), jax.ShapeDtypeStruct((b, t, 256), F32)],
        scratch_shapes=[pltpu.VMEM((2, b, 2, BLOCK, BLOCK), F32),
                        pltpu.VMEM((2, 2, _TAB_ROWS, BLOCK), F32)],
        compiler_params=_cparams(("arbitrary",)),
        name="retention",
    )(dec, p0, p0, p0, p0, p0, p0)


def _merge_kernel(x_ref, uv_ref, batt_ref, catt_ref, of_ref, ob_ref, rg_ref, gate_ref, gl_ref, gc_ref,
                  ws_ref, bs_ref, gn_ref, bd_ref, wb_ref, wo_ref, o_ref, *, tm, n_ctx):
    i = pl.program_id(1)
    grp = jnp.right_shift(lax.broadcasted_iota(jnp.int32, (BLOCK, 256), 1), 6)
    a_chunks = []
    for ch in range(tm // BLOCK):
        rows = slice(ch * BLOCK, (ch + 1) * BLOCK)
        vn = uv_ref[0, rows, 256:512]
        mixed = bs_ref[...]
        for g in range(4):
            mixed = mixed + jnp.where(grp == g, _dot(ws_ref[g], vn), 0.0)
        a_chunks.append((uv_ref[0, rows, 0:256].astype(F32) * mixed).astype(BF16))
    a_br = jnp.concatenate(a_chunks, axis=0)
    o = of_ref[0] + ob_ref[0]
    mean = _split_dot(o, bd_ref[...]) * (1.0 / HEAD_DIM)
    oc = o - mean
    var = _split_dot(oc * oc, bd_ref[...]) * (1.0 / HEAD_DIM)
    rg = rg_ref[0].astype(F32)
    d_br = (oc * lax.rsqrt(var + EPS) * gn_ref[...] * (rg * jax.nn.sigmoid(rg))).astype(BF16)
    branches = (a_br, batt_ref[0], catt_ref[0], d_br)
    d_model = x_ref.shape[-1]
    acc = None
    for n in range(N_BRANCH):
        gate = jax.nn.sigmoid(gate_ref[0, :, n * d_model:(n + 1) * d_model].astype(F32))
        term = gate * _dot(branches[n], wb_ref[n])
        acc = term if acc is None else acc + term
    m = _dot(acc.astype(BF16), wo_ref[...])
    row = i * tm + lax.broadcasted_iota(jnp.int32, (tm, 1), 0)
    g1 = jnp.where(row < n_ctx, gc_ref[...], gl_ref[0])
    o_ref[0] = x_ref[0] + g1 * m


def _merge(xc, p0, b_att, c_att, o_f, o_b, gates, g1_l, g1_c, ws_bf, bias, gn, bd, wb_bf, wo_bf, n_ctx):
    b, t, d = xc.shape
    tm = _pick_tile(t, (384, 256, 128))
    tok = lambda w, cidx: pl.BlockSpec((1, tm, w), lambda bb, i: (bb, i, cidx))
    full = lambda shape: pl.BlockSpec(shape, lambda bb, i: (0,) * len(shape))
    return pl.pallas_call(
        functools.partial(_merge_kernel, tm=tm, n_ctx=n_ctx),
        grid=(b, t // tm),
        in_specs=[
            tok(d, 0),
            tok(512, 0),
            tok(256, 0), tok(256, 0),
            tok(256, 0), tok(256, 0),
            tok(256, 9),
            tok(N_BRANCH * d, 0),
            pl.BlockSpec((1, 1, d), lambda bb, i: (bb, 0, 0)),
            full((1, d)),
            full((4, BLOCK, BLOCK)), full((BLOCK, 256)), full((1, 256)), full((256, 256)),
            full((N_BRANCH, BRANCH_W, d)), full((d, d)),
        ],
        out_specs=tok(d, 0),
        out_shape=jax.ShapeDtypeStruct((b, t, d), F32),
        compiler_params=_cparams(("parallel", "parallel")),
        name="merge",
    )(xc, p0, b_att, c_att, o_f, o_b, p0, gates, g1_l, g1_c, ws_bf, bias, gn, bd, wb_bf, wo_bf)


def _ffn_kernel(x_ref, modl_ref, modc_ref, g_ref, w1_ref, w3_ref, w2_ref, o_ref, h_scr, acc_scr, *, tm, n_ctx):
    i = pl.program_id(1)
    k = pl.program_id(2)

    @pl.when(k == 0)
    def _():
        h = _modulated_norm(x_ref[0], g_ref[...], modl_ref[0, 0:1, :], modl_ref[0, 1:2, :],
                            modc_ref[0:1, :], modc_ref[1:2, :], i * tm, n_ctx)
        h_scr[...] = h.astype(BF16)
        acc_scr[...] = jnp.zeros_like(acc_scr)

    h = h_scr[...]
    a = _dot(h, w1_ref[...])
    bgate = _dot(h, w3_ref[...])
    act = (a * jax.nn.sigmoid(a) * bgate).astype(BF16)
    acc_scr[...] += _dot(act, w2_ref[...])

    @pl.when(k == pl.num_programs(2) - 1)
    def _():
        row = i * tm + lax.broadcasted_iota(jnp.int32, (tm, 1), 0)
        g2 = jnp.where(row < n_ctx, modc_ref[2:3, :], modl_ref[0, 2:3, :])
        o_ref[0] = x_ref[0] + g2 * acc_scr[...]


def _ffn(xc, modl, modc, g, w1, w3, w2, n_ctx):
    b, t, d = xc.shape
    dff = w1.shape[1]
    tm = _pick_tile(t, (768, 384, 256, 128))
    tf = _pick_tile(dff, (1408, 512, 256, 128))
    return pl.pallas_call(
        functools.partial(_ffn_kernel, tm=tm, n_ctx=n_ctx),
        grid=(b, t // tm, dff // tf),
        in_specs=[
            pl.BlockSpec((1, tm, d), lambda bb, i, k: (bb, i, 0)),
            pl.BlockSpec((1, 3, d), lambda bb, i, k: (bb, 0, 0)),
            pl.BlockSpec((3, d), lambda bb, i, k: (0, 0)),
            pl.BlockSpec((1, d), lambda bb, i, k: (0, 0)),
            pl.BlockSpec((d, tf), lambda bb, i, k: (0, k)),
            pl.BlockSpec((d, tf), lambda bb, i, k: (0, k)),
            pl.BlockSpec((tf, d), lambda bb, i, k: (k, 0)),
        ],
        out_specs=pl.BlockSpec((1, tm, d), lambda bb, i, k: (bb, i, 0)),
        out_shape=jax.ShapeDtypeStruct((b, t, d), F32),
        scratch_shapes=[pltpu.VMEM((tm, d), BF16), pltpu.VMEM((tm, d), F32)],
        compiler_params=_cparams(("parallel", "parallel", "arbitrary")),
        name="ffn",
    )(xc, modl, modc, g, w1, w3, w2)


def _moe_kernel(x_ref, modl_ref, modc_ref, g_ref, rw_ref, rb_ref, w1_ref, w3_ref, w2_ref, o_ref,
                h_scr, comb_scr, acc_scr, eacc_scr, *, tm, n_ctx, n_k):
    i = pl.program_id(1)
    s = pl.program_id(2)
    e = s // n_k
    k = s % n_k
    lane = lax.broadcasted_iota(jnp.int32, (tm, 128), 1)

    @pl.when(s == 0)
    def _():
        h = _modulated_norm(x_ref[0], g_ref[...], modl_ref[0, 0:1, :], modl_ref[0, 1:2, :],
                            modc_ref[0:1, :], modc_ref[1:2, :], i * tm, n_ctx)
        h_scr[...] = h.astype(BF16)
        acc_scr[...] = jnp.zeros_like(acc_scr)
        h_hi = h.astype(BF16)
        h_lo = (h - h_hi.astype(F32)).astype(BF16)
        rw = rw_ref[...]
        rw_hi = rw.astype(BF16)
        rw_lo = (rw - rw_hi.astype(F32)).astype(BF16)
        logits = _dot(h_hi, rw_hi) + _dot(h_hi, rw_lo) + _dot(h_lo, rw_hi) + rb_ref[...]
        logits = jnp.where(lane < N_EXPERTS, logits, NEG_INF)
        v0 = jnp.max(logits, axis=-1, keepdims=True)
        i0 = jnp.min(jnp.where(logits == v0, lane, 128), axis=-1, keepdims=True)
        rest = jnp.where(lane == i0, NEG_INF, logits)
        v1 = jnp.max(rest, axis=-1, keepdims=True)
        i1 = jnp.min(jnp.where(rest == v1, lane, 128), axis=-1, keepdims=True)
        e1 = jnp.exp(v1 - v0)
        p0 = 1.0 / (1.0 + e1)
        comb_scr[...] = jnp.where(lane == i0, p0, jnp.where(lane == i1, e1 * p0, 0.0))

    @pl.when(k == 0)
    def _():
        eacc_scr[...] = jnp.zeros_like(eacc_scr)

    h = h_scr[...]
    a = _dot(h, w1_ref[0])
    bgate = _dot(h, w3_ref[0])
    act = (a * jax.nn.sigmoid(a) * bgate).astype(BF16)
    eacc_scr[...] += _dot(act, w2_ref[0])

    @pl.when(k == n_k - 1)
    def _():
        ce = jnp.sum(jnp.where(lane == e, comb_scr[...], 0.0), axis=-1, keepdims=True)
        acc_scr[...] += ce * eacc_scr[...]

    @pl.when(s == pl.num_programs(2) - 1)
    def _():
        row = i * tm + lax.broadcasted_iota(jnp.int32, (tm, 1), 0)
        g2 = jnp.where(row < n_ctx, modc_ref[2:3, :], modl_ref[0, 2:3, :])
        o_ref[0] = x_ref[0] + g2 * acc_scr[...]


def _moe(xc, modl, modc, g, rw_pad, rb_pad, w1, w3, w2, n_ctx):
    b, t, d = xc.shape
    n_e, _, dff = w1.shape
    tm = _pick_tile(t, (768, 384, 256, 128))
    tf = _pick_tile(dff, (896, 256, 128))
    n_k = dff // tf
    return pl.pallas_call(
        functools.partial(_moe_kernel, tm=tm, n_ctx=n_ctx, n_k=n_k),
        grid=(b, t // tm, n_e * n_k),
        in_specs=[
            pl.BlockSpec((1, tm, d), lambda bb, i, s: (bb, i, 0)),
            pl.BlockSpec((1, 3, d), lambda bb, i, s: (bb, 0, 0)),
            pl.BlockSpec((3, d), lambda bb, i, s: (0, 0)),
            pl.BlockSpec((1, d), lambda bb, i, s: (0, 0)),
            pl.BlockSpec((d, 128), lambda bb, i, s: (0, 0)),
            pl.BlockSpec((1, 128), lambda bb, i, s: (0, 0)),
            pl.BlockSpec((1, d, tf), lambda bb, i, s: (s // n_k, 0, s % n_k)),
            pl.BlockSpec((1, d, tf), lambda bb, i, s: (s // n_k, 0, s % n_k)),
            pl.BlockSpec((1, tf, d), lambda bb, i, s: (s // n_k, s % n_k, 0)),
        ],
        out_specs=pl.BlockSpec((1, tm, d), lambda bb, i, s: (bb, i, 0)),
        out_shape=jax.ShapeDtypeStruct((b, t, d), F32),
        scratch_shapes=[pltpu.VMEM((tm, d), BF16), pltpu.VMEM((tm, 128), F32),
                        pltpu.VMEM((tm, d), F32), pltpu.VMEM((tm, d), F32)],
        compiler_params=_cparams(("parallel", "parallel", "arbitrary")),
        name="moe",
    )(xc, modl, modc, g, rw_pad, rb_pad, w1, w3, w2)


def _rope_tables(n_ctx, s):
    def widen(cos, sin):
        c64 = jnp.concatenate([cos, cos], axis=-1)
        s64 = jnp.concatenate([-sin, sin], axis=-1)
        return jnp.tile(c64, (1, 2)), jnp.tile(s64, (1, 2))

    n_rows = s // GRID_W
    row = jnp.repeat(jnp.arange(n_rows, dtype=F32), GRID_W)
    col = jnp.tile(jnp.arange(GRID_W, dtype=F32), n_rows)
    n_freq = HEAD_DIM // 4
    inv = ROPE_BASE ** (-jnp.arange(n_freq, dtype=F32) / n_freq)
    ang = jnp.concatenate([row[:, None] * inv, col[:, None] * inv], axis=-1)
    cos_ax = jnp.concatenate([jnp.ones((n_ctx, HEAD_DIM // 2), F32), jnp.cos(ang)], axis=0)
    sin_ax = jnp.concatenate([jnp.zeros((n_ctx, HEAD_DIM // 2), F32), jnp.sin(ang)], axis=0)
    n_freq = HEAD_DIM // 2
    inv = ROPE_BASE ** (-jnp.arange(n_freq, dtype=F32) / n_freq)
    pos = jnp.arange(n_ctx + s, dtype=F32)
    ang = pos[:, None] * inv
    return widen(cos_ax, sin_ax) + widen(jnp.cos(ang), jnp.sin(ang))


def _block_diag_ones(n):
    g = jnp.arange(n) // HEAD_DIM
    return (g[:, None] == g[None, :]).astype(BF16)


def kernel(x, c, ctx, c_ctx, w_mod, b_mod, g_norm1, g_norm2, w_in, mlp_g_v, mlp_w_s, mlp_b_s, win_q_norm,
           win_k_norm, win_sink, glb_q_norm, glb_k_norm, ret_decay, ret_gn, w_branch, w_out, ffn_w1, ffn_w3,
           ffn_w2, router_w, router_b, moe_w1, moe_w3, moe_w2):
    batch, s, d = x.shape
    n_ctx = ctx.shape[1]
    depth = w_in.shape[0]
    assert batch + 1 <= 8 and n_ctx % BLOCK == 0 and s % BLOCK == 0

    xc = jnp.concatenate([ctx, x], axis=1)
    cvec = jnp.zeros((8, d), F32).at[:batch].set(c).at[batch].set(c_ctx)
    mod = _modulation(cvec, w_mod, b_mod)
    mod = mod.reshape(depth, 8, 6, d)

    tabs = _rope_tables(n_ctx, s)
    bd512 = _block_diag_ones(IN_TILE)
    bd256 = _block_diag_ones(256)
    ones = lambda n: jnp.ones((n,), F32)
    zeros = lambda n: jnp.zeros((n,), F32)
    scale = HEAD_DIM ** -0.5

    for layer in range(depth):
        ml = mod[layer]
        modl1 = ml[:batch, 0:2]
        modc1 = ml[batch, 0:2]
        g1_l = ml[:batch, 2:3]
        g1_c = ml[batch, 2:3]
        modl2 = ml[:batch, 3:6]
        modc2 = ml[batch, 3:6]

        def qkv_rows(gq, gk):
            return jnp.concatenate([jnp.tile(gq, 4), jnp.tile(gk, 2), ones(128)])

        gain = jnp.stack([
            jnp.concatenate([ones(256), mlp_g_v[layer].reshape(-1)]),
            qkv_rows(win_q_norm[layer], win_k_norm[layer]),
            qkv_rows(glb_q_norm[layer], glb_k_norm[layer]),
            ones(IN_TILE)])
        qk_mask = jnp.concatenate([ones(384), zeros(128)])
        nmask = jnp.stack([jnp.concatenate([zeros(256), ones(256)]), qk_mask, qk_mask, zeros(IN_TILE)])
        q_scale = jnp.concatenate([jnp.full((256,), scale, F32), ones(256)])
        post = jnp.stack([ones(IN_TILE), q_scale, q_scale,
                          jnp.concatenate([ones(256), jnp.full((256,), scale, F32)])])

        p0, kv, gates = _inproj(xc, modl1, modc1, g_norm1[layer][None, :], w_in[layer].astype(BF16),
                                (gain, nmask, post, bd512), tabs, n_ctx)
        b_att = _window_attention(p0, kv, win_sink[layer], n_ctx)
        c_att = _global_attention(p0, kv, n_ctx)
        dec = jnp.broadcast_to(ret_decay[layer].reshape(-1, 1), (8, 128))
        o_f, o_b = _retention(p0, dec, n_ctx)
        bias = jnp.repeat(mlp_b_s[layer].T, HEAD_DIM, axis=1)
        xc = _merge(xc, p0, b_att, c_att, o_f, o_b, gates, g1_l, g1_c, mlp_w_s[layer].astype(BF16), bias,
                    ret_gn[layer].reshape(1, -1), bd256, w_branch[layer].astype(BF16),
                    w_out[layer].astype(BF16), n_ctx)
        j = layer // 2
        g2 = g_norm2[layer][None, :]
        if layer % 2 == 0:
            xc = _ffn(xc, modl2, modc2, g2, ffn_w1[j].astype(BF16), ffn_w3[j].astype(BF16),
                      ffn_w2[j].astype(BF16), n_ctx)
        else:
            rw_pad = jnp.zeros((d, 128), F32).at[:, :N_EXPERTS].set(router_w[j])
            rb_pad = jnp.zeros((1, 128), F32).at[0, :N_EXPERTS].set(router_b[j])
            xc = _moe(xc, modl2, modc2, g2, rw_pad, rb_pad, moe_w1[j].astype(BF16), moe_w3[j].astype(BF16),
                      moe_w2[j].astype(BF16), n_ctx)
    return xc[:, n_ctx:]
```

```python
import functools

import jax
import jax.numpy as jnp
from jax import lax
from jax.experimental import pallas as pl
from jax.experimental.pallas import tpu as pltpu

F32 = jnp.float32
BF16 = jnp.bfloat16

HEAD_DIM = 64
BLOCK = 128
GRID_W = 64
ROPE_BASE = 10000.0
N_BRANCH = 4
BRANCH_W = 256
N_EXPERTS = 8
EPS = 1e-6
NEG_INF = -1e30
LOG2_E = 1.4426950408889634
IN_TILE = 512
N_MIX_TILES = 5
VMEM_LIMIT = 56 * 1024 * 1024


def _cparams(sem):
    return pltpu.CompilerParams(dimension_semantics=sem, vmem_limit_bytes=VMEM_LIMIT)


def _dot(a, b):
    return jnp.dot(a, b, preferred_element_type=F32)


def _dot_nt(a, b):
    return lax.dot_general(a, b, (((1,), (1,)), ((), ())), preferred_element_type=F32)


def _split_dot(a, b_bf16):
    hi = a.astype(BF16)
    lo = (a - hi.astype(F32)).astype(BF16)
    return _dot(hi, b_bf16) + _dot(lo, b_bf16)


def _pick_tile(n, candidates):
    for c in candidates:
        if n % c == 0:
            return c
    raise ValueError(f"no tile for {n} in {candidates}")


def _modulated_norm(x, g, sh_l, sc_l, sh_c, sc_c, row0, n_ctx):
    tm = x.shape[0]
    ms = jnp.mean(x * x, axis=-1, keepdims=True)
    y = x * lax.rsqrt(ms + EPS) * g
    row = row0 + lax.broadcasted_iota(jnp.int32, (tm, 1), 0)
    is_ctx = row < n_ctx
    sh = jnp.where(is_ctx, sh_c, sh_l)
    sc = jnp.where(is_ctx, sc_c, sc_l)
    return y * (1.0 + sc) + sh


def _mod_kernel(c_ref, w_ref, b_ref, o_ref):
    c = c_ref[...]
    s = c * jax.nn.sigmoid(c)
    s_hi = s.astype(BF16)
    s_lo = (s - s_hi.astype(F32)).astype(BF16)
    w = w_ref[0]
    w_hi = w.astype(BF16)
    w_lo = (w - w_hi.astype(F32)).astype(BF16)
    o_ref[0] = _dot(s_hi, w_hi) + _dot(s_hi, w_lo) + _dot(s_lo, w_hi) + b_ref[0]


def _modulation(cvec, w_mod, b_mod):
    depth, d, n = w_mod.shape
    tn = _pick_tile(n, (1536, 1024, 512, 128))
    return pl.pallas_call(
        _mod_kernel,
        grid=(depth, n // tn),
        in_specs=[
            pl.BlockSpec((8, d), lambda l, j: (0, 0)),
            pl.BlockSpec((1, d, tn), lambda l, j: (l, 0, j)),
            pl.BlockSpec((1, 1, tn), lambda l, j: (l, 0, j)),
        ],
        out_specs=pl.BlockSpec((1, 8, tn), lambda l, j: (l, 0, j)),
        out_shape=jax.ShapeDtypeStruct((depth, 8, n), F32),
        compiler_params=_cparams(("parallel", "parallel")),
        name="modulation",
    )(cvec, w_mod, b_mod.reshape(depth, 1, n))


def _group_rms(v, bd, gain, nmask):
    sq = (v * v).astype(BF16)
    half = bd.shape[0]
    gs = jnp.concatenate([_dot(sq[:, c:c + half], bd) for c in range(0, v.shape[-1], half)], axis=1)
    n = v * lax.rsqrt(gs * (1.0 / HEAD_DIM) + EPS) * gain
    return jnp.where(nmask > 0.0, n, v)


def _rope(v, cos, sin):
    w = v.shape[-1]
    lane = lax.broadcasted_iota(jnp.int32, v.shape, 1)
    first = (lane & 32) == 0
    partner = jnp.where(first, pltpu.roll(v, w - 32, 1), pltpu.roll(v, 32, 1))
    return v * cos + partner * sin


def _tile4(t):
    return jnp.concatenate([t, t, t, t], axis=1)


def _inproj_kernel(x_ref, modl_ref, modc_ref, g_ref, w_ref, gain_ref, nmask_ref, post_ref, bd_ref,
                   cax_ref, sax_ref, csq_ref, ssq_ref, p_ref, kv_ref, gate_ref, h_scr, *, tm, n_ctx):
    i = pl.program_id(1)
    j = pl.program_id(2)

    @pl.when(j == 0)
    def _():
        h = _modulated_norm(x_ref[0], g_ref[...], modl_ref[0, 0:1, :], modl_ref[0, 1:2, :],
                            modc_ref[0:1, :], modc_ref[1:2, :], i * tm, n_ctx)
        h_scr[...] = h.astype(BF16)

    acc = _dot(h_scr[...], w_ref[...])

    @pl.when(j == 0)
    def _():
        v = jax.nn.gelu(acc)
        v = _group_rms(v, bd_ref[...], gain_ref[0:1, :], nmask_ref[0:1, :])
        p_ref[0] = v.astype(BF16)

    def attn_tile(row, duplicate):
        v = _group_rms(acc, bd_ref[...], gain_ref[row:row + 1, :], nmask_ref[row:row + 1, :])
        r = _rope(v, _tile4(cax_ref[...]), _tile4(sax_ref[...]))
        v = jnp.where(nmask_ref[row:row + 1, :] > 0.0, r, v) * post_ref[row:row + 1, :]
        p_ref[0] = v.astype(BF16)
        k = v[:, 256:384]
        vv = v[:, 384:512]
        lane = lax.broadcasted_iota(jnp.int32, k.shape, 1)
        lo = lane < HEAD_DIM
        kr = pltpu.roll(k, HEAD_DIM, 1)
        vr = pltpu.roll(vv, HEAD_DIM, 1)
        if duplicate:
            parts = [jnp.where(lo, k, kr), jnp.where(lo, kr, k), jnp.where(lo, vv, vr), jnp.where(lo, vr, vv)]
        else:
            pad = jnp.where(lane == HEAD_DIM, 1.0, 0.0)
            parts = [jnp.where(lo, k, pad), jnp.where(lo, kr, pad), jnp.where(lo, vv, pad), jnp.where(lo, vr, pad)]
        kv_ref[0] = jnp.concatenate(parts, axis=1).astype(BF16)

    @pl.when(j == 1)
    def _():
        attn_tile(1, True)

    @pl.when(j == 2)
    def _():
        attn_tile(2, False)

    @pl.when(j == 3)
    def _():
        v = _rope(acc, _tile4(csq_ref[...]), _tile4(ssq_ref[...])) * post_ref[3:4, :]
        p_ref[0] = v.astype(BF16)

    @pl.when(j == 4)
    def _():
        p_ref[0] = acc.astype(BF16)

    @pl.when(j >= N_MIX_TILES)
    def _():
        gate_ref[0] = acc.astype(BF16)


def _inproj(xc, modl, modc, g, w_bf, ep, tabs, n_ctx):
    b, t, d = xc.shape
    n_in = w_bf.shape[1]
    n_tiles = n_in // IN_TILE
    tm = _pick_tile(t, (1408, 768, 384, 256, 128))
    gain, nmask, post, bd = ep
    cax, sax, csq, ssq = tabs
    n_gate = n_in - N_MIX_TILES * IN_TILE
    tok = lambda bb, i, j: (i, 0)
    full2 = lambda bb, i, j: (0, 0)
    return pl.pallas_call(
        functools.partial(_inproj_kernel, tm=tm, n_ctx=n_ctx),
        grid=(b, t // tm, n_tiles),
        in_specs=[
            pl.BlockSpec((1, tm, d), lambda bb, i, j: (bb, i, 0)),
            pl.BlockSpec((1, 2, d), lambda bb, i, j: (bb, 0, 0)),
            pl.BlockSpec((2, d), full2),
            pl.BlockSpec((1, d), full2),
            pl.BlockSpec((d, IN_TILE), lambda bb, i, j: (0, j)),
            pl.BlockSpec((4, IN_TILE), full2),
            pl.BlockSpec((4, IN_TILE), full2),
            pl.BlockSpec((4, IN_TILE), full2),
            pl.BlockSpec((256, 256), full2),
            pl.BlockSpec((tm, 128), tok),
            pl.BlockSpec((tm, 128), tok),
            pl.BlockSpec((tm, 128), tok),
            pl.BlockSpec((tm, 128), tok),
        ],
        out_specs=[
            pl.BlockSpec((1, tm, IN_TILE), lambda bb, i, j: (bb, i, jnp.minimum(j, N_MIX_TILES - 1))),
            pl.BlockSpec((1, tm, IN_TILE), lambda bb, i, j: (bb, i, jnp.clip(j - 1, 0, 1))),
            pl.BlockSpec((1, tm, IN_TILE), lambda bb, i, j: (bb, i, jnp.maximum(j - N_MIX_TILES, 0))),
        ],
        out_shape=[
            jax.ShapeDtypeStruct((b, t, N_MIX_TILES * IN_TILE), BF16),
            jax.ShapeDtypeStruct((b, t, 2 * IN_TILE), BF16),
            jax.ShapeDtypeStruct((b, t, n_gate), BF16),
        ],
        scratch_shapes=[pltpu.VMEM((tm, d), BF16)],
        compiler_params=_cparams(("parallel", "parallel", "arbitrary")),
        name="inproj",
    )(xc, modl, modc, g, w_bf, gain, nmask, post, bd, cax, sax, csq, ssq)


def _stack_heads(qa):
    qf = qa.astype(F32)
    lo = lax.broadcasted_iota(jnp.int32, qf.shape, 1) < HEAD_DIM
    return jnp.concatenate([jnp.where(lo, qf, 0.0), jnp.where(lo, 0.0, qf)], axis=0).astype(BF16)


def _unstack_heads(o2):
    n = o2.shape[0] // 2
    lo = lax.broadcasted_iota(jnp.int32, (n, o2.shape[1]), 1) < HEAD_DIM
    return jnp.where(lo, o2[:n], o2[n:])


def _win_kernel(sink_ref, q_ref, kc_ref, vc_ref, kp_ref, kx_ref, kn_ref, vp_ref, vx_ref, vn_ref, o_ref,
                *, n_ctx_blocks, n_blocks):
    n = pl.program_id(1)
    lc = n_ctx_blocks * BLOCK
    w = lc + 3 * BLOCK
    is_lat = n >= n_ctx_blocks
    prev_ok = n > n_ctx_blocks
    next_ok = n < n_blocks - 1
    row = lax.broadcasted_iota(jnp.int32, (BLOCK, w), 0)
    col = lax.broadcasted_iota(jnp.int32, (BLOCK, w), 1)
    c = col - lc
    lo_b = jnp.where(prev_ok, row, BLOCK)
    hi_b = jnp.where(next_ok, row + 2 * BLOCK, 2 * BLOCK - 1)
    lat_bias = jnp.where(is_lat, 0.0, NEG_INF)
    bias = jnp.where(col < lc, 0.0, jnp.where(c >= lo_b, jnp.where(c <= hi_b, lat_bias, NEG_INF), NEG_INF))
    bias2 = jnp.concatenate([bias, bias], axis=0)
    rid = lax.broadcasted_iota(jnp.int32, (2 * BLOCK, 1), 0)
    outs = []
    for a in range(2):
        sl = slice(128 * a, 128 * (a + 1))
        q2 = _stack_heads(q_ref[0, :, sl])
        keys = jnp.concatenate([kc_ref[0, :, sl], kp_ref[0, :, sl], kx_ref[0, :, sl], kn_ref[0, :, sl]], axis=0)
        vals = jnp.concatenate([vc_ref[0, :, sl], vp_ref[0, :, sl], vx_ref[0, :, sl], vn_ref[0, :, sl]], axis=0)
        s = _dot_nt(q2, keys) + bias2
        sk = jnp.where(rid < BLOCK, sink_ref[2 * a], sink_ref[2 * a + 1])
        m = jnp.maximum(jnp.max(s, axis=-1, keepdims=True), sk)
        e = jnp.exp(s - m)
        den = jnp.sum(e, axis=-1, keepdims=True) + jnp.exp(sk - m)
        p = (e / den).astype(BF16)
        outs.append(_unstack_heads(_dot(p, vals)))
    o_ref[0] = jnp.concatenate(outs, axis=1).astype(BF16)


def _window_attention(p0, kv, sink, n_ctx):
    b, t, _ = p0.shape
    nb = t // BLOCK
    ncb = n_ctx // BLOCK
    blk = lambda cidx: pl.BlockSpec((1, BLOCK, 256), lambda bb, n: (bb, n, cidx))
    prv = lambda cidx: pl.BlockSpec((1, BLOCK, 256), lambda bb, n: (bb, jnp.maximum(n - 1, 0), cidx))
    nxt = lambda cidx: pl.BlockSpec((1, BLOCK, 256), lambda bb, n: (bb, jnp.minimum(n + 1, nb - 1), cidx))
    ctx = lambda cidx: pl.BlockSpec((1, n_ctx, 256), lambda bb, n: (bb, 0, cidx))
    return pl.pallas_call(
        functools.partial(_win_kernel, n_ctx_blocks=ncb, n_blocks=nb),
        grid=(b, nb),
        in_specs=[
            pl.BlockSpec(memory_space=pltpu.SMEM),
            blk(2),
            ctx(0), ctx(1),
            prv(0), blk(0), nxt(0),
            prv(1), blk(1), nxt(1),
        ],
        out_specs=pl.BlockSpec((1, BLOCK, 256), lambda bb, n: (bb, n, 0)),
        out_shape=jax.ShapeDtypeStruct((b, t, 256), BF16),
        compiler_params=_cparams(("parallel", "parallel")),
        name="window_attn",
    )(sink, p0, kv, kv, kv, kv, kv, kv, kv, kv)


SAFE_LOG2_BOUND = 60.0


def _glb_kernel(q_ref, k_ref, v_ref, o_ref, kmax_scr, *, tq, tk, n_ctx, t_all):
    i = pl.program_id(1)
    sls = (slice(0, 128), slice(128, 256))
    lane = lax.broadcasted_iota(jnp.int32, (2 * tq, 128), 1)

    @pl.when(i == 0)
    def _():
        klane = lax.broadcasted_iota(jnp.int32, (tk, 128), 1)
        for a in range(2):
            def kbody(ci, mx):
                start = pl.multiple_of(ci * tk, tk)
                k = k_ref[0, pl.ds(start, tk), sls[a]].astype(F32)
                sq = jnp.sum(jnp.where(klane < HEAD_DIM, k * k, 0.0), axis=-1, keepdims=True)
                return jnp.maximum(mx, jnp.max(sq, axis=0, keepdims=True))
            mx = lax.fori_loop(0, t_all // tk, kbody, jnp.zeros((1, 1), F32))
            kmax_scr[a:a + 1, :] = jnp.broadcast_to(jnp.sqrt(mx), (1, 128))

    q2s, bnds = [], []
    for a in range(2):
        q = q_ref[0, :, sls[a]].astype(F32)
        q2 = jnp.concatenate([q, pltpu.roll(q, HEAD_DIM, 1)], axis=0)
        q2 = jnp.where(lane < HEAD_DIM, q2, 0.0)
        nrm = jnp.sqrt(jnp.sum(q2 * q2, axis=-1, keepdims=True))
        bnd = nrm * kmax_scr[a:a + 1, 0:1] * 1.01
        q2s.append(q2)
        bnds.append(bnd)
    worst = jnp.max(jnp.maximum(bnds[0], bnds[1]))

    def finish(res):
        outs = []
        for acc in res:
            den = jnp.sum(jnp.where(lane == HEAD_DIM, acc, 0.0), axis=-1, keepdims=True)
            o = acc / den
            lo_t = lax.broadcasted_iota(jnp.int32, (tq, 128), 1) < HEAD_DIM
            outs.append(jnp.where(lo_t, o[:tq], pltpu.roll(o, HEAD_DIM, 1)[tq:]))
        o_ref[0] = jnp.concatenate(outs, axis=1).astype(BF16)

    def attend_fixed(n_chunks, size):
        qb = [jnp.where(lane == HEAD_DIM, -bnds[a], q2s[a]).astype(BF16) for a in range(2)]

        def body(ci, carry):
            start = pl.multiple_of(ci * size, size)
            out = []
            for a in range(2):
                k = k_ref[0, pl.ds(start, size), sls[a]]
                v = v_ref[0, pl.ds(start, size), sls[a]]
                p = jnp.exp2(_dot_nt(qb[a], k))
                out.append(carry[a] + _dot(p.astype(BF16), v))
            return tuple(out)

        zero = jnp.zeros((2 * tq, 128), F32)
        finish(lax.fori_loop(0, n_chunks, body, (zero, zero)))

    def attend_online(n_chunks, size):
        qb = [q2s[a].astype(BF16) for a in range(2)]

        def body(ci, carry):
            start = pl.multiple_of(ci * size, size)
            out = []
            for a in range(2):
                m, acc = carry[a]
                k = k_ref[0, pl.ds(start, size), sls[a]]
                v = v_ref[0, pl.ds(start, size), sls[a]]
                s = _dot_nt(qb[a], k)
                m_new = jnp.maximum(m, jnp.max(s, axis=-1, keepdims=True))
                p = jnp.exp2(s - m_new)
                acc = jnp.exp2(m - m_new) * acc + _dot(p.astype(BF16), v)
                out.append((m_new, acc))
            return tuple(out)

        init = (jnp.full((2 * tq, 1), NEG_INF, F32), jnp.zeros((2 * tq, 128), F32))
        res = lax.fori_loop(0, n_chunks, body, (init, init))
        finish([acc for (_, acc) in res])

    is_ctx = (i + 1) * tq <= n_ctx
    safe = worst <= SAFE_LOG2_BOUND

    @pl.when(jnp.logical_and(is_ctx, safe))
    def _():
        attend_fixed(1, n_ctx)

    @pl.when(jnp.logical_and(is_ctx, jnp.logical_not(safe)))
    def _():
        attend_online(1, n_ctx)

    @pl.when(jnp.logical_and(jnp.logical_not(is_ctx), safe))
    def _():
        attend_fixed(t_all // tk, tk)

    @pl.when(jnp.logical_and(jnp.logical_not(is_ctx), jnp.logical_not(safe)))
    def _():
        attend_online(t_all // tk, tk)


def _global_attention(p0, kv, n_ctx):
    b, t, _ = p0.shape
    tq = _pick_tile(n_ctx, (256, 128))
    tk = _pick_tile(t, (768, 384, 256, 128))
    return pl.pallas_call(
        functools.partial(_glb_kernel, tq=tq, tk=tk, n_ctx=n_ctx, t_all=t),
        grid=(b, t // tq),
        in_specs=[
            pl.BlockSpec((1, tq, 256), lambda bb, i: (bb, i, 4)),
            pl.BlockSpec((1, t, 256), lambda bb, i: (bb, 0, 2)),
            pl.BlockSpec((1, t, 256), lambda bb, i: (bb, 0, 3)),
        ],
        out_specs=pl.BlockSpec((1, tq, 256), lambda bb, i: (bb, i, 0)),
        out_shape=jax.ShapeDtypeStruct((b, t, 256), BF16),
        scratch_shapes=[pltpu.VMEM((8, 128), F32)],
        compiler_params=_cparams(("arbitrary", "arbitrary")),
        name="global_attn",
    )(p0, kv, kv)


_TAB_D, _TAB_Q, _TAB_K, _TAB_C, _TAB_ROWS = 0, 256, 384, 512, 640


def _ret_kernel(dec_ref, qf_ref, kf_ref, vf_ref, qb_ref, kb_ref, vb_ref, of_ref, ob_ref, st_scr, tab_scr,
                *, batch):
    i = pl.program_id(0)
    r = lax.broadcasted_iota(jnp.int32, (BLOCK, BLOCK), 0)
    c = lax.broadcasted_iota(jnp.int32, (BLOCK, BLOCK), 1)
    lane_lo = c < HEAD_DIM
    same_head = (r < HEAD_DIM) == lane_lo

    @pl.when(i == 0)
    def _():
        st_scr[...] = jnp.zeros_like(st_scr)
        x = dec_ref[...]
        lg = jnp.where(x >= 0.0, -jnp.log(1.0 + jnp.exp(-x)), x - jnp.log(1.0 + jnp.exp(x)))
        rf = r.astype(F32)
        cf = c.astype(F32)
        for d in range(2):
            for pair in range(2):
                le = lg[4 * d + 2 * pair:4 * d + 2 * pair + 1, :]
                lo_ = lg[4 * d + 2 * pair + 1:4 * d + 2 * pair + 2, :]
                lp = jnp.where(lane_lo[0:1, :], le, lo_)
                if d == 0:
                    rel = rf - cf
                    qpow = rf + 1.0
                    kpow = (BLOCK - 1.0) - rf
                else:
                    rel = cf - rf - 1.0
                    qpow = (BLOCK - 1.0) - rf
                    kpow = rf
                msk = rel >= 0.0
                relc = jnp.where(msk, rel, 0.0)
                tab_scr[d, pair, _TAB_D:_TAB_D + BLOCK, :] = jnp.where(msk, jnp.exp(le * relc), 0.0)
                tab_scr[d, pair, _TAB_D + BLOCK:_TAB_Q, :] = jnp.where(msk, jnp.exp(lo_ * relc), 0.0)
                tab_scr[d, pair, _TAB_Q:_TAB_K, :] = jnp.exp(lp * qpow)
                tab_scr[d, pair, _TAB_K:_TAB_C, :] = jnp.exp(lp * kpow)
                tab_scr[d, pair, _TAB_C:_TAB_ROWS, :] = jnp.where(
                    r < HEAD_DIM, jnp.exp(le * float(BLOCK)), jnp.exp(lo_ * float(BLOCK)))

    for d, (q_ref, k_ref, v_ref, o_ref) in enumerate(((qf_ref, kf_ref, vf_ref, of_ref),
                                                     (qb_ref, kb_ref, vb_ref, ob_ref))):
        for bb in range(batch):
            for pair in range(2):
                sl = slice(128 * pair, 128 * (pair + 1))
                q = q_ref[bb, :, sl].astype(F32)
                k = k_ref[bb, :, sl].astype(F32)
                v = v_ref[bb, :, sl]
                q2 = jnp.concatenate([jnp.where(lane_lo, q, 0.0), jnp.where(lane_lo, 0.0, q)], axis=0)
                qk = _dot_nt(q2.astype(BF16), k.astype(BF16)) * tab_scr[d, pair, _TAB_D:_TAB_Q, :]
                o_intra = _unstack_heads(_dot(qk.astype(BF16), v))
                s = st_scr[d, bb, pair]
                qd = q * tab_scr[d, pair, _TAB_Q:_TAB_K, :]
                o_ref[bb, :, sl] = o_intra + _dot(qd.astype(BF16), s.astype(BF16))
                kd = k * tab_scr[d, pair, _TAB_K:_TAB_C, :]
                upd = _dot(kd.T.astype(BF16), v)
                st_scr[d, bb, pair] = jnp.where(same_head, tab_scr[d, pair, _TAB_C:_TAB_ROWS, :] * s + upd, 0.0)


def _retention(p0, dec, n_ctx):
    b, t, _ = p0.shape
    nb = t // BLOCK
    ncb = n_ctx // BLOCK
    bwd = lambda i: jnp.where(i < ncb, ncb - 1 - i, nb - 1 + ncb - i)
    fspec = lambda cidx: pl.BlockSpec((b, BLOCK, 256), lambda i: (0, i, cidx))
    bspec = lambda cidx: pl.BlockSpec((b, BLOCK, 256), lambda i: (0, bwd(i), cidx))
    return pl.pallas_call(
        functools.partial(_ret_kernel, batch=b),
        grid=(nb,),
        in_specs=[
            pl.BlockSpec((8, 128), lambda i: (0, 0)),
            fspec(6), fspec(7), fspec(8),
            bspec(6), bspec(7), bspec(8),
        ],
        out_specs=[
            pl.BlockSpec((b, BLOCK, 256), lambda i: (0, i, 0)),
            pl.BlockSpec((b, BLOCK, 256), lambda i: (0, bwd(i), 0)),
        ],
        out_shape=[jax.ShapeDtypeStruct((b, t, 256), F32), jax.ShapeDtypeStruct((b, t, 256), F32)],
        scratch_shapes=[pltpu.VMEM((2, b, 2, BLOCK, BLOCK), F32),
                        pltpu.VMEM((2, 2, _TAB_ROWS, BLOCK), F32)],
        compiler_params=_cparams(("arbitrary",)),
        name="retention",
    )(dec, p0, p0, p0, p0, p0, p0)


def _merge_kernel(x_ref, uv_ref, batt_ref, catt_ref, of_ref, ob_ref, rg_ref, gate_ref, gl_ref, gc_ref,
                  ws_ref, bs_ref, gn_ref, bd_ref, wb_ref, wo_ref, o_ref, *, tm, n_ctx):
    i = pl.program_id(1)
    grp = jnp.right_shift(lax.broadcasted_iota(jnp.int32, (BLOCK, 256), 1), 6)
    a_chunks = []
    for ch in range(tm // BLOCK):
        rows = slice(ch * BLOCK, (ch + 1) * BLOCK)
        vn = uv_ref[0, rows, 256:512]
        mixed = bs_ref[...]
        for g in range(4):
            mixed = mixed + jnp.where(grp == g, _dot(ws_ref[g], vn), 0.0)
        a_chunks.append((uv_ref[0, rows, 0:256].astype(F32) * mixed).astype(BF16))
    a_br = jnp.concatenate(a_chunks, axis=0)
    o = of_ref[0] + ob_ref[0]
    mean = _split_dot(o, bd_ref[...]) * (1.0 / HEAD_DIM)
    oc = o - mean
    var = _split_dot(oc * oc, bd_ref[...]) * (1.0 / HEAD_DIM)
    rg = rg_ref[0].astype(F32)
    d_br = (oc * lax.rsqrt(var + EPS) * gn_ref[...] * (rg * jax.nn.sigmoid(rg))).astype(BF16)
    branches = (a_br, batt_ref[0], catt_ref[0], d_br)
    d_model = x_ref.shape[-1]
    acc = None
    for n in range(N_BRANCH):
        gate = jax.nn.sigmoid(gate_ref[0, :, n * d_model:(n + 1) * d_model].astype(F32))
        term = gate * _dot(branches[n], wb_ref[n])
        acc = term if acc is None else acc + term
    m = _dot(acc.astype(BF16), wo_ref[...])
    row = i * tm + lax.broadcasted_iota(jnp.int32, (tm, 1), 0)
    g1 = jnp.where(row < n_ctx, gc_ref[...], gl_ref[0])
    o_ref[0] = x_ref[0] + g1 * m


def _merge(xc, p0, b_att, c_att, o_f, o_b, gates, g1_l, g1_c, ws_bf, bias, gn, bd, wb_bf, wo_bf, n_ctx):
    b, t, d = xc.shape
    tm = _pick_tile(t, (384, 256, 128))
    tok = lambda w, cidx: pl.BlockSpec((1, tm, w), lambda bb, i: (bb, i, cidx))
    full = lambda shape: pl.BlockSpec(shape, lambda bb, i: (0,) * len(shape))
    return pl.pallas_call(
        functools.partial(_merge_kernel, tm=tm, n_ctx=n_ctx),
        grid=(b, t // tm),
        in_specs=[
            tok(d, 0),
            tok(512, 0),
            tok(256, 0), tok(256, 0),
            tok(256, 0), tok(256, 0),
            tok(256, 9),
            tok(N_BRANCH * d, 0),
            pl.BlockSpec((1, 1, d), lambda bb, i: (bb, 0, 0)),
            full((1, d)),
            full((4, BLOCK, BLOCK)), full((BLOCK, 256)), full((1, 256)), full((256, 256)),
            full((N_BRANCH, BRANCH_W, d)), full((d, d)),
        ],
        out_specs=tok(d, 0),
        out_shape=jax.ShapeDtypeStruct((b, t, d), F32),
        compiler_params=_cparams(("parallel", "parallel")),
        name="merge",
    )(xc, p0, b_att, c_att, o_f, o_b, p0, gates, g1_l, g1_c, ws_bf, bias, gn, bd, wb_bf, wo_bf)


def _ffn_kernel(x_ref, modl_ref, modc_ref, g_ref, w1_ref, w3_ref, w2_ref, o_ref, h_scr, acc_scr, *, tm, n_ctx):
    i = pl.program_id(1)
    k = pl.program_id(2)

    @pl.when(k == 0)
    def _():
        h = _modulated_norm(x_ref[0], g_ref[...], modl_ref[0, 0:1, :], modl_ref[0, 1:2, :],
                            modc_ref[0:1, :], modc_ref[1:2, :], i * tm, n_ctx)
        h_scr[...] = h.astype(BF16)
        acc_scr[...] = jnp.zeros_like(acc_scr)

    h = h_scr[...]
    a = _dot(h, w1_ref[...])
    bgate = _dot(h, w3_ref[...])
    act = (a * jax.nn.sigmoid(a) * bgate).astype(BF16)
    acc_scr[...] += _dot(act, w2_ref[...])

    @pl.when(k == pl.num_programs(2) - 1)
    def _():
        row = i * tm + lax.broadcasted_iota(jnp.int32, (tm, 1), 0)
        g2 = jnp.where(row < n_ctx, modc_ref[2:3, :], modl_ref[0, 2:3, :])
        o_ref[0] = x_ref[0] + g2 * acc_scr[...]


def _ffn(xc, modl, modc, g, w1, w3, w2, n_ctx):
    b, t, d = xc.shape
    dff = w1.shape[1]
    tm = _pick_tile(t, (768, 384, 256, 128))
    tf = _pick_tile(dff, (1408, 512, 256, 128))
    return pl.pallas_call(
        functools.partial(_ffn_kernel, tm=tm, n_ctx=n_ctx),
        grid=(b, t // tm, dff // tf),
        in_specs=[
            pl.BlockSpec((1, tm, d), lambda bb, i, k: (bb, i, 0)),
            pl.BlockSpec((1, 3, d), lambda bb, i, k: (bb, 0, 0)),
            pl.BlockSpec((3, d), lambda bb, i, k: (0, 0)),
            pl.BlockSpec((1, d), lambda bb, i, k: (0, 0)),
            pl.BlockSpec((d, tf), lambda bb, i, k: (0, k)),
            pl.BlockSpec((d, tf), lambda bb, i, k: (0, k)),
            pl.BlockSpec((tf, d), lambda bb, i, k: (k, 0)),
        ],
        out_specs=pl.BlockSpec((1, tm, d), lambda bb, i, k: (bb, i, 0)),
        out_shape=jax.ShapeDtypeStruct((b, t, d), F32),
        scratch_shapes=[pltpu.VMEM((tm, d), BF16), pltpu.VMEM((tm, d), F32)],
        compiler_params=_cparams(("parallel", "parallel", "arbitrary")),
        name="ffn",
    )(xc, modl, modc, g, w1, w3, w2)


def _router_kernel(x_ref, modl_ref, modc_ref, g_ref, rw_ref, rb_ref, h_ref, ids_ref, p_ref, *, tm, n_ctx):
    i = pl.program_id(1)
    lane = lax.broadcasted_iota(jnp.int32, (tm, 128), 1)
    h = _modulated_norm(x_ref[0], g_ref[...], modl_ref[0, 0:1, :], modl_ref[0, 1:2, :],
                        modc_ref[0:1, :], modc_ref[1:2, :], i * tm, n_ctx)
    h_ref[0] = h
    h_hi = h.astype(BF16)
    h_lo = (h - h_hi.astype(F32)).astype(BF16)
    rw = rw_ref[...]
    rw_hi = rw.astype(BF16)
    rw_lo = (rw - rw_hi.astype(F32)).astype(BF16)
    logits = _dot(h_hi, rw_hi) + _dot(h_hi, rw_lo) + _dot(h_lo, rw_hi) + rb_ref[...]
    logits = jnp.where(lane < N_EXPERTS, logits, NEG_INF)
    v0 = jnp.max(logits, axis=-1, keepdims=True)
    i0 = jnp.min(jnp.where(logits == v0, lane, 128), axis=-1, keepdims=True)
    rest = jnp.where(lane == i0, NEG_INF, logits)
    v1 = jnp.max(rest, axis=-1, keepdims=True)
    i1 = jnp.min(jnp.where(rest == v1, lane, 128), axis=-1, keepdims=True)
    e1 = jnp.exp(v1 - v0)
    p0 = 1.0 / (1.0 + e1)
    ids_ref[0] = jnp.where(lane == 0, i0, jnp.where(lane == 1, i1, 0))
    p_ref[0] = jnp.where(lane == 0, p0, jnp.where(lane == 1, e1 * p0, 0.0))


def _router(xc, modl, modc, g, rw_pad, rb_pad, n_ctx):
    b, t, d = xc.shape
    tm = _pick_tile(t, (768, 384, 256, 128))
    tok = lambda w: pl.BlockSpec((1, tm, w), lambda bb, i: (bb, i, 0))
    return pl.pallas_call(
        functools.partial(_router_kernel, tm=tm, n_ctx=n_ctx),
        grid=(b, t // tm),
        in_specs=[
            tok(d),
            pl.BlockSpec((1, 3, d), lambda bb, i: (bb, 0, 0)),
            pl.BlockSpec((3, d), lambda bb, i: (0, 0)),
            pl.BlockSpec((1, d), lambda bb, i: (0, 0)),
            pl.BlockSpec((d, 128), lambda bb, i: (0, 0)),
            pl.BlockSpec((1, 128), lambda bb, i: (0, 0)),
        ],
        out_specs=[tok(d), tok(128), tok(128)],
        out_shape=[jax.ShapeDtypeStruct((b, t, d), F32), jax.ShapeDtypeStruct((b, t, 128), jnp.int32),
                   jax.ShapeDtypeStruct((b, t, 128), F32)],
        compiler_params=_cparams(("parallel", "parallel")),
        name="router",
    )(xc, modl, modc, g, rw_pad, rb_pad)


def _routing_plan(ids, tm_e):
    n_tok = ids.shape[0]
    n_asg = 2 * n_tok
    n_tiles = n_asg // tm_e + N_EXPERTS
    e_flat = jnp.concatenate([ids[:, 0], ids[:, 1]])
    onehot = (e_flat[:, None] == jnp.arange(N_EXPERTS, dtype=jnp.int32)[None, :]).astype(jnp.int32)
    csum = jnp.cumsum(onehot, axis=0)
    rank = jnp.sum(csum * onehot, axis=1) - 1
    counts = csum[-1]
    padded = ((counts + tm_e - 1) // tm_e) * tm_e
    ends = jnp.cumsum(padded)
    starts = ends - padded
    dest = jnp.sum(onehot * starts[None, :], axis=1) + rank
    asg = jnp.zeros((n_tiles * tm_e,), jnp.int32).at[dest].set(jnp.arange(n_asg, dtype=jnp.int32))
    tile_start = jnp.arange(n_tiles, dtype=jnp.int32) * tm_e
    n_used = ends[-1] // tm_e
    tile_e = jnp.minimum(jnp.sum((tile_start[:, None] >= ends[None, :]).astype(jnp.int32), axis=1), N_EXPERTS - 1)
    n_valid = jnp.clip(counts[tile_e] - (tile_start - starts[tile_e]), 0, tm_e)
    n_valid = jnp.where(tile_start < ends[-1], n_valid, 0).astype(jnp.int32)
    last_e = tile_e[jnp.maximum(n_used - 1, 0)]
    tile_e = jnp.where(tile_start < ends[-1], tile_e, last_e).astype(jnp.int32)
    return tile_e, n_valid, asg.reshape(n_tiles, 1, tm_e)


def _expert_kernel(te_ref, nv_ref, asg_ref, h_hbm, w1_ref, w3_ref, w2_ref, y_hbm,
                   hbuf, hbf, acc, gsem, ssem, *, n_tok, n_k):
    i = pl.program_id(0)
    k = pl.program_id(1)
    nv = nv_ref[i]

    def row_copy(r, gather):
        a = asg_ref[0, 0, r]
        if gather:
            src = jnp.where(a >= n_tok, a - n_tok, a)
            return pltpu.make_async_copy(h_hbm.at[pl.ds(src, 1), :], hbuf.at[pl.ds(r, 1), :], gsem)
        return pltpu.make_async_copy(acc.at[pl.ds(r, 1), :], y_hbm.at[pl.ds(a, 1), :], ssem)

    @pl.when(jnp.logical_and(i == 0, k == 0))
    def _():
        hbuf[...] = jnp.zeros_like(hbuf)

    @pl.when(jnp.logical_and(k == 0, nv > 0))
    def _():
        lax.fori_loop(0, nv, lambda r, c: (row_copy(r, True).start(), c)[1], 0)
        lax.fori_loop(0, nv, lambda r, c: (row_copy(r, True).wait(), c)[1], 0)
        hbf[...] = hbuf[...].astype(BF16)
        acc[...] = jnp.zeros_like(acc)

    @pl.when(nv > 0)
    def _():
        h = hbf[...]
        a = _dot(h, w1_ref[0])
        bgate = _dot(h, w3_ref[0])
        act = (a * jax.nn.sigmoid(a) * bgate).astype(BF16)
        acc[...] += _dot(act, w2_ref[0])

    @pl.when(jnp.logical_and(k == n_k - 1, nv > 0))
    def _():
        lax.fori_loop(0, nv, lambda r, c: (row_copy(r, False).start(), c)[1], 0)
        lax.fori_loop(0, nv, lambda r, c: (row_copy(r, False).wait(), c)[1], 0)


def _experts(h_flat, tile_e, n_valid, asg, w1, w3, w2, tm_e):
    n_tok, d = h_flat.shape
    n_e, _, dff = w1.shape
    n_tiles = asg.shape[0]
    tf = _pick_tile(dff, (896, 256, 128))
    n_k = dff // tf
    kk = lambda i, k, nv: jnp.where(nv[i] > 0, k, n_k - 1)
    grid_spec = pltpu.PrefetchScalarGridSpec(
        num_scalar_prefetch=2,
        grid=(n_tiles, n_k),
        in_specs=[
            pl.BlockSpec((1, 1, tm_e), lambda i, k, te, nv: (i, 0, 0), memory_space=pltpu.SMEM),
            pl.BlockSpec(memory_space=pl.ANY),
            pl.BlockSpec((1, d, tf), lambda i, k, te, nv: (te[i], 0, kk(i, k, nv))),
            pl.BlockSpec((1, d, tf), lambda i, k, te, nv: (te[i], 0, kk(i, k, nv))),
            pl.BlockSpec((1, tf, d), lambda i, k, te, nv: (te[i], kk(i, k, nv), 0)),
        ],
        out_specs=pl.BlockSpec(memory_space=pl.ANY),
        scratch_shapes=[pltpu.VMEM((tm_e, d), F32), pltpu.VMEM((tm_e, d), BF16), pltpu.VMEM((tm_e, d), F32),
                        pltpu.SemaphoreType.DMA(()), pltpu.SemaphoreType.DMA(())],
    )
    return pl.pallas_call(
        functools.partial(_expert_kernel, n_tok=n_tok, n_k=n_k),
        grid_spec=grid_spec,
        out_shape=jax.ShapeDtypeStruct((2 * n_tok, d), F32),
        compiler_params=_cparams(("arbitrary", "arbitrary")),
        name="experts",
    )(tile_e, n_valid, asg, h_flat, w1, w3, w2)


def _combine_kernel(x_ref, y0_ref, y1_ref, p_ref, gl_ref, gc_ref, o_ref, *, tm, n_ctx):
    i = pl.program_id(1)
    p = p_ref[0]
    lane = lax.broadcasted_iota(jnp.int32, p.shape, 1)
    p0 = jnp.sum(jnp.where(lane == 0, p, 0.0), axis=-1, keepdims=True)
    p1 = jnp.sum(jnp.where(lane == 1, p, 0.0), axis=-1, keepdims=True)
    row = i * tm + lax.broadcasted_iota(jnp.int32, (tm, 1), 0)
    g2 = jnp.where(row < n_ctx, gc_ref[...], gl_ref[0])
    o_ref[0] = x_ref[0] + g2 * (p0 * y0_ref[0, 0] + p1 * y1_ref[0, 0])


def _combine(xc, y, p, g2_l, g2_c, n_ctx):
    b, t, d = xc.shape
    tm = _pick_tile(t, (768, 384, 256, 128))
    tok = lambda w: pl.BlockSpec((1, tm, w), lambda bb, i: (bb, i, 0))
    ysp = lambda kk: pl.BlockSpec((1, 1, tm, d), lambda bb, i: (kk, bb, i, 0))
    return pl.pallas_call(
        functools.partial(_combine_kernel, tm=tm, n_ctx=n_ctx),
        grid=(b, t // tm),
        in_specs=[tok(d), ysp(0), ysp(1), tok(128),
                  pl.BlockSpec((1, 1, d), lambda bb, i: (bb, 0, 0)),
                  pl.BlockSpec((1, d), lambda bb, i: (0, 0))],
        out_specs=tok(d),
        out_shape=jax.ShapeDtypeStruct((b, t, d), F32),
        compiler_params=_cparams(("parallel", "parallel")),
        name="moe_combine",
    )(xc, y, y, p, g2_l, g2_c)


MOE_TILE = 512


def _moe(xc, modl, modc, g, rw_pad, rb_pad, w1, w3, w2, n_ctx):
    b, t, d = xc.shape
    h, ids, p = _router(xc, modl, modc, g, rw_pad, rb_pad, n_ctx)
    tile_e, n_valid, asg = _routing_plan(ids.reshape(b * t, 128)[:, :2], MOE_TILE)
    y = _experts(h.reshape(b * t, d), tile_e, n_valid, asg, w1, w3, w2, MOE_TILE)
    return _combine(xc, y.reshape(2, b, t, d), p, modl[:, 2:3], modc[2:3], n_ctx)


def _rope_tables(n_ctx, s):
    def widen(cos, sin):
        c64 = jnp.concatenate([cos, cos], axis=-1)
        s64 = jnp.concatenate([-sin, sin], axis=-1)
        return jnp.tile(c64, (1, 2)), jnp.tile(s64, (1, 2))

    n_rows = s // GRID_W
    row = jnp.repeat(jnp.arange(n_rows, dtype=F32), GRID_W)
    col = jnp.tile(jnp.arange(GRID_W, dtype=F32), n_rows)
    n_freq = HEAD_DIM // 4
    inv = ROPE_BASE ** (-jnp.arange(n_freq, dtype=F32) / n_freq)
    ang = jnp.concatenate([row[:, None] * inv, col[:, None] * inv], axis=-1)
    cos_ax = jnp.concatenate([jnp.ones((n_ctx, HEAD_DIM // 2), F32), jnp.cos(ang)], axis=0)
    sin_ax = jnp.concatenate([jnp.zeros((n_ctx, HEAD_DIM // 2), F32), jnp.sin(ang)], axis=0)
    n_freq = HEAD_DIM // 2
    inv = ROPE_BASE ** (-jnp.arange(n_freq, dtype=F32) / n_freq)
    pos = jnp.arange(n_ctx + s, dtype=F32)
    ang = pos[:, None] * inv
    return widen(cos_ax, sin_ax) + widen(jnp.cos(ang), jnp.sin(ang))


def _block_diag_ones(n):
    g = jnp.arange(n) // HEAD_DIM
    return (g[:, None] == g[None, :]).astype(BF16)


def kernel(x, c, ctx, c_ctx, w_mod, b_mod, g_norm1, g_norm2, w_in, mlp_g_v, mlp_w_s, mlp_b_s, win_q_norm,
           win_k_norm, win_sink, glb_q_norm, glb_k_norm, ret_decay, ret_gn, w_branch, w_out, ffn_w1, ffn_w3,
           ffn_w2, router_w, router_b, moe_w1, moe_w3, moe_w2):
    batch, s, d = x.shape
    n_ctx = ctx.shape[1]
    depth = w_in.shape[0]
    assert batch + 1 <= 8 and n_ctx % BLOCK == 0 and s % BLOCK == 0

    xc = jnp.concatenate([ctx, x], axis=1)
    cvec = jnp.zeros((8, d), F32).at[:batch].set(c).at[batch].set(c_ctx)
    mod = _modulation(cvec, w_mod, b_mod)
    mod = mod.reshape(depth, 8, 6, d)

    tabs = _rope_tables(n_ctx, s)
    bd256 = _block_diag_ones(256)
    ones = lambda n: jnp.ones((n,), F32)
    zeros = lambda n: jnp.zeros((n,), F32)
    scale = HEAD_DIM ** -0.5

    for layer in range(depth):
        ml = mod[layer]
        modl1 = ml[:batch, 0:2]
        modc1 = ml[batch, 0:2]
        g1_l = ml[:batch, 2:3]
        g1_c = ml[batch, 2:3]
        modl2 = ml[:batch, 3:6]
        modc2 = ml[batch, 3:6]

        def qkv_rows(gq, gk):
            return jnp.concatenate([jnp.tile(gq, 4), jnp.tile(gk, 2), ones(128)])

        gain = jnp.stack([
            jnp.concatenate([ones(256), mlp_g_v[layer].reshape(-1)]),
            qkv_rows(win_q_norm[layer], win_k_norm[layer]),
            qkv_rows(glb_q_norm[layer], glb_k_norm[layer]),
            ones(IN_TILE)])
        qk_mask = jnp.concatenate([ones(384), zeros(128)])
        nmask = jnp.stack([jnp.concatenate([zeros(256), ones(256)]), qk_mask, qk_mask, zeros(IN_TILE)])
        q_scale = jnp.concatenate([jnp.full((256,), scale, F32), ones(256)])
        q_scale2 = jnp.concatenate([jnp.full((256,), scale * LOG2_E, F32), ones(256)])
        post = jnp.stack([ones(IN_TILE), q_scale, q_scale2,
                          jnp.concatenate([ones(256), jnp.full((256,), scale, F32)])])

        p0, kv, gates = _inproj(xc, modl1, modc1, g_norm1[layer][None, :], w_in[layer].astype(BF16),
                                (gain, nmask, post, bd256), tabs, n_ctx)
        b_att = _window_attention(p0, kv, win_sink[layer], n_ctx)
        c_att = _global_attention(p0, kv, n_ctx)
        dec = jnp.broadcast_to(ret_decay[layer].reshape(-1, 1), (8, 128))
        o_f, o_b = _retention(p0, dec, n_ctx)
        bias = jnp.repeat(mlp_b_s[layer].T, HEAD_DIM, axis=1)
        xc = _merge(xc, p0, b_att, c_att, o_f, o_b, gates, g1_l, g1_c, mlp_w_s[layer].astype(BF16), bias,
                    ret_gn[layer].reshape(1, -1), bd256, w_branch[layer].astype(BF16),
                    w_out[layer].astype(BF16), n_ctx)
        j = layer // 2
        g2 = g_norm2[layer][None, :]
        if layer % 2 == 0:
            xc = _ffn(xc, modl2, modc2, g2, ffn_w1[j].astype(BF16), ffn_w3[j].astype(BF16),
                      ffn_w2[j].astype(BF16), n_ctx)
        else:
            rw_pad = jnp.zeros((d, 128), F32).at[:, :N_EXPERTS].set(router_w[j])
            rb_pad = jnp.zeros((1, 128), F32).at[0, :N_EXPERTS].set(router_b[j])
            xc = _moe(xc, modl2, modc2, g2, rw_pad, rb_pad, moe_w1[j].astype(BF16), moe_w3[j].astype(BF16),
                      moe_w2[j].astype(BF16), n_ctx)
    return xc[:, n_ctx:]
```

```python
import functools

import jax
import jax.numpy as jnp
from jax import lax
from jax.experimental import pallas as pl
from jax.experimental.pallas import tpu as pltpu

F32 = jnp.float32
BF16 = jnp.bfloat16

HEAD_DIM = 64
BLOCK = 128
GRID_W = 64
ROPE_BASE = 10000.0
N_BRANCH = 4
BRANCH_W = 256
N_EXPERTS = 8
EPS = 1e-6
NEG_INF = -1e30
LOG2_E = 1.4426950408889634
IN_TILE = 512
N_MIX_TILES = 5
VMEM_LIMIT = 56 * 1024 * 1024


def _cparams(sem):
    return pltpu.CompilerParams(dimension_semantics=sem, vmem_limit_bytes=VMEM_LIMIT)


def _dot(a, b):
    return jnp.dot(a, b, preferred_element_type=F32)


def _dot_nt(a, b):
    return lax.dot_general(a, b, (((1,), (1,)), ((), ())), preferred_element_type=F32)


def _split_dot(a, b_bf16):
    hi = a.astype(BF16)
    lo = (a - hi.astype(F32)).astype(BF16)
    return _dot(hi, b_bf16) + _dot(lo, b_bf16)


def _pick_tile(n, candidates):
    for c in candidates:
        if n % c == 0:
            return c
    raise ValueError(f"no tile for {n} in {candidates}")


def _modulated_norm(x, g, sh_l, sc_l, sh_c, sc_c, row0, n_ctx):
    tm = x.shape[0]
    ms = jnp.mean(x * x, axis=-1, keepdims=True)
    y = x * lax.rsqrt(ms + EPS) * g
    row = row0 + lax.broadcasted_iota(jnp.int32, (tm, 1), 0)
    is_ctx = row < n_ctx
    sh = jnp.where(is_ctx, sh_c, sh_l)
    sc = jnp.where(is_ctx, sc_c, sc_l)
    return y * (1.0 + sc) + sh


def _mod_kernel(c_ref, w_ref, b_ref, o_ref):
    c = c_ref[...]
    s = c * jax.nn.sigmoid(c)
    s_hi = s.astype(BF16)
    s_lo = (s - s_hi.astype(F32)).astype(BF16)
    w = w_ref[0]
    w_hi = w.astype(BF16)
    w_lo = (w - w_hi.astype(F32)).astype(BF16)
    o_ref[0] = _dot(s_hi, w_hi) + _dot(s_hi, w_lo) + _dot(s_lo, w_hi) + b_ref[0]


def _modulation(cvec, w_mod, b_mod):
    depth, d, n = w_mod.shape
    tn = _pick_tile(n, (1536, 1024, 512, 128))
    return pl.pallas_call(
        _mod_kernel,
        grid=(depth, n // tn),
        in_specs=[
            pl.BlockSpec((8, d), lambda l, j: (0, 0)),
            pl.BlockSpec((1, d, tn), lambda l, j: (l, 0, j)),
            pl.BlockSpec((1, 1, tn), lambda l, j: (l, 0, j)),
        ],
        out_specs=pl.BlockSpec((1, 8, tn), lambda l, j: (l, 0, j)),
        out_shape=jax.ShapeDtypeStruct((depth, 8, n), F32),
        compiler_params=_cparams(("parallel", "parallel")),
        name="modulation",
    )(cvec, w_mod, b_mod.reshape(depth, 1, n))


def _group_rms(v, bd, gain, nmask):
    sq = (v * v).astype(BF16)
    half = bd.shape[0]
    gs = jnp.concatenate([_dot(sq[:, c:c + half], bd) for c in range(0, v.shape[-1], half)], axis=1)
    n = v * lax.rsqrt(gs * (1.0 / HEAD_DIM) + EPS) * gain
    return jnp.where(nmask > 0.0, n, v)


def _rope(v, cos, sin):
    w = v.shape[-1]
    lane = lax.broadcasted_iota(jnp.int32, v.shape, 1)
    first = (lane & 32) == 0
    partner = jnp.where(first, pltpu.roll(v, w - 32, 1), pltpu.roll(v, 32, 1))
    return v * cos + partner * sin


def _tile4(t):
    return jnp.concatenate([t, t, t, t], axis=1)


def _inproj_kernel(x_ref, modl_ref, modc_ref, g_ref, w_ref, gain_ref, nmask_ref, post_ref, bd_ref,
                   cax_ref, sax_ref, csq_ref, ssq_ref, p_ref, kv_ref, gate_ref, h_scr, *, tm, n_ctx):
    i = pl.program_id(1)
    j = pl.program_id(2)

    @pl.when(j == 0)
    def _():
        h = _modulated_norm(x_ref[0], g_ref[...], modl_ref[0, 0:1, :], modl_ref[0, 1:2, :],
                            modc_ref[0:1, :], modc_ref[1:2, :], i * tm, n_ctx)
        h_scr[...] = h.astype(BF16)

    acc = _dot(h_scr[...], w_ref[...])

    @pl.when(j == 0)
    def _():
        v = jax.nn.gelu(acc)
        v = _group_rms(v, bd_ref[...], gain_ref[0:1, :], nmask_ref[0:1, :])
        p_ref[0] = v.astype(BF16)

    def attn_tile(row, duplicate):
        v = _group_rms(acc, bd_ref[...], gain_ref[row:row + 1, :], nmask_ref[row:row + 1, :])
        r = _rope(v, _tile4(cax_ref[...]), _tile4(sax_ref[...]))
        v = jnp.where(nmask_ref[row:row + 1, :] > 0.0, r, v) * post_ref[row:row + 1, :]
        p_ref[0] = v.astype(BF16)
        k = v[:, 256:384]
        vv = v[:, 384:512]
        lane = lax.broadcasted_iota(jnp.int32, k.shape, 1)
        lo = lane < HEAD_DIM
        kr = pltpu.roll(k, HEAD_DIM, 1)
        vr = pltpu.roll(vv, HEAD_DIM, 1)
        if duplicate:
            parts = [jnp.where(lo, k, kr), jnp.where(lo, kr, k), jnp.where(lo, vv, vr), jnp.where(lo, vr, vv)]
        else:
            pad = jnp.where(lane == HEAD_DIM, 1.0, 0.0)
            parts = [jnp.where(lo, k, pad), jnp.where(lo, kr, pad), jnp.where(lo, vv, pad), jnp.where(lo, vr, pad)]
        kv_ref[0] = jnp.concatenate(parts, axis=1).astype(BF16)

    @pl.when(j == 1)
    def _():
        attn_tile(1, True)

    @pl.when(j == 2)
    def _():
        attn_tile(2, False)

    @pl.when(j == 3)
    def _():
        v = _rope(acc, _tile4(csq_ref[...]), _tile4(ssq_ref[...])) * post_ref[3:4, :]
        p_ref[0] = v.astype(BF16)

    @pl.when(j == 4)
    def _():
        p_ref[0] = acc.astype(BF16)

    @pl.when(j >= N_MIX_TILES)
    def _():
        gate_ref[0] = acc.astype(BF16)


def _inproj(xc, modl, modc, g, w_bf, ep, tabs, n_ctx):
    b, t, d = xc.shape
    n_in = w_bf.shape[1]
    n_tiles = n_in // IN_TILE
    tm = _pick_tile(t, (1408, 768, 384, 256, 128))
    gain, nmask, post, bd = ep
    cax, sax, csq, ssq = tabs
    n_gate = n_in - N_MIX_TILES * IN_TILE
    tok = lambda bb, i, j: (i, 0)
    full2 = lambda bb, i, j: (0, 0)
    return pl.pallas_call(
        functools.partial(_inproj_kernel, tm=tm, n_ctx=n_ctx),
        grid=(b, t // tm, n_tiles),
        in_specs=[
            pl.BlockSpec((1, tm, d), lambda bb, i, j: (bb, i, 0)),
            pl.BlockSpec((1, 2, d), lambda bb, i, j: (bb, 0, 0)),
            pl.BlockSpec((2, d), full2),
            pl.BlockSpec((1, d), full2),
            pl.BlockSpec((d, IN_TILE), lambda bb, i, j: (0, j)),
            pl.BlockSpec((4, IN_TILE), full2),
            pl.BlockSpec((4, IN_TILE), full2),
            pl.BlockSpec((4, IN_TILE), full2),
            pl.BlockSpec((256, 256), full2),
            pl.BlockSpec((tm, 128), tok),
            pl.BlockSpec((tm, 128), tok),
            pl.BlockSpec((tm, 128), tok),
            pl.BlockSpec((tm, 128), tok),
        ],
        out_specs=[
            pl.BlockSpec((1, tm, IN_TILE), lambda bb, i, j: (bb, i, jnp.minimum(j, N_MIX_TILES - 1))),
            pl.BlockSpec((1, tm, IN_TILE), lambda bb, i, j: (bb, i, jnp.clip(j - 1, 0, 1))),
            pl.BlockSpec((1, tm, IN_TILE), lambda bb, i, j: (bb, i, jnp.maximum(j - N_MIX_TILES, 0))),
        ],
        out_shape=[
            jax.ShapeDtypeStruct((b, t, N_MIX_TILES * IN_TILE), BF16),
            jax.ShapeDtypeStruct((b, t, 2 * IN_TILE), BF16),
            jax.ShapeDtypeStruct((b, t, n_gate), BF16),
        ],
        scratch_shapes=[pltpu.VMEM((tm, d), BF16)],
        compiler_params=_cparams(("parallel", "parallel", "arbitrary")),
        name="inproj",
    )(xc, modl, modc, g, w_bf, gain, nmask, post, bd, cax, sax, csq, ssq)


def _stack_heads(qa):
    qf = qa.astype(F32)
    lo = lax.broadcasted_iota(jnp.int32, qf.shape, 1) < HEAD_DIM
    return jnp.concatenate([jnp.where(lo, qf, 0.0), jnp.where(lo, 0.0, qf)], axis=0).astype(BF16)


def _unstack_heads(o2):
    n = o2.shape[0] // 2
    lo = lax.broadcasted_iota(jnp.int32, (n, o2.shape[1]), 1) < HEAD_DIM
    return jnp.where(lo, o2[:n], o2[n:])


def _win_kernel(sink_ref, q_ref, kc_ref, vc_ref, kp_ref, kx_ref, kn_ref, vp_ref, vx_ref, vn_ref, o_ref,
                *, n_ctx_blocks, n_blocks):
    j = pl.program_id(1)
    lc = n_ctx_blocks * BLOCK
    w = lc + 3 * BLOCK
    row = lax.broadcasted_iota(jnp.int32, (BLOCK, w), 0)
    col = lax.broadcasted_iota(jnp.int32, (BLOCK, w), 1)
    c = col - lc
    rid = lax.broadcasted_iota(jnp.int32, (2 * BLOCK, 1), 0)
    for sub in range(2):
        n = 2 * j + sub
        rows = slice(sub * BLOCK, (sub + 1) * BLOCK)
        is_lat = n >= n_ctx_blocks
        prev_ok = n > n_ctx_blocks
        next_ok = n < n_blocks - 1
        lo_b = jnp.where(prev_ok, row, BLOCK)
        hi_b = jnp.where(next_ok, row + 2 * BLOCK, 2 * BLOCK - 1)
        lat_bias = jnp.where(is_lat, 0.0, NEG_INF)
        bias = jnp.where(col < lc, 0.0, jnp.where(c >= lo_b, jnp.where(c <= hi_b, lat_bias, NEG_INF), NEG_INF))
        bias2 = jnp.concatenate([bias, bias], axis=0)
        outs = []
        for a in range(2):
            sl = slice(128 * a, 128 * (a + 1))
            q2 = _stack_heads(q_ref[0, rows, sl])
            if sub == 0:
                local = [(kp_ref, vp_ref, slice(None)), (kx_ref, vx_ref, slice(0, BLOCK)),
                         (kx_ref, vx_ref, slice(BLOCK, 2 * BLOCK))]
            else:
                local = [(kx_ref, vx_ref, slice(0, BLOCK)), (kx_ref, vx_ref, slice(BLOCK, 2 * BLOCK)),
                         (kn_ref, vn_ref, slice(None))]
            keys = jnp.concatenate([kc_ref[0, :, sl]] + [kr[0, rs, sl] for kr, _, rs in local], axis=0)
            vals = jnp.concatenate([vc_ref[0, :, sl]] + [vr[0, rs, sl] for _, vr, rs in local], axis=0)
            s = _dot_nt(q2, keys) + bias2
            sk = jnp.where(rid < BLOCK, sink_ref[2 * a], sink_ref[2 * a + 1])
            m = jnp.maximum(jnp.max(s, axis=-1, keepdims=True), sk)
            e = jnp.exp(s - m)
            den = jnp.sum(e, axis=-1, keepdims=True) + jnp.exp(sk - m)
            p = (e / den).astype(BF16)
            outs.append(_unstack_heads(_dot(p, vals)))
        o_ref[0, rows, :] = jnp.concatenate(outs, axis=1).astype(BF16)


def _window_attention(p0, kv, sink, n_ctx):
    b, t, _ = p0.shape
    nb = t // BLOCK
    ncb = n_ctx // BLOCK
    assert nb % 2 == 0 and ncb % 2 == 0
    pair = lambda cidx: pl.BlockSpec((1, 2 * BLOCK, 256), lambda bb, j: (bb, j, cidx))
    prv = lambda cidx: pl.BlockSpec((1, BLOCK, 256), lambda bb, j: (bb, jnp.maximum(2 * j - 1, 0), cidx))
    nxt = lambda cidx: pl.BlockSpec((1, BLOCK, 256), lambda bb, j: (bb, jnp.minimum(2 * j + 2, nb - 1), cidx))
    ctx = lambda cidx: pl.BlockSpec((1, n_ctx, 256), lambda bb, j: (bb, 0, cidx))
    return pl.pallas_call(
        functools.partial(_win_kernel, n_ctx_blocks=ncb, n_blocks=nb),
        grid=(b, nb // 2),
        in_specs=[
            pl.BlockSpec(memory_space=pltpu.SMEM),
            pair(2),
            ctx(0), ctx(1),
            prv(0), pair(0), nxt(0),
            prv(1), pair(1), nxt(1),
        ],
        out_specs=pl.BlockSpec((1, 2 * BLOCK, 256), lambda bb, j: (bb, j, 0)),
        out_shape=jax.ShapeDtypeStruct((b, t, 256), BF16),
        compiler_params=_cparams(("parallel", "parallel")),
        name="window_attn",
    )(sink, p0, kv, kv, kv, kv, kv, kv, kv, kv)


SAFE_LOG2_BOUND = 60.0


def _glb_kernel(q_ref, k_ref, v_ref, o_ref, kmax_scr, *, tq, tk, n_ctx, t_all):
    i = pl.program_id(1)
    sls = (slice(0, 128), slice(128, 256))
    lane = lax.broadcasted_iota(jnp.int32, (2 * tq, 128), 1)

    @pl.when(i == 0)
    def _():
        klane = lax.broadcasted_iota(jnp.int32, (tk, 128), 1)
        for a in range(2):
            def kbody(ci, mx):
                start = pl.multiple_of(ci * tk, tk)
                k = k_ref[0, pl.ds(start, tk), sls[a]].astype(F32)
                sq = jnp.sum(jnp.where(klane < HEAD_DIM, k * k, 0.0), axis=-1, keepdims=True)
                return jnp.maximum(mx, jnp.max(sq, axis=0, keepdims=True))
            mx = lax.fori_loop(0, t_all // tk, kbody, jnp.zeros((1, 1), F32))
            kmax_scr[a:a + 1, :] = jnp.broadcast_to(jnp.sqrt(mx), (1, 128))

    q2s, bnds = [], []
    for a in range(2):
        q = q_ref[0, :, sls[a]].astype(F32)
        q2 = jnp.concatenate([q, pltpu.roll(q, HEAD_DIM, 1)], axis=0)
        q2 = jnp.where(lane < HEAD_DIM, q2, 0.0)
        nrm = jnp.sqrt(jnp.sum(q2 * q2, axis=-1, keepdims=True))
        bnd = nrm * kmax_scr[a:a + 1, 0:1] * 1.01
        q2s.append(q2)
        bnds.append(bnd)
    worst = jnp.max(jnp.maximum(bnds[0], bnds[1]))

    def finish(res):
        outs = []
        for acc in res:
            den = jnp.sum(jnp.where(lane == HEAD_DIM, acc, 0.0), axis=-1, keepdims=True)
            o = acc / den
            lo_t = lax.broadcasted_iota(jnp.int32, (tq, 128), 1) < HEAD_DIM
            outs.append(jnp.where(lo_t, o[:tq], pltpu.roll(o, HEAD_DIM, 1)[tq:]))
        o_ref[0] = jnp.concatenate(outs, axis=1).astype(BF16)

    def attend_fixed(n_chunks, size):
        qb = [jnp.where(lane == HEAD_DIM, -bnds[a], q2s[a]).astype(BF16) for a in range(2)]

        def body(ci, carry):
            start = pl.multiple_of(ci * size, size)
            out = []
            for a in range(2):
                k = k_ref[0, pl.ds(start, size), sls[a]]
                v = v_ref[0, pl.ds(start, size), sls[a]]
                p = jnp.exp2(_dot_nt(qb[a], k))
                out.append(carry[a] + _dot(p.astype(BF16), v))
            return tuple(out)

        zero = jnp.zeros((2 * tq, 128), F32)
        finish(lax.fori_loop(0, n_chunks, body, (zero, zero)))

    def attend_online(n_chunks, size):
        qb = [q2s[a].astype(BF16) for a in range(2)]

        def body(ci, carry):
            start = pl.multiple_of(ci * size, size)
            out = []
            for a in range(2):
                m, acc = carry[a]
                k = k_ref[0, pl.ds(start, size), sls[a]]
                v = v_ref[0, pl.ds(start, size), sls[a]]
                s = _dot_nt(qb[a], k)
                m_new = jnp.maximum(m, jnp.max(s, axis=-1, keepdims=True))
                p = jnp.exp2(s - m_new)
                acc = jnp.exp2(m - m_new) * acc + _dot(p.astype(BF16), v)
                out.append((m_new, acc))
            return tuple(out)

        init = (jnp.full((2 * tq, 1), NEG_INF, F32), jnp.zeros((2 * tq, 128), F32))
        res = lax.fori_loop(0, n_chunks, body, (init, init))
        finish([acc for (_, acc) in res])

    is_ctx = (i + 1) * tq <= n_ctx
    safe = worst <= SAFE_LOG2_BOUND

    @pl.when(jnp.logical_and(is_ctx, safe))
    def _():
        attend_fixed(1, n_ctx)

    @pl.when(jnp.logical_and(is_ctx, jnp.logical_not(safe)))
    def _():
        attend_online(1, n_ctx)

    @pl.when(jnp.logical_and(jnp.logical_not(is_ctx), safe))
    def _():
        attend_fixed(t_all // tk, tk)

    @pl.when(jnp.logical_and(jnp.logical_not(is_ctx), jnp.logical_not(safe)))
    def _():
        attend_online(t_all // tk, tk)


def _global_attention(p0, kv, n_ctx):
    b, t, _ = p0.shape
    tq = _pick_tile(n_ctx, (256, 128))
    tk = _pick_tile(t, (768, 384, 256, 128))
    return pl.pallas_call(
        functools.partial(_glb_kernel, tq=tq, tk=tk, n_ctx=n_ctx, t_all=t),
        grid=(b, t // tq),
        in_specs=[
            pl.BlockSpec((1, tq, 256), lambda bb, i: (bb, i, 4)),
            pl.BlockSpec((1, t, 256), lambda bb, i: (bb, 0, 2)),
            pl.BlockSpec((1, t, 256), lambda bb, i: (bb, 0, 3)),
        ],
        out_specs=pl.BlockSpec((1, tq, 256), lambda bb, i: (bb, i, 0)),
        out_shape=jax.ShapeDtypeStruct((b, t, 256), BF16),
        scratch_shapes=[pltpu.VMEM((8, 128), F32)],
        compiler_params=_cparams(("arbitrary", "arbitrary")),
        name="global_attn",
    )(p0, kv, kv)


_TAB_D, _TAB_Q, _TAB_K, _TAB_C, _TAB_ROWS = 0, 256, 384, 512, 640


def _ret_kernel(dec_ref, qf_ref, kf_ref, vf_ref, qb_ref, kb_ref, vb_ref, of_ref, ob_ref, st_scr, tab_scr,
                *, batch):
    i = pl.program_id(0)
    r = lax.broadcasted_iota(jnp.int32, (BLOCK, BLOCK), 0)
    c = lax.broadcasted_iota(jnp.int32, (BLOCK, BLOCK), 1)
    lane_lo = c < HEAD_DIM
    same_head = (r < HEAD_DIM) == lane_lo

    @pl.when(i == 0)
    def _():
        st_scr[...] = jnp.zeros_like(st_scr)
        x = dec_ref[...]
        lg = jnp.where(x >= 0.0, -jnp.log(1.0 + jnp.exp(-x)), x - jnp.log(1.0 + jnp.exp(x)))
        rf = r.astype(F32)
        cf = c.astype(F32)
        for d in range(2):
            for pair in range(2):
                le = lg[4 * d + 2 * pair:4 * d + 2 * pair + 1, :]
                lo_ = lg[4 * d + 2 * pair + 1:4 * d + 2 * pair + 2, :]
                lp = jnp.where(lane_lo[0:1, :], le, lo_)
                if d == 0:
                    rel = rf - cf
                    qpow = rf + 1.0
                    kpow = (BLOCK - 1.0) - rf
                else:
                    rel = cf - rf - 1.0
                    qpow = (BLOCK - 1.0) - rf
                    kpow = rf
                msk = rel >= 0.0
                relc = jnp.where(msk, rel, 0.0)
                tab_scr[d, pair, _TAB_D:_TAB_D + BLOCK, :] = jnp.where(msk, jnp.exp(le * relc), 0.0)
                tab_scr[d, pair, _TAB_D + BLOCK:_TAB_Q, :] = jnp.where(msk, jnp.exp(lo_ * relc), 0.0)
                tab_scr[d, pair, _TAB_Q:_TAB_K, :] = jnp.exp(lp * qpow)
                tab_scr[d, pair, _TAB_K:_TAB_C, :] = jnp.exp(lp * kpow)
                tab_scr[d, pair, _TAB_C:_TAB_ROWS, :] = jnp.where(
                    r < HEAD_DIM, jnp.exp(le * float(BLOCK)), jnp.exp(lo_ * float(BLOCK)))

    for d, (q_ref, k_ref, v_ref, o_ref) in enumerate(((qf_ref, kf_ref, vf_ref, of_ref),
                                                     (qb_ref, kb_ref, vb_ref, ob_ref))):
        for bb in range(batch):
            for pair in range(2):
                sl = slice(128 * pair, 128 * (pair + 1))
                q = q_ref[bb, :, sl].astype(F32)
                k = k_ref[bb, :, sl].astype(F32)
                v = v_ref[bb, :, sl]
                q2 = jnp.concatenate([jnp.where(lane_lo, q, 0.0), jnp.where(lane_lo, 0.0, q)], axis=0)
                qk = _dot_nt(q2.astype(BF16), k.astype(BF16)) * tab_scr[d, pair, _TAB_D:_TAB_Q, :]
                o_intra = _unstack_heads(_dot(qk.astype(BF16), v))
                s = st_scr[d, bb, pair]
                qd = q * tab_scr[d, pair, _TAB_Q:_TAB_K, :]
                o_ref[bb, :, sl] = o_intra + _dot(qd.astype(BF16), s.astype(BF16))
                kd = k * tab_scr[d, pair, _TAB_K:_TAB_C, :]
                upd = _dot(kd.T.astype(BF16), v)
                st_scr[d, bb, pair] = jnp.where(same_head, tab_scr[d, pair, _TAB_C:_TAB_ROWS, :] * s + upd, 0.0)


def _retention(p0, dec, n_ctx):
    b, t, _ = p0.shape
    nb = t // BLOCK
    ncb = n_ctx // BLOCK
    bwd = lambda i: jnp.where(i < ncb, ncb - 1 - i, nb - 1 + ncb - i)
    fspec = lambda cidx: pl.BlockSpec((b, BLOCK, 256), lambda i: (0, i, cidx))
    bspec = lambda cidx: pl.BlockSpec((b, BLOCK, 256), lambda i: (0, bwd(i), cidx))
    return pl.pallas_call(
        functools.partial(_ret_kernel, batch=b),
        grid=(nb,),
        in_specs=[
            pl.BlockSpec((8, 128), lambda i: (0, 0)),
            fspec(6), fspec(7), fspec(8),
            bspec(6), bspec(7), bspec(8),
        ],
        out_specs=[
            pl.BlockSpec((b, BLOCK, 256), lambda i: (0, i, 0)),
            pl.BlockSpec((b, BLOCK, 256), lambda i: (0, bwd(i), 0)),
        ],
        out_shape=[jax.ShapeDtypeStruct((b, t, 256), F32), jax.ShapeDtypeStruct((b, t, 256), F32)],
        scratch_shapes=[pltpu.VMEM((2, b, 2, BLOCK, BLOCK), F32),
                        pltpu.VMEM((2, 2, _TAB_ROWS, BLOCK), F32)],
        compiler_params=_cparams(("arbitrary",)),
        name="retention",
    )(dec, p0, p0, p0, p0, p0, p0)


def _merge_kernel(x_ref, uv_ref, batt_ref, catt_ref, of_ref, ob_ref, rg_ref, gate_ref, gl_ref, gc_ref,
                  ws_ref, bs_ref, gn_ref, bd_ref, wb_ref, wo_ref, o_ref, *, tm, n_ctx):
    i = pl.program_id(1)
    grp = jnp.right_shift(lax.broadcasted_iota(jnp.int32, (BLOCK, 256), 1), 6)
    a_chunks = []
    for ch in range(tm // BLOCK):
        rows = slice(ch * BLOCK, (ch + 1) * BLOCK)
        vn = uv_ref[0, rows, 256:512]
        mixed = bs_ref[...]
        for g in range(4):
            mixed = mixed + jnp.where(grp == g, _dot(ws_ref[g], vn), 0.0)
        a_chunks.append((uv_ref[0, rows, 0:256].astype(F32) * mixed).astype(BF16))
    a_br = jnp.concatenate(a_chunks, axis=0)
    o = of_ref[0] + ob_ref[0]
    mean = _split_dot(o, bd_ref[...]) * (1.0 / HEAD_DIM)
    oc = o - mean
    var = _split_dot(oc * oc, bd_ref[...]) * (1.0 / HEAD_DIM)
    rg = rg_ref[0].astype(F32)
    d_br = (oc * lax.rsqrt(var + EPS) * gn_ref[...] * (rg * jax.nn.sigmoid(rg))).astype(BF16)
    branches = (a_br, batt_ref[0], catt_ref[0], d_br)
    d_model = x_ref.shape[-1]
    acc = None
    for n in range(N_BRANCH):
        gate = 0.5 + 0.5 * jnp.tanh(0.5 * gate_ref[0, :, n * d_model:(n + 1) * d_model].astype(F32))
        term = gate * _dot(branches[n], wb_ref[n])
        acc = term if acc is None else acc + term
    m = _dot(acc.astype(BF16), wo_ref[...])
    row = i * tm + lax.broadcasted_iota(jnp.int32, (tm, 1), 0)
    g1 = jnp.where(row < n_ctx, gc_ref[...], gl_ref[0])
    o_ref[0] = x_ref[0] + g1 * m


def _merge(xc, p0, b_att, c_att, o_f, o_b, gates, g1_l, g1_c, ws_bf, bias, gn, bd, wb_bf, wo_bf, n_ctx):
    b, t, d = xc.shape
    tm = _pick_tile(t, (384, 256, 128))
    tok = lambda w, cidx: pl.BlockSpec((1, tm, w), lambda bb, i: (bb, i, cidx))
    full = lambda shape: pl.BlockSpec(shape, lambda bb, i: (0,) * len(shape))
    return pl.pallas_call(
        functools.partial(_merge_kernel, tm=tm, n_ctx=n_ctx),
        grid=(b, t // tm),
        in_specs=[
            tok(d, 0),
            tok(512, 0),
            tok(256, 0), tok(256, 0),
            tok(256, 0), tok(256, 0),
            tok(256, 9),
            tok(N_BRANCH * d, 0),
            pl.BlockSpec((1, 1, d), lambda bb, i: (bb, 0, 0)),
            full((1, d)),
            full((4, BLOCK, BLOCK)), full((BLOCK, 256)), full((1, 256)), full((256, 256)),
            full((N_BRANCH, BRANCH_W, d)), full((d, d)),
        ],
        out_specs=tok(d, 0),
        out_shape=jax.ShapeDtypeStruct((b, t, d), F32),
        compiler_params=_cparams(("parallel", "parallel")),
        name="merge",
    )(xc, p0, b_att, c_att, o_f, o_b, p0, gates, g1_l, g1_c, ws_bf, bias, gn, bd, wb_bf, wo_bf)


def _ffn_kernel(x_ref, modl_ref, modc_ref, g_ref, w1_ref, w3_ref, w2_ref, o_ref, h_scr, acc_scr, *, tm, n_ctx):
    i = pl.program_id(1)
    k = pl.program_id(2)

    @pl.when(k == 0)
    def _():
        h = _modulated_norm(x_ref[0], g_ref[...], modl_ref[0, 0:1, :], modl_ref[0, 1:2, :],
                            modc_ref[0:1, :], modc_ref[1:2, :], i * tm, n_ctx)
        h_scr[...] = h.astype(BF16)
        acc_scr[...] = jnp.zeros_like(acc_scr)

    h = h_scr[...]
    a = _dot(h, w1_ref[...])
    bgate = _dot(h, w3_ref[...])
    act = (a * jax.nn.sigmoid(a) * bgate).astype(BF16)
    acc_scr[...] += _dot(act, w2_ref[...])

    @pl.when(k == pl.num_programs(2) - 1)
    def _():
        row = i * tm + lax.broadcasted_iota(jnp.int32, (tm, 1), 0)
        g2 = jnp.where(row < n_ctx, modc_ref[2:3, :], modl_ref[0, 2:3, :])
        o_ref[0] = x_ref[0] + g2 * acc_scr[...]


def _ffn(xc, modl, modc, g, w1, w3, w2, n_ctx):
    b, t, d = xc.shape
    dff = w1.shape[1]
    tm = _pick_tile(t, (768, 384, 256, 128))
    tf = _pick_tile(dff, (1408, 512, 256, 128))
    return pl.pallas_call(
        functools.partial(_ffn_kernel, tm=tm, n_ctx=n_ctx),
        grid=(b, t // tm, dff // tf),
        in_specs=[
            pl.BlockSpec((1, tm, d), lambda bb, i, k: (bb, i, 0)),
            pl.BlockSpec((1, 3, d), lambda bb, i, k: (bb, 0, 0)),
            pl.BlockSpec((3, d), lambda bb, i, k: (0, 0)),
            pl.BlockSpec((1, d), lambda bb, i, k: (0, 0)),
            pl.BlockSpec((d, tf), lambda bb, i, k: (0, k)),
            pl.BlockSpec((d, tf), lambda bb, i, k: (0, k)),
            pl.BlockSpec((tf, d), lambda bb, i, k: (k, 0)),
        ],
        out_specs=pl.BlockSpec((1, tm, d), lambda bb, i, k: (bb, i, 0)),
        out_shape=jax.ShapeDtypeStruct((b, t, d), F32),
        scratch_shapes=[pltpu.VMEM((tm, d), BF16), pltpu.VMEM((tm, d), F32)],
        compiler_params=_cparams(("parallel", "parallel", "arbitrary")),
        name="ffn",
    )(xc, modl, modc, g, w1, w3, w2)


def _router_kernel(x_ref, modl_ref, modc_ref, g_ref, rw_ref, rb_ref, h_ref, ids_ref, p_ref, *, tm, n_ctx):
    i = pl.program_id(1)
    lane = lax.broadcasted_iota(jnp.int32, (tm, 128), 1)
    h = _modulated_norm(x_ref[0], g_ref[...], modl_ref[0, 0:1, :], modl_ref[0, 1:2, :],
                        modc_ref[0:1, :], modc_ref[1:2, :], i * tm, n_ctx)
    h_ref[0] = h
    h_hi = h.astype(BF16)
    h_lo = (h - h_hi.astype(F32)).astype(BF16)
    rw = rw_ref[...]
    rw_hi = rw.astype(BF16)
    rw_lo = (rw - rw_hi.astype(F32)).astype(BF16)
    logits = _dot(h_hi, rw_hi) + _dot(h_hi, rw_lo) + _dot(h_lo, rw_hi) + rb_ref[...]
    logits = jnp.where(lane < N_EXPERTS, logits, NEG_INF)
    v0 = jnp.max(logits, axis=-1, keepdims=True)
    i0 = jnp.min(jnp.where(logits == v0, lane, 128), axis=-1, keepdims=True)
    rest = jnp.where(lane == i0, NEG_INF, logits)
    v1 = jnp.max(rest, axis=-1, keepdims=True)
    i1 = jnp.min(jnp.where(rest == v1, lane, 128), axis=-1, keepdims=True)
    e1 = jnp.exp(v1 - v0)
    p0 = 1.0 / (1.0 + e1)
    ids_ref[0] = jnp.where(lane == 0, i0, jnp.where(lane == 1, i1, 0))
    p_ref[0] = jnp.where(lane == 0, p0, jnp.where(lane == 1, e1 * p0, 0.0))


def _router(xc, modl, modc, g, rw_pad, rb_pad, n_ctx):
    b, t, d = xc.shape
    tm = _pick_tile(t, (768, 384, 256, 128))
    tok = lambda w: pl.BlockSpec((1, tm, w), lambda bb, i: (bb, i, 0))
    return pl.pallas_call(
        functools.partial(_router_kernel, tm=tm, n_ctx=n_ctx),
        grid=(b, t // tm),
        in_specs=[
            tok(d),
            pl.BlockSpec((1, 3, d), lambda bb, i: (bb, 0, 0)),
            pl.BlockSpec((3, d), lambda bb, i: (0, 0)),
            pl.BlockSpec((1, d), lambda bb, i: (0, 0)),
            pl.BlockSpec((d, 128), lambda bb, i: (0, 0)),
            pl.BlockSpec((1, 128), lambda bb, i: (0, 0)),
        ],
        out_specs=[tok(d), tok(128), tok(128)],
        out_shape=[jax.ShapeDtypeStruct((b, t, d), F32), jax.ShapeDtypeStruct((b, t, 128), jnp.int32),
                   jax.ShapeDtypeStruct((b, t, 128), F32)],
        compiler_params=_cparams(("parallel", "parallel")),
        name="router",
    )(xc, modl, modc, g, rw_pad, rb_pad)


def _routing_plan(ids, tm_e):
    n_tok = ids.shape[0]
    n_asg = 2 * n_tok
    n_tiles = n_asg // tm_e + N_EXPERTS
    e_flat = jnp.concatenate([ids[:, 0], ids[:, 1]])
    onehot = (e_flat[:, None] == jnp.arange(N_EXPERTS, dtype=jnp.int32)[None, :]).astype(jnp.int32)
    csum = jnp.cumsum(onehot, axis=0)
    rank = jnp.sum(csum * onehot, axis=1) - 1
    counts = csum[-1]
    padded = ((counts + tm_e - 1) // tm_e) * tm_e
    ends = jnp.cumsum(padded)
    starts = ends - padded
    dest = jnp.sum(onehot * starts[None, :], axis=1) + rank
    row = jnp.arange(n_tiles * tm_e, dtype=jnp.int32)
    spare = n_asg + ((row // tm_e) % 2) * tm_e + row % tm_e
    asg = spare.at[dest].set(jnp.arange(n_asg, dtype=jnp.int32))
    tile_start = jnp.arange(n_tiles, dtype=jnp.int32) * tm_e
    n_used = ends[-1] // tm_e
    tile_e = jnp.minimum(jnp.sum((tile_start[:, None] >= ends[None, :]).astype(jnp.int32), axis=1), N_EXPERTS - 1)
    n_valid = jnp.clip(counts[tile_e] - (tile_start - starts[tile_e]), 0, tm_e)
    n_valid = jnp.where(tile_start < ends[-1], n_valid, 0).astype(jnp.int32)
    last_e = tile_e[jnp.maximum(n_used - 1, 0)]
    tile_e = jnp.where(tile_start < ends[-1], tile_e, last_e).astype(jnp.int32)
    src = jnp.where(asg >= n_asg, 0, jnp.where(asg >= n_tok, asg - n_tok, asg))
    return tile_e, n_valid, asg.reshape(n_tiles, 1, tm_e), src.reshape(n_tiles, 1, tm_e)


DMA_UNROLL = 8


def _expert_kernel(te_ref, nv_ref, asg_ref, src_ref, src_next_ref, h_hbm, w1_ref, w3_ref, w2_ref, y_hbm,
                   hbuf, hbf, acc, obuf, gsem, ssem, *, n_k, tm_e):
    i = pl.program_id(0)
    k = pl.program_id(1)
    n_tiles = pl.num_programs(0)
    used = nv_ref[i] > 0
    nxt = jnp.minimum(i + 1, n_tiles - 1)
    next_used = jnp.logical_and(i + 1 < n_tiles, nv_ref[nxt] > 0)
    slot = i % 2

    def issue_gather(idx_ref, s):
        def body(r, c):
            src = idx_ref[0, 0, r]
            pltpu.make_async_copy(h_hbm.at[pl.ds(src, 1), :], hbuf.at[s, pl.ds(r, 1), :], gsem.at[s]).start()
            return c
        lax.fori_loop(0, tm_e, body, 0, unroll=DMA_UNROLL)

    def wait_gather(s):
        pltpu.make_async_copy(h_hbm.at[pl.ds(0, tm_e), :], hbuf.at[s], gsem.at[s]).wait()

    def issue_scatter():
        def body(r, c):
            a = asg_ref[0, 0, r]
            pltpu.make_async_copy(obuf.at[pl.ds(r, 1), :], y_hbm.at[pl.ds(a, 1), :], ssem).start()
            return c
        lax.fori_loop(0, tm_e, body, 0, unroll=DMA_UNROLL)

    def wait_scatter():
        pltpu.make_async_copy(obuf, y_hbm.at[pl.ds(0, tm_e), :], ssem).wait()

    @pl.when(jnp.logical_and(used, k == 0))
    def _():
        @pl.when(i == 0)
        def _():
            obuf[...] = jnp.zeros_like(obuf)
            n_rows = y_hbm.shape[0]
            for base in (n_rows - 2 * tm_e, n_rows - tm_e):
                pltpu.make_async_copy(obuf, y_hbm.at[pl.ds(base, tm_e), :], ssem).start()
            wait_scatter()
            wait_scatter()
            issue_gather(src_ref, 0)
        wait_gather(slot)
        hbf[...] = hbuf[slot].astype(BF16)

        @pl.when(next_used)
        def _():
            issue_gather(src_next_ref, 1 - slot)

    @pl.when(used)
    def _():
        h = hbf[...]
        a = _dot(h, w1_ref[0])
        bgate = _dot(h, w3_ref[0])
        act = (a * jax.nn.sigmoid(a) * bgate).astype(BF16)
        part = _dot(act, w2_ref[0])

        @pl.when(k == 0)
        def _():
            acc[...] = part

        @pl.when(jnp.logical_and(k > 0, k < n_k - 1))
        def _():
            acc[...] += part

        @pl.when(k == n_k - 1)
        def _():
            @pl.when(i > 0)
            def _():
                wait_scatter()
            obuf[...] = acc[...] + part if n_k > 1 else part
            issue_scatter()

            @pl.when(jnp.logical_not(next_used))
            def _():
                wait_scatter()


def _experts(h_flat, tile_e, n_valid, asg, src, w1, w3, w2, tm_e):
    n_tok, d = h_flat.shape
    n_e, _, dff = w1.shape
    n_tiles = asg.shape[0]
    tf = _pick_tile(dff, (896, 256, 128))
    n_k = dff // tf
    kk = lambda i, k, nv: jnp.where(nv[i] > 0, k, n_k - 1)
    grid_spec = pltpu.PrefetchScalarGridSpec(
        num_scalar_prefetch=2,
        grid=(n_tiles, n_k),
        in_specs=[
            pl.BlockSpec((1, 1, tm_e), lambda i, k, te, nv: (i, 0, 0), memory_space=pltpu.SMEM),
            pl.BlockSpec((1, 1, tm_e), lambda i, k, te, nv: (i, 0, 0), memory_space=pltpu.SMEM),
            pl.BlockSpec((1, 1, tm_e), lambda i, k, te, nv: (jnp.minimum(i + 1, n_tiles - 1), 0, 0),
                         memory_space=pltpu.SMEM),
            pl.BlockSpec(memory_space=pl.ANY),
            pl.BlockSpec((1, d, tf), lambda i, k, te, nv: (te[i], 0, kk(i, k, nv))),
            pl.BlockSpec((1, d, tf), lambda i, k, te, nv: (te[i], 0, kk(i, k, nv))),
            pl.BlockSpec((1, tf, d), lambda i, k, te, nv: (te[i], kk(i, k, nv), 0)),
        ],
        out_specs=pl.BlockSpec(memory_space=pl.ANY),
        scratch_shapes=[pltpu.VMEM((2, tm_e, d), F32), pltpu.VMEM((tm_e, d), BF16), pltpu.VMEM((tm_e, d), F32),
                        pltpu.VMEM((tm_e, d), F32), pltpu.SemaphoreType.DMA((2,)), pltpu.SemaphoreType.DMA(())],
    )
    return pl.pallas_call(
        functools.partial(_expert_kernel, n_k=n_k, tm_e=tm_e),
        grid_spec=grid_spec,
        out_shape=jax.ShapeDtypeStruct((2 * n_tok + 2 * tm_e, d), F32),
        compiler_params=_cparams(("arbitrary", "arbitrary")),
        name="experts",
    )(tile_e, n_valid, asg, src, src, h_flat, w1, w3, w2)


def _combine_kernel(x_ref, y0_ref, y1_ref, p_ref, gl_ref, gc_ref, o_ref, *, tm, n_ctx, first):
    i = pl.program_id(1)
    p = p_ref[0]
    lane = lax.broadcasted_iota(jnp.int32, p.shape, 1)
    p0 = jnp.sum(jnp.where(lane == 0, p, 0.0), axis=-1, keepdims=True)
    p1 = jnp.sum(jnp.where(lane == 1, p, 0.0), axis=-1, keepdims=True)
    row = (i + first) * tm + lax.broadcasted_iota(jnp.int32, (tm, 1), 0)
    g2 = jnp.where(row < n_ctx, gc_ref[...], gl_ref[0])
    o_ref[0] = x_ref[0] + g2 * (p0 * y0_ref[...] + p1 * y1_ref[...])


def _combine(xc, y, p, g2_l, g2_c, n_ctx, latent_only):
    b, t, d = xc.shape
    tm = _pick_tile(n_ctx, (256, 128)) if latent_only else _pick_tile(t, (768, 384, 256, 128))
    first = n_ctx // tm if latent_only else 0
    tpb = t // tm
    tok = lambda w: pl.BlockSpec((1, tm, w), lambda bb, i: (bb, i + first, 0))
    ysp = lambda kk: pl.BlockSpec((tm, d), lambda bb, i: ((kk * b + bb) * tpb + i + first, 0))
    return pl.pallas_call(
        functools.partial(_combine_kernel, tm=tm, n_ctx=n_ctx, first=first),
        grid=(b, tpb - first),
        in_specs=[tok(d), ysp(0), ysp(1), tok(128),
                  pl.BlockSpec((1, 1, d), lambda bb, i: (bb, 0, 0)),
                  pl.BlockSpec((1, d), lambda bb, i: (0, 0))],
        out_specs=pl.BlockSpec((1, tm, d), lambda bb, i: (bb, i, 0)),
        out_shape=jax.ShapeDtypeStruct((b, t - first * tm, d), F32),
        compiler_params=_cparams(("parallel", "parallel")),
        name="moe_combine",
    )(xc, y, y, p, g2_l, g2_c)


MOE_TILE = 512


def _moe(xc, modl, modc, g, rw_pad, rb_pad, w1, w3, w2, n_ctx, latent_only):
    b, t, d = xc.shape
    h, ids, p = _router(xc, modl, modc, g, rw_pad, rb_pad, n_ctx)
    tile_e, n_valid, asg, src = _routing_plan(ids.reshape(b * t, 128)[:, :2], MOE_TILE)
    y = _experts(h.reshape(b * t, d), tile_e, n_valid, asg, src, w1, w3, w2, MOE_TILE)
    return _combine(xc, y, p, modl[:, 2:3], modc[2:3], n_ctx, latent_only)


def _rope_tables(n_ctx, s):
    def widen(cos, sin):
        c64 = jnp.concatenate([cos, cos], axis=-1)
        s64 = jnp.concatenate([-sin, sin], axis=-1)
        return jnp.tile(c64, (1, 2)), jnp.tile(s64, (1, 2))

    n_rows = s // GRID_W
    row = jnp.repeat(jnp.arange(n_rows, dtype=F32), GRID_W)
    col = jnp.tile(jnp.arange(GRID_W, dtype=F32), n_rows)
    n_freq = HEAD_DIM // 4
    inv = ROPE_BASE ** (-jnp.arange(n_freq, dtype=F32) / n_freq)
    ang = jnp.concatenate([row[:, None] * inv, col[:, None] * inv], axis=-1)
    cos_ax = jnp.concatenate([jnp.ones((n_ctx, HEAD_DIM // 2), F32), jnp.cos(ang)], axis=0)
    sin_ax = jnp.concatenate([jnp.zeros((n_ctx, HEAD_DIM // 2), F32), jnp.sin(ang)], axis=0)
    n_freq = HEAD_DIM // 2
    inv = ROPE_BASE ** (-jnp.arange(n_freq, dtype=F32) / n_freq)
    pos = jnp.arange(n_ctx + s, dtype=F32)
    ang = pos[:, None] * inv
    return widen(cos_ax, sin_ax) + widen(jnp.cos(ang), jnp.sin(ang))


def _block_diag_ones(n):
    g = jnp.arange(n) // HEAD_DIM
    return (g[:, None] == g[None, :]).astype(BF16)


def kernel(x, c, ctx, c_ctx, w_mod, b_mod, g_norm1, g_norm2, w_in, mlp_g_v, mlp_w_s, mlp_b_s, win_q_norm,
           win_k_norm, win_sink, glb_q_norm, glb_k_norm, ret_decay, ret_gn, w_branch, w_out, ffn_w1, ffn_w3,
           ffn_w2, router_w, router_b, moe_w1, moe_w3, moe_w2):
    batch, s, d = x.shape
    n_ctx = ctx.shape[1]
    depth = w_in.shape[0]
    assert batch + 1 <= 8 and n_ctx % BLOCK == 0 and s % BLOCK == 0

    xc = jnp.concatenate([ctx, x], axis=1)
    cvec = jnp.zeros((8, d), F32).at[:batch].set(c).at[batch].set(c_ctx)
    mod = _modulation(cvec, w_mod, b_mod)
    mod = mod.reshape(depth, 8, 6, d)

    tabs = _rope_tables(n_ctx, s)
    bd256 = _block_diag_ones(256)
    ones = lambda n: jnp.ones((n,), F32)
    zeros = lambda n: jnp.zeros((n,), F32)
    scale = HEAD_DIM ** -0.5

    for layer in range(depth):
        ml = mod[layer]
        modl1 = ml[:batch, 0:2]
        modc1 = ml[batch, 0:2]
        g1_l = ml[:batch, 2:3]
        g1_c = ml[batch, 2:3]
        modl2 = ml[:batch, 3:6]
        modc2 = ml[batch, 3:6]

        def qkv_rows(gq, gk):
            return jnp.concatenate([jnp.tile(gq, 4), jnp.tile(gk, 2), ones(128)])

        gain = jnp.stack([
            jnp.concatenate([ones(256), mlp_g_v[layer].reshape(-1)]),
            qkv_rows(win_q_norm[layer], win_k_norm[layer]),
            qkv_rows(glb_q_norm[layer], glb_k_norm[layer]),
            ones(IN_TILE)])
        qk_mask = jnp.concatenate([ones(384), zeros(128)])
        nmask = jnp.stack([jnp.concatenate([zeros(256), ones(256)]), qk_mask, qk_mask, zeros(IN_TILE)])
        q_scale = jnp.concatenate([jnp.full((256,), scale, F32), ones(256)])
        q_scale2 = jnp.concatenate([jnp.full((256,), scale * LOG2_E, F32), ones(256)])
        post = jnp.stack([ones(IN_TILE), q_scale, q_scale2,
                          jnp.concatenate([ones(256), jnp.full((256,), scale, F32)])])

        p0, kv, gates = _inproj(xc, modl1, modc1, g_norm1[layer][None, :], w_in[layer].astype(BF16),
                                (gain, nmask, post, bd256), tabs, n_ctx)
        b_att = _window_attention(p0, kv, win_sink[layer], n_ctx)
        c_att = _global_attention(p0, kv, n_ctx)
        dec = jnp.broadcast_to(ret_decay[layer].reshape(-1, 1), (8, 128))
        o_f, o_b = _retention(p0, dec, n_ctx)
        bias = jnp.repeat(mlp_b_s[layer].T, HEAD_DIM, axis=1)
        xc = _merge(xc, p0, b_att, c_att, o_f, o_b, gates, g1_l, g1_c, mlp_w_s[layer].astype(BF16), bias,
                    ret_gn[layer].reshape(1, -1), bd256, w_branch[layer].astype(BF16),
                    w_out[layer].astype(BF16), n_ctx)
        j = layer // 2
        g2 = g_norm2[layer][None, :]
        if layer % 2 == 0:
            xc = _ffn(xc, modl2, modc2, g2, ffn_w1[j].astype(BF16), ffn_w3[j].astype(BF16),
                      ffn_w2[j].astype(BF16), n_ctx)
        else:
            rw_pad = jnp.zeros((d, 128), F32).at[:, :N_EXPERTS].set(router_w[j])
            rb_pad = jnp.zeros((1, 128), F32).at[0, :N_EXPERTS].set(router_b[j])
            xc = _moe(xc, modl2, modc2, g2, rw_pad, rb_pad, moe_w1[j].astype(BF16), moe_w3[j].astype(BF16),
                      moe_w2[j].astype(BF16), n_ctx, latent_only=(layer == depth - 1))
    return xc if depth % 2 == 0 else xc[:, n_ctx:]
```

```python
import functools

import jax
import jax.numpy as jnp
from jax import lax
from jax.experimental import pallas as pl
from jax.experimental.pallas import tpu as pltpu

F32 = jnp.float32
BF16 = jnp.bfloat16

HEAD_DIM = 64
BLOCK = 128
GRID_W = 64
ROPE_BASE = 10000.0
N_BRANCH = 4
BRANCH_W = 256
N_EXPERTS = 8
EPS = 1e-6
NEG_INF = -1e30
LOG2_E = 1.4426950408889634
IN_TILE = 512
N_MIX_TILES = 5
VMEM_LIMIT = 56 * 1024 * 1024


def _cparams(sem):
    return pltpu.CompilerParams(dimension_semantics=sem, vmem_limit_bytes=VMEM_LIMIT)


def _dot(a, b):
    return jnp.dot(a, b, preferred_element_type=F32)


def _dot_nt(a, b):
    return lax.dot_general(a, b, (((1,), (1,)), ((), ())), preferred_element_type=F32)


def _split_dot(a, b_bf16):
    hi = a.astype(BF16)
    lo = (a - hi.astype(F32)).astype(BF16)
    return _dot(hi, b_bf16) + _dot(lo, b_bf16)


def _pick_tile(n, candidates):
    for c in candidates:
        if n % c == 0:
            return c
    raise ValueError(f"no tile for {n} in {candidates}")


def _modulated_norm(x, g, sh_l, sc_l, sh_c, sc_c, row0, n_ctx):
    tm = x.shape[0]
    ms = jnp.mean(x * x, axis=-1, keepdims=True)
    y = x * lax.rsqrt(ms + EPS) * g
    row = row0 + lax.broadcasted_iota(jnp.int32, (tm, 1), 0)
    is_ctx = row < n_ctx
    sh = jnp.where(is_ctx, sh_c, sh_l)
    sc = jnp.where(is_ctx, sc_c, sc_l)
    return y * (1.0 + sc) + sh


def _mod_kernel(c_ref, w_ref, b_ref, o_ref):
    c = c_ref[...]
    s = c * jax.nn.sigmoid(c)
    s_hi = s.astype(BF16)
    s_lo = (s - s_hi.astype(F32)).astype(BF16)
    w = w_ref[0]
    w_hi = w.astype(BF16)
    w_lo = (w - w_hi.astype(F32)).astype(BF16)
    o_ref[0] = _dot(s_hi, w_hi) + _dot(s_hi, w_lo) + _dot(s_lo, w_hi) + b_ref[0]


def _modulation(cvec, w_mod, b_mod):
    depth, d, n = w_mod.shape
    tn = _pick_tile(n, (1536, 1024, 512, 128))
    return pl.pallas_call(
        _mod_kernel,
        grid=(depth, n // tn),
        in_specs=[
            pl.BlockSpec((8, d), lambda l, j: (0, 0)),
            pl.BlockSpec((1, d, tn), lambda l, j: (l, 0, j)),
            pl.BlockSpec((1, 1, tn), lambda l, j: (l, 0, j)),
        ],
        out_specs=pl.BlockSpec((1, 8, tn), lambda l, j: (l, 0, j)),
        out_shape=jax.ShapeDtypeStruct((depth, 8, n), F32),
        compiler_params=_cparams(("parallel", "parallel")),
        name="modulation",
    )(cvec, w_mod, b_mod.reshape(depth, 1, n))


def _group_rms(v, bd, gain):
    sq = (v * v).astype(BF16)
    half = bd.shape[0]
    parts = []
    for c in range(0, v.shape[-1], half):
        wd = min(half, v.shape[-1] - c)
        parts.append(_dot(sq[:, c:c + wd], bd[:wd, :wd]))
    gs = parts[0] if len(parts) == 1 else jnp.concatenate(parts, axis=1)
    return v * lax.rsqrt(gs * (1.0 / HEAD_DIM) + EPS) * gain


def _rope(v, cos, sin):
    w = v.shape[-1]
    lane = lax.broadcasted_iota(jnp.int32, v.shape, 1)
    first = (lane & 32) == 0
    partner = jnp.where(first, pltpu.roll(v, w - 32, 1), pltpu.roll(v, 32, 1))
    return v * cos + partner * sin


def _tile(t, n):
    return jnp.concatenate([t] * n, axis=1)


def _inproj_kernel(x_ref, modl_ref, modc_ref, g_ref, w_ref, gain_ref, post_ref, bd_ref,
                   cax_ref, sax_ref, csq_ref, ssq_ref, p_ref, kv_ref, gate_ref, *, tm, n_ctx, n_tiles):
    i = pl.program_id(1)
    h = _modulated_norm(x_ref[0], g_ref[...], modl_ref[0, 0:1, :], modl_ref[0, 1:2, :],
                        modc_ref[0:1, :], modc_ref[1:2, :], i * tm, n_ctx).astype(BF16)
    lane = lax.broadcasted_iota(jnp.int32, (tm, 128), 1)
    lo = lane < HEAD_DIM

    def attn_tile(acc, j, duplicate):
        c0 = j * IN_TILE
        qk = _group_rms(acc[:, :384], bd_ref[...], gain_ref[j:j + 1, :])
        qk = _rope(qk, _tile(cax_ref[...], 3), _tile(sax_ref[...], 3))
        vv = acc[:, 384:512]
        p_ref[0, :, c0:c0 + 384] = qk.astype(BF16)
        p_ref[0, :, c0 + 384:c0 + 512] = vv.astype(BF16)
        k = qk[:, 256:384]
        kr = pltpu.roll(k, HEAD_DIM, 1)
        vr = pltpu.roll(vv, HEAD_DIM, 1)
        if duplicate:
            parts = [jnp.where(lo, k, kr), jnp.where(lo, kr, k), jnp.where(lo, vv, vr), jnp.where(lo, vr, vv)]
        else:
            pad = jnp.where(lane == HEAD_DIM, 1.0, 0.0)
            parts = [jnp.where(lo, k, pad), jnp.where(lo, kr, pad), jnp.where(lo, vv, pad), jnp.where(lo, vr, pad)]
        for n, part in enumerate(parts):
            c = (j - 1) * IN_TILE + n * 128
            kv_ref[0, :, c:c + 128] = part.astype(BF16)

    for j in range(n_tiles):
        c0 = j * IN_TILE
        acc = _dot(h, w_ref[:, c0:c0 + IN_TILE])
        if j == 0:
            v = jax.nn.gelu(acc)
            p_ref[0, :, 0:256] = v[:, :256].astype(BF16)
            p_ref[0, :, 256:512] = _group_rms(v[:, 256:], bd_ref[...], gain_ref[0:1, :256]).astype(BF16)
        elif j == 1:
            attn_tile(acc, 1, True)
        elif j == 2:
            attn_tile(acc, 2, False)
        elif j == 3:
            v = _rope(acc, _tile(csq_ref[...], 4), _tile(ssq_ref[...], 4)) * post_ref[...]
            p_ref[0, :, c0:c0 + IN_TILE] = v.astype(BF16)
        elif j < N_MIX_TILES:
            p_ref[0, :, c0:c0 + IN_TILE] = acc.astype(BF16)
        else:
            g0 = c0 - N_MIX_TILES * IN_TILE
            gate_ref[0, :, g0:g0 + IN_TILE] = acc.astype(BF16)


def _inproj(xc, modl, modc, g, w_bf, ep, tabs, n_ctx):
    b, t, d = xc.shape
    n_in = w_bf.shape[1]
    tm = _pick_tile(t, (384, 256, 128))
    gain, post, bd = ep
    cax, sax, csq, ssq = tabs
    n_gate = n_in - N_MIX_TILES * IN_TILE
    tok = lambda bb, i: (i, 0)
    full2 = lambda bb, i: (0, 0)
    out = lambda w: pl.BlockSpec((1, tm, w), lambda bb, i: (bb, i, 0))
    return pl.pallas_call(
        functools.partial(_inproj_kernel, tm=tm, n_ctx=n_ctx, n_tiles=n_in // IN_TILE),
        grid=(b, t // tm),
        in_specs=[
            pl.BlockSpec((1, tm, d), lambda bb, i: (bb, i, 0)),
            pl.BlockSpec((1, 2, d), lambda bb, i: (bb, 0, 0)),
            pl.BlockSpec((2, d), full2),
            pl.BlockSpec((1, d), full2),
            pl.BlockSpec((d, n_in), full2),
            pl.BlockSpec((3, 384), full2),
            pl.BlockSpec((1, IN_TILE), full2),
            pl.BlockSpec((256, 256), full2),
            pl.BlockSpec((tm, 128), tok),
            pl.BlockSpec((tm, 128), tok),
            pl.BlockSpec((tm, 128), tok),
            pl.BlockSpec((tm, 128), tok),
        ],
        out_specs=[out(N_MIX_TILES * IN_TILE), out(2 * IN_TILE), out(n_gate)],
        out_shape=[
            jax.ShapeDtypeStruct((b, t, N_MIX_TILES * IN_TILE), BF16),
            jax.ShapeDtypeStruct((b, t, 2 * IN_TILE), BF16),
            jax.ShapeDtypeStruct((b, t, n_gate), BF16),
        ],
        compiler_params=_cparams(("parallel", "parallel")),
        name="inproj",
    )(xc, modl, modc, g, w_bf, gain, post, bd, cax, sax, csq, ssq)


def _stack_heads(qa):
    qf = qa.astype(F32)
    lo = lax.broadcasted_iota(jnp.int32, qf.shape, 1) < HEAD_DIM
    return jnp.concatenate([jnp.where(lo, qf, 0.0), jnp.where(lo, 0.0, qf)], axis=0).astype(BF16)


def _unstack_heads(o2):
    n = o2.shape[0] // 2
    lo = lax.broadcasted_iota(jnp.int32, (n, o2.shape[1]), 1) < HEAD_DIM
    return jnp.where(lo, o2[:n], o2[n:])


def _win_kernel(sink_ref, q_ref, kc_ref, vc_ref, kp_ref, kx_ref, kn_ref, vp_ref, vx_ref, vn_ref, o_ref,
                *, n_ctx_blocks, n_blocks):
    j = pl.program_id(1)
    lc = n_ctx_blocks * BLOCK
    w = lc + 3 * BLOCK
    row = lax.broadcasted_iota(jnp.int32, (BLOCK, w), 0)
    col = lax.broadcasted_iota(jnp.int32, (BLOCK, w), 1)
    c = col - lc
    rid = lax.broadcasted_iota(jnp.int32, (2 * BLOCK, 1), 0)
    for sub in range(2):
        n = 2 * j + sub
        rows = slice(sub * BLOCK, (sub + 1) * BLOCK)
        is_lat = n >= n_ctx_blocks
        prev_ok = n > n_ctx_blocks
        next_ok = n < n_blocks - 1
        lo_b = jnp.where(prev_ok, row, BLOCK)
        hi_b = jnp.where(next_ok, row + 2 * BLOCK, 2 * BLOCK - 1)
        lat_bias = jnp.where(is_lat, 0.0, NEG_INF)
        bias = jnp.where(col < lc, 0.0, jnp.where(c >= lo_b, jnp.where(c <= hi_b, lat_bias, NEG_INF), NEG_INF))
        bias2 = jnp.concatenate([bias, bias], axis=0)
        outs = []
        for a in range(2):
            sl = slice(128 * a, 128 * (a + 1))
            q2 = _stack_heads(q_ref[0, rows, sl])
            if sub == 0:
                local = [(kp_ref, vp_ref, slice(None)), (kx_ref, vx_ref, slice(0, BLOCK)),
                         (kx_ref, vx_ref, slice(BLOCK, 2 * BLOCK))]
            else:
                local = [(kx_ref, vx_ref, slice(0, BLOCK)), (kx_ref, vx_ref, slice(BLOCK, 2 * BLOCK)),
                         (kn_ref, vn_ref, slice(None))]
            keys = jnp.concatenate([kc_ref[0, :, sl]] + [kr[0, rs, sl] for kr, _, rs in local], axis=0)
            vals = jnp.concatenate([vc_ref[0, :, sl]] + [vr[0, rs, sl] for _, vr, rs in local], axis=0)
            s = _dot_nt(q2, keys) + bias2
            sk = jnp.where(rid < BLOCK, sink_ref[2 * a], sink_ref[2 * a + 1])
            m = jnp.maximum(jnp.max(s, axis=-1, keepdims=True), sk)
            e = jnp.exp(s - m)
            den = jnp.sum(e, axis=-1, keepdims=True) + jnp.exp(sk - m)
            p = (e / den).astype(BF16)
            outs.append(_unstack_heads(_dot(p, vals)))
        o_ref[0, rows, :] = jnp.concatenate(outs, axis=1).astype(BF16)


def _window_attention(p0, kv, sink, n_ctx):
    b, t, _ = p0.shape
    nb = t // BLOCK
    ncb = n_ctx // BLOCK
    assert nb % 2 == 0 and ncb % 2 == 0
    pair = lambda cidx: pl.BlockSpec((1, 2 * BLOCK, 256), lambda bb, j: (bb, j, cidx))
    prv = lambda cidx: pl.BlockSpec((1, BLOCK, 256), lambda bb, j: (bb, jnp.maximum(2 * j - 1, 0), cidx))
    nxt = lambda cidx: pl.BlockSpec((1, BLOCK, 256), lambda bb, j: (bb, jnp.minimum(2 * j + 2, nb - 1), cidx))
    ctx = lambda cidx: pl.BlockSpec((1, n_ctx, 256), lambda bb, j: (bb, 0, cidx))
    return pl.pallas_call(
        functools.partial(_win_kernel, n_ctx_blocks=ncb, n_blocks=nb),
        grid=(b, nb // 2),
        in_specs=[
            pl.BlockSpec(memory_space=pltpu.SMEM),
            pair(2),
            ctx(0), ctx(1),
            prv(0), pair(0), nxt(0),
            prv(1), pair(1), nxt(1),
        ],
        out_specs=pl.BlockSpec((1, 2 * BLOCK, 256), lambda bb, j: (bb, j, 0)),
        out_shape=jax.ShapeDtypeStruct((b, t, 256), BF16),
        compiler_params=_cparams(("parallel", "parallel")),
        name="window_attn",
    )(sink, p0, kv, kv, kv, kv, kv, kv, kv, kv)


SAFE_LOG2_BOUND = 60.0


def _glb_kernel(q_ref, k_ref, v_ref, o_ref, kmax_scr, *, tq, tk, n_ctx, t_all):
    i = pl.program_id(1)
    sls = (slice(0, 128), slice(128, 256))
    lane = lax.broadcasted_iota(jnp.int32, (2 * tq, 128), 1)

    @pl.when(i == 0)
    def _():
        klane = lax.broadcasted_iota(jnp.int32, (tk, 128), 1)
        for a in range(2):
            def kbody(ci, mx):
                start = pl.multiple_of(ci * tk, tk)
                k = k_ref[0, pl.ds(start, tk), sls[a]].astype(F32)
                sq = jnp.sum(jnp.where(klane < HEAD_DIM, k * k, 0.0), axis=-1, keepdims=True)
                return jnp.maximum(mx, jnp.max(sq, axis=0, keepdims=True))
            mx = lax.fori_loop(0, t_all // tk, kbody, jnp.zeros((1, 1), F32))
            kmax_scr[a:a + 1, :] = jnp.broadcast_to(jnp.sqrt(mx), (1, 128))

    q2s, bnds = [], []
    for a in range(2):
        q = q_ref[0, :, sls[a]].astype(F32)
        q2 = jnp.concatenate([q, pltpu.roll(q, HEAD_DIM, 1)], axis=0)
        q2 = jnp.where(lane < HEAD_DIM, q2, 0.0)
        nrm = jnp.sqrt(jnp.sum(q2 * q2, axis=-1, keepdims=True))
        bnd = nrm * kmax_scr[a:a + 1, 0:1] * 1.01
        q2s.append(q2)
        bnds.append(bnd)
    worst = jnp.max(jnp.maximum(bnds[0], bnds[1]))

    def finish(res):
        outs = []
        for acc in res:
            den = jnp.sum(jnp.where(lane == HEAD_DIM, acc, 0.0), axis=-1, keepdims=True)
            o = acc / den
            lo_t = lax.broadcasted_iota(jnp.int32, (tq, 128), 1) < HEAD_DIM
            outs.append(jnp.where(lo_t, o[:tq], pltpu.roll(o, HEAD_DIM, 1)[tq:]))
        o_ref[0] = jnp.concatenate(outs, axis=1).astype(BF16)

    def attend_fixed(n_chunks, size):
        qb = [jnp.where(lane == HEAD_DIM, -bnds[a], q2s[a]).astype(BF16) for a in range(2)]

        def body(ci, carry):
            start = pl.multiple_of(ci * size, size)
            out = []
            for a in range(2):
                k = k_ref[0, pl.ds(start, size), sls[a]]
                v = v_ref[0, pl.ds(start, size), sls[a]]
                p = jnp.exp2(_dot_nt(qb[a], k))
                out.append(carry[a] + _dot(p.astype(BF16), v))
            return tuple(out)

        zero = jnp.zeros((2 * tq, 128), F32)
        finish(lax.fori_loop(0, n_chunks, body, (zero, zero)))

    def attend_online(n_chunks, size):
        qb = [q2s[a].astype(BF16) for a in range(2)]

        def body(ci, carry):
            start = pl.multiple_of(ci * size, size)
            out = []
            for a in range(2):
                m, acc = carry[a]
                k = k_ref[0, pl.ds(start, size), sls[a]]
                v = v_ref[0, pl.ds(start, size), sls[a]]
                s = _dot_nt(qb[a], k)
                m_new = jnp.maximum(m, jnp.max(s, axis=-1, keepdims=True))
                p = jnp.exp2(s - m_new)
                acc = jnp.exp2(m - m_new) * acc + _dot(p.astype(BF16), v)
                out.append((m_new, acc))
            return tuple(out)

        init = (jnp.full((2 * tq, 1), NEG_INF, F32), jnp.zeros((2 * tq, 128), F32))
        res = lax.fori_loop(0, n_chunks, body, (init, init))
        finish([acc for (_, acc) in res])

    is_ctx = (i + 1) * tq <= n_ctx
    safe = worst <= SAFE_LOG2_BOUND

    @pl.when(jnp.logical_and(is_ctx, safe))
    def _():
        attend_fixed(1, n_ctx)

    @pl.when(jnp.logical_and(is_ctx, jnp.logical_not(safe)))
    def _():
        attend_online(1, n_ctx)

    @pl.when(jnp.logical_and(jnp.logical_not(is_ctx), safe))
    def _():
        attend_fixed(t_all // tk, tk)

    @pl.when(jnp.logical_and(jnp.logical_not(is_ctx), jnp.logical_not(safe)))
    def _():
        attend_online(t_all // tk, tk)


def _global_attention(p0, kv, n_ctx):
    b, t, _ = p0.shape
    tq = _pick_tile(n_ctx, (256, 128))
    tk = _pick_tile(t, (1408, 768, 384, 256, 128))
    return pl.pallas_call(
        functools.partial(_glb_kernel, tq=tq, tk=tk, n_ctx=n_ctx, t_all=t),
        grid=(b, t // tq),
        in_specs=[
            pl.BlockSpec((1, tq, 256), lambda bb, i: (bb, i, 4)),
            pl.BlockSpec((1, t, 256), lambda bb, i: (bb, 0, 2)),
            pl.BlockSpec((1, t, 256), lambda bb, i: (bb, 0, 3)),
        ],
        out_specs=pl.BlockSpec((1, tq, 256), lambda bb, i: (bb, i, 0)),
        out_shape=jax.ShapeDtypeStruct((b, t, 256), BF16),
        scratch_shapes=[pltpu.VMEM((8, 128), F32)],
        compiler_params=_cparams(("arbitrary", "arbitrary")),
        name="global_attn",
    )(p0, kv, kv)


_TAB_D, _TAB_Q, _TAB_K, _TAB_C, _TAB_ROWS = 0, 256, 384, 512, 640


def _ret_kernel(dec_ref, qf_ref, kf_ref, vf_ref, qb_ref, kb_ref, vb_ref, of_ref, ob_ref, st_scr, tab_scr,
                *, batch):
    i = pl.program_id(0)
    r = lax.broadcasted_iota(jnp.int32, (BLOCK, BLOCK), 0)
    c = lax.broadcasted_iota(jnp.int32, (BLOCK, BLOCK), 1)
    lane_lo = c < HEAD_DIM
    same_head = (r < HEAD_DIM) == lane_lo

    @pl.when(i == 0)
    def _():
        st_scr[...] = jnp.zeros_like(st_scr)
        x = dec_ref[...]
        lg = jnp.where(x >= 0.0, -jnp.log(1.0 + jnp.exp(-x)), x - jnp.log(1.0 + jnp.exp(x)))
        rf = r.astype(F32)
        cf = c.astype(F32)
        for d in range(2):
            for pair in range(2):
                le = lg[4 * d + 2 * pair:4 * d + 2 * pair + 1, :]
                lo_ = lg[4 * d + 2 * pair + 1:4 * d + 2 * pair + 2, :]
                lp = jnp.where(lane_lo[0:1, :], le, lo_)
                if d == 0:
                    rel = rf - cf
                    qpow = rf + 1.0
                    kpow = (BLOCK - 1.0) - rf
                else:
                    rel = cf - rf - 1.0
                    qpow = (BLOCK - 1.0) - rf
                    kpow = rf
                msk = rel >= 0.0
                relc = jnp.where(msk, rel, 0.0)
                tab_scr[d, pair, _TAB_D:_TAB_D + BLOCK, :] = jnp.where(msk, jnp.exp(le * relc), 0.0)
                tab_scr[d, pair, _TAB_D + BLOCK:_TAB_Q, :] = jnp.where(msk, jnp.exp(lo_ * relc), 0.0)
                tab_scr[d, pair, _TAB_Q:_TAB_K, :] = jnp.exp(lp * qpow)
                tab_scr[d, pair, _TAB_K:_TAB_C, :] = jnp.exp(lp * kpow)
                tab_scr[d, pair, _TAB_C:_TAB_ROWS, :] = jnp.where(
                    r < HEAD_DIM, jnp.exp(le * float(BLOCK)), jnp.exp(lo_ * float(BLOCK)))

    for d, (q_ref, k_ref, v_ref, o_ref) in enumerate(((qf_ref, kf_ref, vf_ref, of_ref),
                                                     (qb_ref, kb_ref, vb_ref, ob_ref))):
        for bb in range(batch):
            for pair in range(2):
                sl = slice(128 * pair, 128 * (pair + 1))
                q = q_ref[bb, :, sl].astype(F32)
                k = k_ref[bb, :, sl].astype(F32)
                v = v_ref[bb, :, sl]
                q2 = jnp.concatenate([jnp.where(lane_lo, q, 0.0), jnp.where(lane_lo, 0.0, q)], axis=0)
                qk = _dot_nt(q2.astype(BF16), k.astype(BF16)) * tab_scr[d, pair, _TAB_D:_TAB_Q, :]
                o_intra = _unstack_heads(_dot(qk.astype(BF16), v))
                s = st_scr[d, bb, pair]
                qd = q * tab_scr[d, pair, _TAB_Q:_TAB_K, :]
                o_ref[bb, :, sl] = o_intra + _dot(qd.astype(BF16), s.astype(BF16))
                kd = k * tab_scr[d, pair, _TAB_K:_TAB_C, :]
                upd = _dot(kd.T.astype(BF16), v)
                st_scr[d, bb, pair] = jnp.where(same_head, tab_scr[d, pair, _TAB_C:_TAB_ROWS, :] * s + upd, 0.0)


def _retention(p0, dec, n_ctx):
    b, t, _ = p0.shape
    nb = t // BLOCK
    ncb = n_ctx // BLOCK
    bwd = lambda i: jnp.where(i < ncb, ncb - 1 - i, nb - 1 + ncb - i)
    fspec = lambda cidx: pl.BlockSpec((b, BLOCK, 256), lambda i: (0, i, cidx))
    bspec = lambda cidx: pl.BlockSpec((b, BLOCK, 256), lambda i: (0, bwd(i), cidx))
    return pl.pallas_call(
        functools.partial(_ret_kernel, batch=b),
        grid=(nb,),
        in_specs=[
            pl.BlockSpec((8, 128), lambda i: (0, 0)),
            fspec(6), fspec(7), fspec(8),
            bspec(6), bspec(7), bspec(8),
        ],
        out_specs=[
            pl.BlockSpec((b, BLOCK, 256), lambda i: (0, i, 0)),
            pl.BlockSpec((b, BLOCK, 256), lambda i: (0, bwd(i), 0)),
        ],
        out_shape=[jax.ShapeDtypeStruct((b, t, 256), F32), jax.ShapeDtypeStruct((b, t, 256), F32)],
        scratch_shapes=[pltpu.VMEM((2, b, 2, BLOCK, BLOCK), F32),
                        pltpu.VMEM((2, 2, _TAB_ROWS, BLOCK), F32)],
        compiler_params=_cparams(("arbitrary",)),
        name="retention",
    )(dec, p0, p0, p0, p0, p0, p0)


def _merge_kernel(x_ref, uv_ref, batt_ref, catt_ref, of_ref, ob_ref, rg_ref, gate_ref, gl_ref, gc_ref,
                  ws_ref, bs_ref, gn_ref, bd_ref, wb_ref, wo_ref, o_ref, *, tm, n_ctx):
    i = pl.program_id(1)
    grp = jnp.right_shift(lax.broadcasted_iota(jnp.int32, (BLOCK, 256), 1), 6)
    a_chunks = []
    for ch in range(tm // BLOCK):
        rows = slice(ch * BLOCK, (ch + 1) * BLOCK)
        vn = uv_ref[0, rows, 256:512]
        mixed = bs_ref[...]
        for g in range(4):
            mixed = mixed + jnp.where(grp == g, _dot(ws_ref[g], vn), 0.0)
        a_chunks.append((uv_ref[0, rows, 0:256].astype(F32) * mixed).astype(BF16))
    a_br = jnp.concatenate(a_chunks, axis=0)
    o = of_ref[0] + ob_ref[0]
    mean = _split_dot(o, bd_ref[...]) * (1.0 / HEAD_DIM)
    oc = o - mean
    var = _split_dot(oc * oc, bd_ref[...]) * (1.0 / HEAD_DIM)
    rg = rg_ref[0].astype(F32)
    d_br = (oc * lax.rsqrt(var + EPS) * gn_ref[...] * (rg * jax.nn.sigmoid(rg))).astype(BF16)
    branches = (a_br, batt_ref[0], catt_ref[0], d_br)
    d_model = x_ref.shape[-1]
    acc = None
    for n in range(N_BRANCH):
        gate = 0.5 + 0.5 * jnp.tanh(0.5 * gate_ref[0, :, n * d_model:(n + 1) * d_model].astype(F32))
        term = gate * _dot(branches[n], wb_ref[n])
        acc = term if acc is None else acc + term
    m = _dot(acc.astype(BF16), wo_ref[...])
    row = i * tm + lax.broadcasted_iota(jnp.int32, (tm, 1), 0)
    g1 = jnp.where(row < n_ctx, gc_ref[...], gl_ref[0])
    o_ref[0] = x_ref[0] + g1 * m


def _merge(xc, p0, b_att, c_att, o_f, o_b, gates, g1_l, g1_c, ws_bf, bias, gn, bd, wb_bf, wo_bf, n_ctx):
    b, t, d = xc.shape
    tm = _pick_tile(t, (384, 256, 128))
    tok = lambda w, cidx: pl.BlockSpec((1, tm, w), lambda bb, i: (bb, i, cidx))
    full = lambda shape: pl.BlockSpec(shape, lambda bb, i: (0,) * len(shape))
    return pl.pallas_call(
        functools.partial(_merge_kernel, tm=tm, n_ctx=n_ctx),
        grid=(b, t // tm),
        in_specs=[
            tok(d, 0),
            tok(512, 0),
            tok(256, 0), tok(256, 0),
            tok(256, 0), tok(256, 0),
            tok(256, 9),
            tok(N_BRANCH * d, 0),
            pl.BlockSpec((1, 1, d), lambda bb, i: (bb, 0, 0)),
            full((1, d)),
            full((4, BLOCK, BLOCK)), full((BLOCK, 256)), full((1, 256)), full((256, 256)),
            full((N_BRANCH, BRANCH_W, d)), full((d, d)),
        ],
        out_specs=tok(d, 0),
        out_shape=jax.ShapeDtypeStruct((b, t, d), F32),
        compiler_params=_cparams(("parallel", "parallel")),
        name="merge",
    )(xc, p0, b_att, c_att, o_f, o_b, p0, gates, g1_l, g1_c, ws_bf, bias, gn, bd, wb_bf, wo_bf)


FF_CHUNK = 512


def _swiglu_chunks(h, w1_ref, w3_ref, w2_ref, lead):
    tf = w1_ref.shape[-1]
    out = None
    for c0 in range(0, tf, FF_CHUNK):
        c1 = min(c0 + FF_CHUNK, tf)
        a = _dot(h, w1_ref[lead + (slice(None), slice(c0, c1))])
        bgate = _dot(h, w3_ref[lead + (slice(None), slice(c0, c1))])
        act = (a * jax.nn.sigmoid(a) * bgate).astype(BF16)
        part = _dot(act, w2_ref[lead + (slice(c0, c1), slice(None))])
        out = part if out is None else out + part
    return out


def _ffn_kernel(x_ref, modl_ref, modc_ref, g_ref, w1_ref, w3_ref, w2_ref, o_ref, h_scr, acc_scr, *, tm, n_ctx):
    i = pl.program_id(1)
    k = pl.program_id(2)

    @pl.when(k == 0)
    def _():
        h = _modulated_norm(x_ref[0], g_ref[...], modl_ref[0, 0:1, :], modl_ref[0, 1:2, :],
                            modc_ref[0:1, :], modc_ref[1:2, :], i * tm, n_ctx)
        h_scr[...] = h.astype(BF16)
        acc_scr[...] = jnp.zeros_like(acc_scr)

    acc_scr[...] += _swiglu_chunks(h_scr[...], w1_ref, w3_ref, w2_ref, ())

    @pl.when(k == pl.num_programs(2) - 1)
    def _():
        row = i * tm + lax.broadcasted_iota(jnp.int32, (tm, 1), 0)
        g2 = jnp.where(row < n_ctx, modc_ref[2:3, :], modl_ref[0, 2:3, :])
        o_ref[0] = x_ref[0] + g2 * acc_scr[...]


def _ffn(xc, modl, modc, g, w1, w3, w2, n_ctx):
    b, t, d = xc.shape
    dff = w1.shape[1]
    tm = _pick_tile(t, (768, 384, 256, 128))
    tf = _pick_tile(dff, (1408, 512, 256, 128))
    return pl.pallas_call(
        functools.partial(_ffn_kernel, tm=tm, n_ctx=n_ctx),
        grid=(b, t // tm, dff // tf),
        in_specs=[
            pl.BlockSpec((1, tm, d), lambda bb, i, k: (bb, i, 0)),
            pl.BlockSpec((1, 3, d), lambda bb, i, k: (bb, 0, 0)),
            pl.BlockSpec((3, d), lambda bb, i, k: (0, 0)),
            pl.BlockSpec((1, d), lambda bb, i, k: (0, 0)),
            pl.BlockSpec((d, tf), lambda bb, i, k: (0, k)),
            pl.BlockSpec((d, tf), lambda bb, i, k: (0, k)),
            pl.BlockSpec((tf, d), lambda bb, i, k: (k, 0)),
        ],
        out_specs=pl.BlockSpec((1, tm, d), lambda bb, i, k: (bb, i, 0)),
        out_shape=jax.ShapeDtypeStruct((b, t, d), F32),
        scratch_shapes=[pltpu.VMEM((tm, d), BF16), pltpu.VMEM((tm, d), F32)],
        compiler_params=_cparams(("parallel", "parallel", "arbitrary")),
        name="ffn",
    )(xc, modl, modc, g, w1, w3, w2)


def _router_kernel(x_ref, modl_ref, modc_ref, g_ref, rw_ref, rb_ref, h_ref, ids_ref, p_ref, *, tm, n_ctx):
    i = pl.program_id(1)
    lane = lax.broadcasted_iota(jnp.int32, (tm, 128), 1)
    h = _modulated_norm(x_ref[0], g_ref[...], modl_ref[0, 0:1, :], modl_ref[0, 1:2, :],
                        modc_ref[0:1, :], modc_ref[1:2, :], i * tm, n_ctx)
    h_ref[0] = h
    h_hi = h.astype(BF16)
    h_lo = (h - h_hi.astype(F32)).astype(BF16)
    rw = rw_ref[...]
    rw_hi = rw.astype(BF16)
    rw_lo = (rw - rw_hi.astype(F32)).astype(BF16)
    logits = _dot(h_hi, rw_hi) + _dot(h_hi, rw_lo) + _dot(h_lo, rw_hi) + rb_ref[...]
    logits = jnp.where(lane < N_EXPERTS, logits, NEG_INF)
    v0 = jnp.max(logits, axis=-1, keepdims=True)
    i0 = jnp.min(jnp.where(logits == v0, lane, 128), axis=-1, keepdims=True)
    rest = jnp.where(lane == i0, NEG_INF, logits)
    v1 = jnp.max(rest, axis=-1, keepdims=True)
    i1 = jnp.min(jnp.where(rest == v1, lane, 128), axis=-1, keepdims=True)
    e1 = jnp.exp(v1 - v0)
    p0 = 1.0 / (1.0 + e1)
    ids_ref[0] = jnp.where(lane == 0, i0, jnp.where(lane == 1, i1, 0))
    p_ref[0] = jnp.where(lane == 0, p0, jnp.where(lane == 1, e1 * p0, 0.0))


def _router(xc, modl, modc, g, rw_pad, rb_pad, n_ctx):
    b, t, d = xc.shape
    tm = _pick_tile(t, (768, 384, 256, 128))
    tok = lambda w: pl.BlockSpec((1, tm, w), lambda bb, i: (bb, i, 0))
    return pl.pallas_call(
        functools.partial(_router_kernel, tm=tm, n_ctx=n_ctx),
        grid=(b, t // tm),
        in_specs=[
            tok(d),
            pl.BlockSpec((1, 3, d), lambda bb, i: (bb, 0, 0)),
            pl.BlockSpec((3, d), lambda bb, i: (0, 0)),
            pl.BlockSpec((1, d), lambda bb, i: (0, 0)),
            pl.BlockSpec((d, 128), lambda bb, i: (0, 0)),
            pl.BlockSpec((1, 128), lambda bb, i: (0, 0)),
        ],
        out_specs=[tok(d), tok(128), tok(128)],
        out_shape=[jax.ShapeDtypeStruct((b, t, d), F32), jax.ShapeDtypeStruct((b, t, 128), jnp.int32),
                   jax.ShapeDtypeStruct((b, t, 128), F32)],
        compiler_params=_cparams(("parallel", "parallel")),
        name="router",
    )(xc, modl, modc, g, rw_pad, rb_pad)


def _routing_plan(ids, tm_e):
    n_tok = ids.shape[0]
    n_asg = 2 * n_tok
    n_tiles = n_asg // tm_e + N_EXPERTS
    e_flat = jnp.concatenate([ids[:, 0], ids[:, 1]])
    onehot = (e_flat[:, None] == jnp.arange(N_EXPERTS, dtype=jnp.int32)[None, :]).astype(jnp.int32)
    csum = jnp.cumsum(onehot, axis=0)
    rank = jnp.sum(csum * onehot, axis=1) - 1
    counts = csum[-1]
    padded = ((counts + tm_e - 1) // tm_e) * tm_e
    ends = jnp.cumsum(padded)
    starts = ends - padded
    dest = jnp.sum(onehot * starts[None, :], axis=1) + rank
    row = jnp.arange(n_tiles * tm_e, dtype=jnp.int32)
    spare = n_asg + ((row // tm_e) % 2) * tm_e + row % tm_e
    asg = spare.at[dest].set(jnp.arange(n_asg, dtype=jnp.int32))
    tile_start = jnp.arange(n_tiles, dtype=jnp.int32) * tm_e
    n_used = ends[-1] // tm_e
    tile_e = jnp.minimum(jnp.sum((tile_start[:, None] >= ends[None, :]).astype(jnp.int32), axis=1), N_EXPERTS - 1)
    n_valid = jnp.clip(counts[tile_e] - (tile_start - starts[tile_e]), 0, tm_e)
    n_valid = jnp.where(tile_start < ends[-1], n_valid, 0).astype(jnp.int32)
    last_e = tile_e[jnp.maximum(n_used - 1, 0)]
    tile_e = jnp.where(tile_start < ends[-1], tile_e, last_e).astype(jnp.int32)
    src = jnp.where(asg >= n_asg, 0, jnp.where(asg >= n_tok, asg - n_tok, asg))
    return tile_e, n_valid, asg.reshape(n_tiles, 1, tm_e), src.reshape(n_tiles, 1, tm_e)


SUBLANES = 8


def _expert_kernel(te_ref, nv_ref, asg_ref, src_ref, src_next_ref, h_hbm, w1_ref, w3_ref, w2_ref, y_hbm,
                   hbuf, hbf, acc, obuf, gsem, ssem, *, n_k, tm_e):
    i = pl.program_id(0)
    k = pl.program_id(1)
    n_tiles = pl.num_programs(0)
    used = nv_ref[i] > 0
    nxt = jnp.minimum(i + 1, n_tiles - 1)
    next_used = jnp.logical_and(i + 1 < n_tiles, nv_ref[nxt] > 0)
    slot = i % 2

    def issue_gather(idx_ref, s):
        def body(g, c):
            base = pl.multiple_of(g * SUBLANES, SUBLANES)
            for jj in range(SUBLANES):
                src = idx_ref[0, 0, base + jj]
                pltpu.make_async_copy(h_hbm.at[pl.ds(src, 1), :], hbuf.at[s, pl.ds(base + jj, 1), :],
                                      gsem.at[s]).start()
            return c
        lax.fori_loop(0, tm_e // SUBLANES, body, 0)

    def wait_gather(s):
        pltpu.make_async_copy(h_hbm.at[pl.ds(0, tm_e), :], hbuf.at[s], gsem.at[s]).wait()

    def issue_scatter():
        def body(g, c):
            base = pl.multiple_of(g * SUBLANES, SUBLANES)
            for jj in range(SUBLANES):
                a = asg_ref[0, 0, base + jj]
                pltpu.make_async_copy(obuf.at[pl.ds(base + jj, 1), :], y_hbm.at[pl.ds(a, 1), :], ssem).start()
            return c
        lax.fori_loop(0, tm_e // SUBLANES, body, 0)

    def wait_scatter():
        pltpu.make_async_copy(obuf, y_hbm.at[pl.ds(0, tm_e), :], ssem).wait()

    @pl.when(jnp.logical_and(used, k == 0))
    def _():
        @pl.when(i == 0)
        def _():
            obuf[...] = jnp.zeros_like(obuf)
            n_rows = y_hbm.shape[0]
            for base in (n_rows - 2 * tm_e, n_rows - tm_e):
                pltpu.make_async_copy(obuf, y_hbm.at[pl.ds(base, tm_e), :], ssem).start()
            wait_scatter()
            wait_scatter()
            issue_gather(src_ref, 0)
        wait_gather(slot)
        hbf[...] = hbuf[slot].astype(BF16)

        @pl.when(next_used)
        def _():
            issue_gather(src_next_ref, 1 - slot)

    @pl.when(used)
    def _():
        part = _swiglu_chunks(hbf[...], w1_ref, w3_ref, w2_ref, (0,))

        @pl.when(k == 0)
        def _():
            acc[...] = part

        @pl.when(jnp.logical_and(k > 0, k < n_k - 1))
        def _():
            acc[...] += part

        @pl.when(k == n_k - 1)
        def _():
            @pl.when(i > 0)
            def _():
                wait_scatter()
            obuf[...] = acc[...] + part if n_k > 1 else part
            issue_scatter()

            @pl.when(jnp.logical_not(next_used))
            def _():
                wait_scatter()


def _experts(h_flat, tile_e, n_valid, asg, src, w1, w3, w2, tm_e):
    n_tok, d = h_flat.shape
    n_e, _, dff = w1.shape
    n_tiles = asg.shape[0]
    tf = _pick_tile(dff, (896, 256, 128))
    n_k = dff // tf
    kk = lambda i, k, nv: jnp.where(nv[i] > 0, k, n_k - 1)
    grid_spec = pltpu.PrefetchScalarGridSpec(
        num_scalar_prefetch=2,
        grid=(n_tiles, n_k),
        in_specs=[
            pl.BlockSpec((1, 1, tm_e), lambda i, k, te, nv: (i, 0, 0), memory_space=pltpu.SMEM),
            pl.BlockSpec((1, 1, tm_e), lambda i, k, te, nv: (i, 0, 0), memory_space=pltpu.SMEM),
            pl.BlockSpec((1, 1, tm_e), lambda i, k, te, nv: (jnp.minimum(i + 1, n_tiles - 1), 0, 0),
                         memory_space=pltpu.SMEM),
            pl.BlockSpec(memory_space=pl.ANY),
            pl.BlockSpec((1, d, tf), lambda i, k, te, nv: (te[i], 0, kk(i, k, nv))),
            pl.BlockSpec((1, d, tf), lambda i, k, te, nv: (te[i], 0, kk(i, k, nv))),
            pl.BlockSpec((1, tf, d), lambda i, k, te, nv: (te[i], kk(i, k, nv), 0)),
        ],
        out_specs=pl.BlockSpec(memory_space=pl.ANY),
        scratch_shapes=[pltpu.VMEM((2, tm_e, d), F32), pltpu.VMEM((tm_e, d), BF16), pltpu.VMEM((tm_e, d), F32),
                        pltpu.VMEM((tm_e, d), F32), pltpu.SemaphoreType.DMA((2,)), pltpu.SemaphoreType.DMA(())],
    )
    return pl.pallas_call(
        functools.partial(_expert_kernel, n_k=n_k, tm_e=tm_e),
        grid_spec=grid_spec,
        out_shape=jax.ShapeDtypeStruct((2 * n_tok + 2 * tm_e, d), F32),
        compiler_params=_cparams(("arbitrary", "arbitrary")),
        name="experts",
    )(tile_e, n_valid, asg, src, src, h_flat, w1, w3, w2)


def _combine_kernel(x_ref, y0_ref, y1_ref, p_ref, gl_ref, gc_ref, o_ref, *, tm, n_ctx, first):
    i = pl.program_id(1)
    p = p_ref[0]
    lane = lax.broadcasted_iota(jnp.int32, p.shape, 1)
    p0 = jnp.sum(jnp.where(lane == 0, p, 0.0), axis=-1, keepdims=True)
    p1 = jnp.sum(jnp.where(lane == 1, p, 0.0), axis=-1, keepdims=True)
    row = (i + first) * tm + lax.broadcasted_iota(jnp.int32, (tm, 1), 0)
    g2 = jnp.where(row < n_ctx, gc_ref[...], gl_ref[0])
    o_ref[0] = x_ref[0] + g2 * (p0 * y0_ref[...] + p1 * y1_ref[...])


def _combine(xc, y, p, g2_l, g2_c, n_ctx, latent_only):
    b, t, d = xc.shape
    tm = _pick_tile(n_ctx, (256, 128)) if latent_only else _pick_tile(t, (768, 384, 256, 128))
    first = n_ctx // tm if latent_only else 0
    tpb = t // tm
    tok = lambda w: pl.BlockSpec((1, tm, w), lambda bb, i: (bb, i + first, 0))
    ysp = lambda kk: pl.BlockSpec((tm, d), lambda bb, i: ((kk * b + bb) * tpb + i + first, 0))
    return pl.pallas_call(
        functools.partial(_combine_kernel, tm=tm, n_ctx=n_ctx, first=first),
        grid=(b, tpb - first),
        in_specs=[tok(d), ysp(0), ysp(1), tok(128),
                  pl.BlockSpec((1, 1, d), lambda bb, i: (bb, 0, 0)),
                  pl.BlockSpec((1, d), lambda bb, i: (0, 0))],
        out_specs=pl.BlockSpec((1, tm, d), lambda bb, i: (bb, i, 0)),
        out_shape=jax.ShapeDtypeStruct((b, t - first * tm, d), F32),
        compiler_params=_cparams(("parallel", "parallel")),
        name="moe_combine",
    )(xc, y, y, p, g2_l, g2_c)


MOE_TILE = 512


def _moe(xc, modl, modc, g, rw_pad, rb_pad, w1, w3, w2, n_ctx, latent_only):
    b, t, d = xc.shape
    h, ids, p = _router(xc, modl, modc, g, rw_pad, rb_pad, n_ctx)
    tile_e, n_valid, asg, src = _routing_plan(ids.reshape(b * t, 128)[:, :2], MOE_TILE)
    y = _experts(h.reshape(b * t, d), tile_e, n_valid, asg, src, w1, w3, w2, MOE_TILE)
    return _combine(xc, y, p, modl[:, 2:3], modc[2:3], n_ctx, latent_only)


def _rope_tables(n_ctx, s):
    def widen(cos, sin):
        c64 = jnp.concatenate([cos, cos], axis=-1)
        s64 = jnp.concatenate([-sin, sin], axis=-1)
        return jnp.tile(c64, (1, 2)), jnp.tile(s64, (1, 2))

    n_rows = s // GRID_W
    row = jnp.repeat(jnp.arange(n_rows, dtype=F32), GRID_W)
    col = jnp.tile(jnp.arange(GRID_W, dtype=F32), n_rows)
    n_freq = HEAD_DIM // 4
    inv = ROPE_BASE ** (-jnp.arange(n_freq, dtype=F32) / n_freq)
    ang = jnp.concatenate([row[:, None] * inv, col[:, None] * inv], axis=-1)
    cos_ax = jnp.concatenate([jnp.ones((n_ctx, HEAD_DIM // 2), F32), jnp.cos(ang)], axis=0)
    sin_ax = jnp.concatenate([jnp.zeros((n_ctx, HEAD_DIM // 2), F32), jnp.sin(ang)], axis=0)
    n_freq = HEAD_DIM // 2
    inv = ROPE_BASE ** (-jnp.arange(n_freq, dtype=F32) / n_freq)
    pos = jnp.arange(n_ctx + s, dtype=F32)
    ang = pos[:, None] * inv
    return widen(cos_ax, sin_ax) + widen(jnp.cos(ang), jnp.sin(ang))


def _block_diag_ones(n):
    g = jnp.arange(n) // HEAD_DIM
    return (g[:, None] == g[None, :]).astype(BF16)


def kernel(x, c, ctx, c_ctx, w_mod, b_mod, g_norm1, g_norm2, w_in, mlp_g_v, mlp_w_s, mlp_b_s, win_q_norm,
           win_k_norm, win_sink, glb_q_norm, glb_k_norm, ret_decay, ret_gn, w_branch, w_out, ffn_w1, ffn_w3,
           ffn_w2, router_w, router_b, moe_w1, moe_w3, moe_w2):
    batch, s, d = x.shape
    n_ctx = ctx.shape[1]
    depth = w_in.shape[0]
    assert batch + 1 <= 8 and n_ctx % BLOCK == 0 and s % BLOCK == 0

    xc = jnp.concatenate([ctx, x], axis=1)
    cvec = jnp.zeros((8, d), F32).at[:batch].set(c).at[batch].set(c_ctx)
    mod = _modulation(cvec, w_mod, b_mod)
    mod = mod.reshape(depth, 8, 6, d)

    tabs = _rope_tables(n_ctx, s)
    bd256 = _block_diag_ones(256)
    ones = lambda n: jnp.ones((n,), F32)
    zeros = lambda n: jnp.zeros((n,), F32)
    scale = HEAD_DIM ** -0.5

    for layer in range(depth):
        ml = mod[layer]
        modl1 = ml[:batch, 0:2]
        modc1 = ml[batch, 0:2]
        g1_l = ml[:batch, 2:3]
        g1_c = ml[batch, 2:3]
        modl2 = ml[:batch, 3:6]
        modc2 = ml[batch, 3:6]

        def qk_gain(gq, gk, q_scale):
            return jnp.concatenate([jnp.tile(gq, 4) * q_scale, jnp.tile(gk, 2)])

        gain = jnp.stack([
            jnp.concatenate([mlp_g_v[layer].reshape(-1), ones(128)]),
            qk_gain(win_q_norm[layer], win_k_norm[layer], scale),
            qk_gain(glb_q_norm[layer], glb_k_norm[layer], scale * LOG2_E)])
        post = jnp.concatenate([ones(256), jnp.full((256,), scale, F32)])[None, :]

        p0, kv, gates = _inproj(xc, modl1, modc1, g_norm1[layer][None, :], w_in[layer].astype(BF16),
                                (gain, post, bd256), tabs, n_ctx)
        b_att = _window_attention(p0, kv, win_sink[layer], n_ctx)
        c_att = _global_attention(p0, kv, n_ctx)
        dec = jnp.broadcast_to(ret_decay[layer].reshape(-1, 1), (8, 128))
        o_f, o_b = _retention(p0, dec, n_ctx)
        bias = jnp.repeat(mlp_b_s[layer].T, HEAD_DIM, axis=1)
        xc = _merge(xc, p0, b_att, c_att, o_f, o_b, gates, g1_l, g1_c, mlp_w_s[layer].astype(BF16), bias,
                    ret_gn[layer].reshape(1, -1), bd256, w_branch[layer].astype(BF16),
                    w_out[layer].astype(BF16), n_ctx)
        j = layer // 2
        g2 = g_norm2[layer][None, :]
        if layer % 2 == 0:
            xc = _ffn(xc, modl2, modc2, g2, ffn_w1[j].astype(BF16), ffn_w3[j].astype(BF16),
                      ffn_w2[j].astype(BF16), n_ctx)
        else:
            rw_pad = jnp.zeros((d, 128), F32).at[:, :N_EXPERTS].set(router_w[j])
            rb_pad = jnp.zeros((1, 128), F32).at[0, :N_EXPERTS].set(router_b[j])
            xc = _moe(xc, modl2, modc2, g2, rw_pad, rb_pad, moe_w1[j].astype(BF16), moe_w3[j].astype(BF16),
                      moe_w2[j].astype(BF16), n_ctx, latent_only=(layer == depth - 1))
    return xc if depth % 2 == 0 else xc[:, n_ctx:]
```

```python
import functools

import jax
import jax.numpy as jnp
from jax import lax
from jax.experimental import pallas as pl
from jax.experimental.pallas import tpu as pltpu

F32 = jnp.float32
BF16 = jnp.bfloat16

HEAD_DIM = 64
BLOCK = 128
GRID_W = 64
ROPE_BASE = 10000.0
N_BRANCH = 4
BRANCH_W = 256
N_EXPERTS = 8
EPS = 1e-6
NEG_INF = -1e30
LOG2_E = 1.4426950408889634
IN_TILE = 512
N_MIX_TILES = 5
VMEM_LIMIT = 56 * 1024 * 1024


def _cparams(sem):
    return pltpu.CompilerParams(dimension_semantics=sem, vmem_limit_bytes=VMEM_LIMIT)


def _dot(a, b):
    return jnp.dot(a, b, preferred_element_type=F32)


def _dot_nt(a, b):
    return lax.dot_general(a, b, (((1,), (1,)), ((), ())), preferred_element_type=F32)


def _split_dot(a, b_bf16):
    hi = a.astype(BF16)
    lo = (a - hi.astype(F32)).astype(BF16)
    return _dot(hi, b_bf16) + _dot(lo, b_bf16)


def _pick_tile(n, candidates):
    for c in candidates:
        if n % c == 0:
            return c
    raise ValueError(f"no tile for {n} in {candidates}")


def _modulated_norm(x, g, sh_l, sc_l, sh_c, sc_c, row0, n_ctx):
    tm = x.shape[0]
    ms = jnp.mean(x * x, axis=-1, keepdims=True)
    y = x * lax.rsqrt(ms + EPS) * g
    row = row0 + lax.broadcasted_iota(jnp.int32, (tm, 1), 0)
    is_ctx = row < n_ctx
    sh = jnp.where(is_ctx, sh_c, sh_l)
    sc = jnp.where(is_ctx, sc_c, sc_l)
    return y * (1.0 + sc) + sh


def _mod_kernel(c_ref, w_ref, b_ref, o_ref):
    c = c_ref[...]
    s = c * jax.nn.sigmoid(c)
    s_hi = s.astype(BF16)
    s_lo = (s - s_hi.astype(F32)).astype(BF16)
    w = w_ref[0]
    w_hi = w.astype(BF16)
    w_lo = (w - w_hi.astype(F32)).astype(BF16)
    o_ref[0] = _dot(s_hi, w_hi) + _dot(s_hi, w_lo) + _dot(s_lo, w_hi) + b_ref[0]


def _modulation(cvec, w_mod, b_mod):
    depth, d, n = w_mod.shape
    tn = _pick_tile(n, (1536, 1024, 512, 128))
    return pl.pallas_call(
        _mod_kernel,
        grid=(depth, n // tn),
        in_specs=[
            pl.BlockSpec((8, d), lambda l, j: (0, 0)),
            pl.BlockSpec((1, d, tn), lambda l, j: (l, 0, j)),
            pl.BlockSpec((1, 1, tn), lambda l, j: (l, 0, j)),
        ],
        out_specs=pl.BlockSpec((1, 8, tn), lambda l, j: (l, 0, j)),
        out_shape=jax.ShapeDtypeStruct((depth, 8, n), F32),
        compiler_params=_cparams(("parallel", "parallel")),
        name="modulation",
    )(cvec, w_mod, b_mod.reshape(depth, 1, n))


def _group_rms(v, bd, gain):
    sq = (v * v).astype(BF16)
    half = bd.shape[0]
    parts = []
    for c in range(0, v.shape[-1], half):
        wd = min(half, v.shape[-1] - c)
        parts.append(_dot(sq[:, c:c + wd], bd[:wd, :wd]))
    gs = parts[0] if len(parts) == 1 else jnp.concatenate(parts, axis=1)
    return v * lax.rsqrt(gs * (1.0 / HEAD_DIM) + EPS) * gain


def _rope(v, cos, sin):
    w = v.shape[-1]
    lane = lax.broadcasted_iota(jnp.int32, v.shape, 1)
    first = (lane & 32) == 0
    partner = jnp.where(first, pltpu.roll(v, w - 32, 1), pltpu.roll(v, 32, 1))
    return v * cos + partner * sin


def _tile(t, n):
    return jnp.concatenate([t] * n, axis=1)


def _inproj_kernel(x_ref, modl_ref, modc_ref, g_ref, w_ref, gain_ref, post_ref, bd_ref,
                   cax_ref, sax_ref, csq_ref, ssq_ref, p_ref, kv_ref, gate_ref, *, tm, n_ctx, n_tiles):
    i = pl.program_id(1)
    h = _modulated_norm(x_ref[0], g_ref[...], modl_ref[0, 0:1, :], modl_ref[0, 1:2, :],
                        modc_ref[0:1, :], modc_ref[1:2, :], i * tm, n_ctx).astype(BF16)
    lane = lax.broadcasted_iota(jnp.int32, (tm, 128), 1)
    lo = lane < HEAD_DIM

    def attn_tile(acc, j, duplicate):
        c0 = j * IN_TILE
        qk = _group_rms(acc[:, :384], bd_ref[...], gain_ref[j:j + 1, :])
        qk = _rope(qk, _tile(cax_ref[...], 3), _tile(sax_ref[...], 3))
        vv = acc[:, 384:512]
        p_ref[0, :, c0:c0 + 384] = qk.astype(BF16)
        p_ref[0, :, c0 + 384:c0 + 512] = vv.astype(BF16)
        k = qk[:, 256:384]
        kr = pltpu.roll(k, HEAD_DIM, 1)
        vr = pltpu.roll(vv, HEAD_DIM, 1)
        if duplicate:
            parts = [jnp.where(lo, k, kr), jnp.where(lo, kr, k), jnp.where(lo, vv, vr), jnp.where(lo, vr, vv)]
        else:
            pad = jnp.where(lane == HEAD_DIM, 1.0, 0.0)
            parts = [jnp.where(lo, k, pad), jnp.where(lo, kr, pad), jnp.where(lo, vv, pad), jnp.where(lo, vr, pad)]
        for n, part in enumerate(parts):
            c = (j - 1) * IN_TILE + n * 128
            kv_ref[0, :, c:c + 128] = part.astype(BF16)

    for j in range(n_tiles):
        c0 = j * IN_TILE
        acc = _dot(h, w_ref[:, c0:c0 + IN_TILE])
        if j == 0:
            v = jax.nn.gelu(acc)
            p_ref[0, :, 0:256] = v[:, :256].astype(BF16)
            p_ref[0, :, 256:512] = _group_rms(v[:, 256:], bd_ref[...], gain_ref[0:1, :256]).astype(BF16)
        elif j == 1:
            attn_tile(acc, 1, True)
        elif j == 2:
            attn_tile(acc, 2, False)
        elif j == 3:
            v = _rope(acc, _tile(csq_ref[...], 4), _tile(ssq_ref[...], 4)) * post_ref[...]
            p_ref[0, :, c0:c0 + IN_TILE] = v.astype(BF16)
        elif j < N_MIX_TILES:
            p_ref[0, :, c0:c0 + IN_TILE] = acc.astype(BF16)
        else:
            g0 = c0 - N_MIX_TILES * IN_TILE
            gate_ref[0, :, g0:g0 + IN_TILE] = acc.astype(BF16)


def _inproj(xc, modl, modc, g, w_bf, ep, tabs, n_ctx):
    b, t, d = xc.shape
    n_in = w_bf.shape[1]
    tm = _pick_tile(t, (384, 256, 128))
    gain, post, bd = ep
    cax, sax, csq, ssq = tabs
    n_gate = n_in - N_MIX_TILES * IN_TILE
    tok = lambda bb, i: (i, 0)
    full2 = lambda bb, i: (0, 0)
    out = lambda w: pl.BlockSpec((1, tm, w), lambda bb, i: (bb, i, 0))
    return pl.pallas_call(
        functools.partial(_inproj_kernel, tm=tm, n_ctx=n_ctx, n_tiles=n_in // IN_TILE),
        grid=(b, t // tm),
        in_specs=[
            pl.BlockSpec((1, tm, d), lambda bb, i: (bb, i, 0)),
            pl.BlockSpec((1, 2, d), lambda bb, i: (bb, 0, 0)),
            pl.BlockSpec((2, d), full2),
            pl.BlockSpec((1, d), full2),
            pl.BlockSpec((d, n_in), full2),
            pl.BlockSpec((3, 384), full2),
            pl.BlockSpec((1, IN_TILE), full2),
            pl.BlockSpec((256, 256), full2),
            pl.BlockSpec((tm, 128), tok),
            pl.BlockSpec((tm, 128), tok),
            pl.BlockSpec((tm, 128), tok),
            pl.BlockSpec((tm, 128), tok),
        ],
        out_specs=[out(N_MIX_TILES * IN_TILE), out(2 * IN_TILE), out(n_gate)],
        out_shape=[
            jax.ShapeDtypeStruct((b, t, N_MIX_TILES * IN_TILE), BF16),
            jax.ShapeDtypeStruct((b, t, 2 * IN_TILE), BF16),
            jax.ShapeDtypeStruct((b, t, n_gate), BF16),
        ],
        compiler_params=_cparams(("parallel", "parallel")),
        name="inproj",
    )(xc, modl, modc, g, w_bf, gain, post, bd, cax, sax, csq, ssq)


def _stack_heads(qa):
    qf = qa.astype(F32)
    lo = lax.broadcasted_iota(jnp.int32, qf.shape, 1) < HEAD_DIM
    return jnp.concatenate([jnp.where(lo, qf, 0.0), jnp.where(lo, 0.0, qf)], axis=0).astype(BF16)


def _unstack_heads(o2):
    n = o2.shape[0] // 2
    lo = lax.broadcasted_iota(jnp.int32, (n, o2.shape[1]), 1) < HEAD_DIM
    return jnp.where(lo, o2[:n], o2[n:])


def _win_kernel(sink_ref, q_ref, kc_ref, vc_ref, kp_ref, kx_ref, kn_ref, vp_ref, vx_ref, vn_ref, o_ref,
                *, n_ctx_blocks, n_blocks, per_step):
    j = pl.program_id(1)
    lc = n_ctx_blocks * BLOCK
    w = lc + 3 * BLOCK
    row = lax.broadcasted_iota(jnp.int32, (BLOCK, w), 0)
    col = lax.broadcasted_iota(jnp.int32, (BLOCK, w), 1)
    c = col - lc
    rid = lax.broadcasted_iota(jnp.int32, (2 * BLOCK, 1), 0)
    blocks = ([(kp_ref, vp_ref, slice(None))]
              + [(kx_ref, vx_ref, slice(u * BLOCK, (u + 1) * BLOCK)) for u in range(per_step)]
              + [(kn_ref, vn_ref, slice(None))])
    for sub in range(per_step):
        n = per_step * j + sub
        rows = slice(sub * BLOCK, (sub + 1) * BLOCK)
        is_lat = n >= n_ctx_blocks
        prev_ok = n > n_ctx_blocks
        next_ok = n < n_blocks - 1
        lo_b = jnp.where(prev_ok, row, BLOCK)
        hi_b = jnp.where(next_ok, row + 2 * BLOCK, 2 * BLOCK - 1)
        lat_bias = jnp.where(is_lat, 0.0, NEG_INF)
        bias = jnp.where(col < lc, 0.0, jnp.where(c >= lo_b, jnp.where(c <= hi_b, lat_bias, NEG_INF), NEG_INF))
        bias2 = jnp.concatenate([bias, bias], axis=0)
        outs = []
        for a in range(2):
            sl = slice(128 * a, 128 * (a + 1))
            q2 = _stack_heads(q_ref[0, rows, sl])
            local = blocks[sub:sub + 3]
            keys = jnp.concatenate([kc_ref[0, :, sl]] + [kr[0, rs, sl] for kr, _, rs in local], axis=0)
            vals = jnp.concatenate([vc_ref[0, :, sl]] + [vr[0, rs, sl] for _, vr, rs in local], axis=0)
            s = _dot_nt(q2, keys) + bias2
            sk = jnp.where(rid < BLOCK, sink_ref[2 * a], sink_ref[2 * a + 1])
            m = jnp.maximum(jnp.max(s, axis=-1, keepdims=True), sk)
            e = jnp.exp(s - m)
            den = jnp.sum(e, axis=-1, keepdims=True) + jnp.exp(sk - m)
            p = (e / den).astype(BF16)
            outs.append(_unstack_heads(_dot(p, vals)))
        o_ref[0, rows, :] = jnp.concatenate(outs, axis=1).astype(BF16)


def _window_attention(p0, kv, sink, n_ctx):
    b, t, _ = p0.shape
    nb = t // BLOCK
    ncb = n_ctx // BLOCK
    ps = _pick_tile(nb, (6, 5, 4, 3, 2, 1))
    span = lambda cidx: pl.BlockSpec((1, ps * BLOCK, 256), lambda bb, j: (bb, j, cidx))
    prv = lambda cidx: pl.BlockSpec((1, BLOCK, 256), lambda bb, j: (bb, jnp.maximum(ps * j - 1, 0), cidx))
    nxt = lambda cidx: pl.BlockSpec((1, BLOCK, 256), lambda bb, j: (bb, jnp.minimum(ps * j + ps, nb - 1), cidx))
    ctx = lambda cidx: pl.BlockSpec((1, n_ctx, 256), lambda bb, j: (bb, 0, cidx))
    return pl.pallas_call(
        functools.partial(_win_kernel, n_ctx_blocks=ncb, n_blocks=nb, per_step=ps),
        grid=(b, nb // ps),
        in_specs=[
            pl.BlockSpec(memory_space=pltpu.SMEM),
            span(2),
            ctx(0), ctx(1),
            prv(0), span(0), nxt(0),
            prv(1), span(1), nxt(1),
        ],
        out_specs=pl.BlockSpec((1, ps * BLOCK, 256), lambda bb, j: (bb, j, 0)),
        out_shape=jax.ShapeDtypeStruct((b, t, 256), BF16),
        compiler_params=_cparams(("parallel", "parallel")),
        name="window_attn",
    )(sink, p0, kv, kv, kv, kv, kv, kv, kv, kv)


SAFE_LOG2_BOUND = 60.0


def _glb_kernel(q_ref, k_ref, v_ref, o_ref, kmax_scr, *, tq, tk, n_ctx, t_all):
    i = pl.program_id(1)
    sls = (slice(0, 128), slice(128, 256))
    lane = lax.broadcasted_iota(jnp.int32, (2 * tq, 128), 1)

    @pl.when(i == 0)
    def _():
        klane = lax.broadcasted_iota(jnp.int32, (tk, 128), 1)
        for a in range(2):
            def kbody(ci, mx):
                start = pl.multiple_of(ci * tk, tk)
                k = k_ref[0, pl.ds(start, tk), sls[a]].astype(F32)
                sq = jnp.sum(jnp.where(klane < HEAD_DIM, k * k, 0.0), axis=-1, keepdims=True)
                return jnp.maximum(mx, jnp.max(sq, axis=0, keepdims=True))
            mx = lax.fori_loop(0, t_all // tk, kbody, jnp.zeros((1, 1), F32))
            kmax_scr[a:a + 1, :] = jnp.broadcast_to(jnp.sqrt(mx), (1, 128))

    q2s, bnds = [], []
    for a in range(2):
        q = q_ref[0, :, sls[a]].astype(F32)
        q2 = jnp.concatenate([q, pltpu.roll(q, HEAD_DIM, 1)], axis=0)
        q2 = jnp.where(lane < HEAD_DIM, q2, 0.0)
        nrm = jnp.sqrt(jnp.sum(q2 * q2, axis=-1, keepdims=True))
        bnd = nrm * kmax_scr[a:a + 1, 0:1] * 1.01
        q2s.append(q2)
        bnds.append(bnd)
    worst = jnp.max(jnp.maximum(bnds[0], bnds[1]))

    def finish(res):
        outs = []
        for acc in res:
            den = jnp.sum(jnp.where(lane == HEAD_DIM, acc, 0.0), axis=-1, keepdims=True)
            o = acc / den
            lo_t = lax.broadcasted_iota(jnp.int32, (tq, 128), 1) < HEAD_DIM
            outs.append(jnp.where(lo_t, o[:tq], pltpu.roll(o, HEAD_DIM, 1)[tq:]))
        o_ref[0] = jnp.concatenate(outs, axis=1).astype(BF16)

    def attend_fixed(n_chunks, size):
        qb = [jnp.where(lane == HEAD_DIM, -bnds[a], q2s[a]).astype(BF16) for a in range(2)]

        def body(ci, carry):
            start = pl.multiple_of(ci * size, size)
            out = []
            for a in range(2):
                k = k_ref[0, pl.ds(start, size), sls[a]]
                v = v_ref[0, pl.ds(start, size), sls[a]]
                p = jnp.exp2(_dot_nt(qb[a], k))
                out.append(carry[a] + _dot(p.astype(BF16), v))
            return tuple(out)

        zero = jnp.zeros((2 * tq, 128), F32)
        finish(lax.fori_loop(0, n_chunks, body, (zero, zero)))

    def attend_online(n_chunks, size):
        qb = [q2s[a].astype(BF16) for a in range(2)]

        def body(ci, carry):
            start = pl.multiple_of(ci * size, size)
            out = []
            for a in range(2):
                m, acc = carry[a]
                k = k_ref[0, pl.ds(start, size), sls[a]]
                v = v_ref[0, pl.ds(start, size), sls[a]]
                s = _dot_nt(qb[a], k)
                m_new = jnp.maximum(m, jnp.max(s, axis=-1, keepdims=True))
                p = jnp.exp2(s - m_new)
                acc = jnp.exp2(m - m_new) * acc + _dot(p.astype(BF16), v)
                out.append((m_new, acc))
            return tuple(out)

        init = (jnp.full((2 * tq, 1), NEG_INF, F32), jnp.zeros((2 * tq, 128), F32))
        res = lax.fori_loop(0, n_chunks, body, (init, init))
        finish([acc for (_, acc) in res])

    is_ctx = (i + 1) * tq <= n_ctx
    safe = worst <= SAFE_LOG2_BOUND

    @pl.when(jnp.logical_and(is_ctx, safe))
    def _():
        attend_fixed(1, n_ctx)

    @pl.when(jnp.logical_and(is_ctx, jnp.logical_not(safe)))
    def _():
        attend_online(1, n_ctx)

    @pl.when(jnp.logical_and(jnp.logical_not(is_ctx), safe))
    def _():
        attend_fixed(t_all // tk, tk)

    @pl.when(jnp.logical_and(jnp.logical_not(is_ctx), jnp.logical_not(safe)))
    def _():
        attend_online(t_all // tk, tk)


def _global_attention(p0, kv, n_ctx):
    b, t, _ = p0.shape
    tq = _pick_tile(n_ctx, (256, 128))
    tk = _pick_tile(t, (1408, 768, 384, 256, 128))
    return pl.pallas_call(
        functools.partial(_glb_kernel, tq=tq, tk=tk, n_ctx=n_ctx, t_all=t),
        grid=(b, t // tq),
        in_specs=[
            pl.BlockSpec((1, tq, 256), lambda bb, i: (bb, i, 4)),
            pl.BlockSpec((1, t, 256), lambda bb, i: (bb, 0, 2)),
            pl.BlockSpec((1, t, 256), lambda bb, i: (bb, 0, 3)),
        ],
        out_specs=pl.BlockSpec((1, tq, 256), lambda bb, i: (bb, i, 0)),
        out_shape=jax.ShapeDtypeStruct((b, t, 256), BF16),
        scratch_shapes=[pltpu.VMEM((8, 128), F32)],
        compiler_params=_cparams(("arbitrary", "arbitrary")),
        name="global_attn",
    )(p0, kv, kv)


_TAB_D, _TAB_Q, _TAB_K, _TAB_C, _TAB_ROWS = 0, 256, 384, 512, 640


def _ret_kernel(dec_ref, qf_ref, kf_ref, vf_ref, qb_ref, kb_ref, vb_ref, of_ref, ob_ref, st_scr, tab_scr,
                *, batch):
    i = pl.program_id(0)
    r = lax.broadcasted_iota(jnp.int32, (BLOCK, BLOCK), 0)
    c = lax.broadcasted_iota(jnp.int32, (BLOCK, BLOCK), 1)
    lane_lo = c < HEAD_DIM
    same_head = (r < HEAD_DIM) == lane_lo

    @pl.when(i == 0)
    def _():
        st_scr[...] = jnp.zeros_like(st_scr)
        x = dec_ref[...]
        lg = jnp.where(x >= 0.0, -jnp.log(1.0 + jnp.exp(-x)), x - jnp.log(1.0 + jnp.exp(x)))
        rf = r.astype(F32)
        cf = c.astype(F32)
        for d in range(2):
            for pair in range(2):
                le = lg[4 * d + 2 * pair:4 * d + 2 * pair + 1, :]
                lo_ = lg[4 * d + 2 * pair + 1:4 * d + 2 * pair + 2, :]
                lp = jnp.where(lane_lo[0:1, :], le, lo_)
                if d == 0:
                    rel = rf - cf
                    qpow = rf + 1.0
                    kpow = (BLOCK - 1.0) - rf
                else:
                    rel = cf - rf - 1.0
                    qpow = (BLOCK - 1.0) - rf
                    kpow = rf
                msk = rel >= 0.0
                relc = jnp.where(msk, rel, 0.0)
                tab_scr[d, pair, _TAB_D:_TAB_D + BLOCK, :] = jnp.where(msk, jnp.exp(le * relc), 0.0)
                tab_scr[d, pair, _TAB_D + BLOCK:_TAB_Q, :] = jnp.where(msk, jnp.exp(lo_ * relc), 0.0)
                tab_scr[d, pair, _TAB_Q:_TAB_K, :] = jnp.exp(lp * qpow)
                tab_scr[d, pair, _TAB_K:_TAB_C, :] = jnp.exp(lp * kpow)
                tab_scr[d, pair, _TAB_C:_TAB_ROWS, :] = jnp.where(
                    r < HEAD_DIM, jnp.exp(le * float(BLOCK)), jnp.exp(lo_ * float(BLOCK)))

    for d, (q_ref, k_ref, v_ref, o_ref) in enumerate(((qf_ref, kf_ref, vf_ref, of_ref),
                                                     (qb_ref, kb_ref, vb_ref, ob_ref))):
        for bb in range(batch):
            for pair in range(2):
                sl = slice(128 * pair, 128 * (pair + 1))
                q = q_ref[bb, :, sl].astype(F32)
                k = k_ref[bb, :, sl].astype(F32)
                v = v_ref[bb, :, sl]
                q2 = jnp.concatenate([jnp.where(lane_lo, q, 0.0), jnp.where(lane_lo, 0.0, q)], axis=0)
                qk = _dot_nt(q2.astype(BF16), k.astype(BF16)) * tab_scr[d, pair, _TAB_D:_TAB_Q, :]
                o_intra = _unstack_heads(_dot(qk.astype(BF16), v))
                s = st_scr[d, bb, pair]
                qd = q * tab_scr[d, pair, _TAB_Q:_TAB_K, :]
                o_ref[bb, :, sl] = o_intra + _dot(qd.astype(BF16), s.astype(BF16))
                kd = k * tab_scr[d, pair, _TAB_K:_TAB_C, :]
                upd = _dot(kd.T.astype(BF16), v)
                st_scr[d, bb, pair] = jnp.where(same_head, tab_scr[d, pair, _TAB_C:_TAB_ROWS, :] * s + upd, 0.0)


def _retention(p0, dec, n_ctx):
    b, t, _ = p0.shape
    nb = t // BLOCK
    ncb = n_ctx // BLOCK
    bwd = lambda i: jnp.where(i < ncb, ncb - 1 - i, nb - 1 + ncb - i)
    fspec = lambda cidx: pl.BlockSpec((b, BLOCK, 256), lambda i: (0, i, cidx))
    bspec = lambda cidx: pl.BlockSpec((b, BLOCK, 256), lambda i: (0, bwd(i), cidx))
    return pl.pallas_call(
        functools.partial(_ret_kernel, batch=b),
        grid=(nb,),
        in_specs=[
            pl.BlockSpec((8, 128), lambda i: (0, 0)),
            fspec(6), fspec(7), fspec(8),
            bspec(6), bspec(7), bspec(8),
        ],
        out_specs=[
            pl.BlockSpec((b, BLOCK, 256), lambda i: (0, i, 0)),
            pl.BlockSpec((b, BLOCK, 256), lambda i: (0, bwd(i), 0)),
        ],
        out_shape=[jax.ShapeDtypeStruct((b, t, 256), F32), jax.ShapeDtypeStruct((b, t, 256), F32)],
        scratch_shapes=[pltpu.VMEM((2, b, 2, BLOCK, BLOCK), F32),
                        pltpu.VMEM((2, 2, _TAB_ROWS, BLOCK), F32)],
        compiler_params=_cparams(("arbitrary",)),
        name="retention",
    )(dec, p0, p0, p0, p0, p0, p0)


def _merge_kernel(x_ref, uv_ref, batt_ref, catt_ref, of_ref, ob_ref, rg_ref, gate_ref, gl_ref, gc_ref,
                  ws_ref, bs_ref, gn_ref, bd_ref, wb_ref, wo_ref, o_ref, *, tm, n_ctx):
    i = pl.program_id(1)
    grp = jnp.right_shift(lax.broadcasted_iota(jnp.int32, (BLOCK, 256), 1), 6)
    a_chunks = []
    for ch in range(tm // BLOCK):
        rows = slice(ch * BLOCK, (ch + 1) * BLOCK)
        vn = uv_ref[0, rows, 256:512]
        mixed = bs_ref[...]
        for g in range(4):
            mixed = mixed + jnp.where(grp == g, _dot(ws_ref[g], vn), 0.0)
        a_chunks.append((uv_ref[0, rows, 0:256].astype(F32) * mixed).astype(BF16))
    a_br = jnp.concatenate(a_chunks, axis=0)
    o = of_ref[0] + ob_ref[0]
    mean = _split_dot(o, bd_ref[...]) * (1.0 / HEAD_DIM)
    oc = o - mean
    var = _split_dot(oc * oc, bd_ref[...]) * (1.0 / HEAD_DIM)
    rg = rg_ref[0].astype(F32)
    d_br = (oc * lax.rsqrt(var + EPS) * gn_ref[...] * (rg * jax.nn.sigmoid(rg))).astype(BF16)
    branches = (a_br, batt_ref[0], catt_ref[0], d_br)
    d_model = x_ref.shape[-1]
    acc = None
    for n in range(N_BRANCH):
        gate = 0.5 + 0.5 * jnp.tanh(0.5 * gate_ref[0, :, n * d_model:(n + 1) * d_model].astype(F32))
        term = gate * _dot(branches[n], wb_ref[n])
        acc = term if acc is None else acc + term
    m = _dot(acc.astype(BF16), wo_ref[...])
    row = i * tm + lax.broadcasted_iota(jnp.int32, (tm, 1), 0)
    g1 = jnp.where(row < n_ctx, gc_ref[...], gl_ref[0])
    o_ref[0] = x_ref[0] + g1 * m


def _merge(xc, p0, b_att, c_att, o_f, o_b, gates, g1_l, g1_c, ws_bf, bias, gn, bd, wb_bf, wo_bf, n_ctx):
    b, t, d = xc.shape
    tm = _pick_tile(t, (384, 256, 128))
    tok = lambda w, cidx: pl.BlockSpec((1, tm, w), lambda bb, i: (bb, i, cidx))
    full = lambda shape: pl.BlockSpec(shape, lambda bb, i: (0,) * len(shape))
    return pl.pallas_call(
        functools.partial(_merge_kernel, tm=tm, n_ctx=n_ctx),
        grid=(b, t // tm),
        in_specs=[
            tok(d, 0),
            tok(512, 0),
            tok(256, 0), tok(256, 0),
            tok(256, 0), tok(256, 0),
            tok(256, 9),
            tok(N_BRANCH * d, 0),
            pl.BlockSpec((1, 1, d), lambda bb, i: (bb, 0, 0)),
            full((1, d)),
            full((4, BLOCK, BLOCK)), full((BLOCK, 256)), full((1, 256)), full((256, 256)),
            full((N_BRANCH, BRANCH_W, d)), full((d, d)),
        ],
        out_specs=tok(d, 0),
        out_shape=jax.ShapeDtypeStruct((b, t, d), F32),
        compiler_params=_cparams(("parallel", "parallel")),
        name="merge",
    )(xc, p0, b_att, c_att, o_f, o_b, p0, gates, g1_l, g1_c, ws_bf, bias, gn, bd, wb_bf, wo_bf)


FF_CHUNK = 512


def _swiglu_chunks(h, w1_ref, w3_ref, w2_ref, lead):
    tf = w1_ref.shape[-1]
    out = None
    for c0 in range(0, tf, FF_CHUNK):
        c1 = min(c0 + FF_CHUNK, tf)
        a = _dot(h, w1_ref[lead + (slice(None), slice(c0, c1))].astype(BF16))
        bgate = _dot(h, w3_ref[lead + (slice(None), slice(c0, c1))].astype(BF16))
        act = (a * jax.nn.sigmoid(a) * bgate).astype(BF16)
        part = _dot(act, w2_ref[lead + (slice(c0, c1), slice(None))].astype(BF16))
        out = part if out is None else out + part
    return out


def _ffn_kernel(x_ref, modl_ref, modc_ref, g_ref, w1_ref, w3_ref, w2_ref, o_ref, *, tm, n_ctx):
    i = pl.program_id(1)
    x = x_ref[0]
    h = _modulated_norm(x, g_ref[...], modl_ref[0, 0:1, :], modl_ref[0, 1:2, :],
                        modc_ref[0:1, :], modc_ref[1:2, :], i * tm, n_ctx).astype(BF16)
    y = _swiglu_chunks(h, w1_ref, w3_ref, w2_ref, ())
    row = i * tm + lax.broadcasted_iota(jnp.int32, (tm, 1), 0)
    g2 = jnp.where(row < n_ctx, modc_ref[2:3, :], modl_ref[0, 2:3, :])
    o_ref[0] = x + g2 * y


def _ffn(xc, modl, modc, g, w1, w3, w2, n_ctx):
    b, t, d = xc.shape
    dff = w1.shape[1]
    tm = _pick_tile(t, (384, 256, 128))
    full = lambda bb, i: (0, 0)
    return pl.pallas_call(
        functools.partial(_ffn_kernel, tm=tm, n_ctx=n_ctx),
        grid=(b, t // tm),
        in_specs=[
            pl.BlockSpec((1, tm, d), lambda bb, i: (bb, i, 0)),
            pl.BlockSpec((1, 3, d), lambda bb, i: (bb, 0, 0)),
            pl.BlockSpec((3, d), full),
            pl.BlockSpec((1, d), full),
            pl.BlockSpec((d, dff), full),
            pl.BlockSpec((d, dff), full),
            pl.BlockSpec((dff, d), full),
        ],
        out_specs=pl.BlockSpec((1, tm, d), lambda bb, i: (bb, i, 0)),
        out_shape=jax.ShapeDtypeStruct((b, t, d), F32),
        compiler_params=_cparams(("parallel", "parallel")),
        name="ffn",
    )(xc, modl, modc, g, w1, w3, w2)


def _router_kernel(x_ref, modl_ref, modc_ref, g_ref, rw_ref, rb_ref, h_ref, ids_ref, p_ref, *, tm, n_ctx):
    i = pl.program_id(1)
    lane = lax.broadcasted_iota(jnp.int32, (tm, 128), 1)
    h = _modulated_norm(x_ref[0], g_ref[...], modl_ref[0, 0:1, :], modl_ref[0, 1:2, :],
                        modc_ref[0:1, :], modc_ref[1:2, :], i * tm, n_ctx)
    h_ref[0] = h
    h_hi = h.astype(BF16)
    h_lo = (h - h_hi.astype(F32)).astype(BF16)
    rw = rw_ref[...]
    rw_hi = rw.astype(BF16)
    rw_lo = (rw - rw_hi.astype(F32)).astype(BF16)
    logits = _dot(h_hi, rw_hi) + _dot(h_hi, rw_lo) + _dot(h_lo, rw_hi) + rb_ref[...]
    logits = jnp.where(lane < N_EXPERTS, logits, NEG_INF)
    v0 = jnp.max(logits, axis=-1, keepdims=True)
    i0 = jnp.min(jnp.where(logits == v0, lane, 128), axis=-1, keepdims=True)
    rest = jnp.where(lane == i0, NEG_INF, logits)
    v1 = jnp.max(rest, axis=-1, keepdims=True)
    i1 = jnp.min(jnp.where(rest == v1, lane, 128), axis=-1, keepdims=True)
    e1 = jnp.exp(v1 - v0)
    p0 = 1.0 / (1.0 + e1)
    ids_ref[0] = jnp.where(lane == 0, i0, jnp.where(lane == 1, i1, 0))
    p_ref[0] = jnp.where(lane == 0, p0, jnp.where(lane == 1, e1 * p0, 0.0))


def _router(xc, modl, modc, g, rw_pad, rb_pad, n_ctx):
    b, t, d = xc.shape
    tm = _pick_tile(t, (768, 384, 256, 128))
    tok = lambda w: pl.BlockSpec((1, tm, w), lambda bb, i: (bb, i, 0))
    return pl.pallas_call(
        functools.partial(_router_kernel, tm=tm, n_ctx=n_ctx),
        grid=(b, t // tm),
        in_specs=[
            tok(d),
            pl.BlockSpec((1, 3, d), lambda bb, i: (bb, 0, 0)),
            pl.BlockSpec((3, d), lambda bb, i: (0, 0)),
            pl.BlockSpec((1, d), lambda bb, i: (0, 0)),
            pl.BlockSpec((d, 128), lambda bb, i: (0, 0)),
            pl.BlockSpec((1, 128), lambda bb, i: (0, 0)),
        ],
        out_specs=[tok(d), tok(128), tok(128)],
        out_shape=[jax.ShapeDtypeStruct((b, t, d), F32), jax.ShapeDtypeStruct((b, t, 128), jnp.int32),
                   jax.ShapeDtypeStruct((b, t, 128), F32)],
        compiler_params=_cparams(("parallel", "parallel")),
        name="router",
    )(xc, modl, modc, g, rw_pad, rb_pad)


def _routing_plan(ids, tm_e):
    n_tok = ids.shape[0]
    n_asg = 2 * n_tok
    n_tiles = n_asg // tm_e + N_EXPERTS
    e_flat = jnp.concatenate([ids[:, 0], ids[:, 1]])
    onehot = (e_flat[:, None] == jnp.arange(N_EXPERTS, dtype=jnp.int32)[None, :]).astype(jnp.int32)
    csum = jnp.cumsum(onehot, axis=0)
    rank = jnp.sum(csum * onehot, axis=1) - 1
    counts = csum[-1]
    padded = ((counts + tm_e - 1) // tm_e) * tm_e
    ends = jnp.cumsum(padded)
    starts = ends - padded
    dest = jnp.sum(onehot * starts[None, :], axis=1) + rank
    row = jnp.arange(n_tiles * tm_e, dtype=jnp.int32)
    spare = n_asg + ((row // tm_e) % 2) * tm_e + row % tm_e
    asg = spare.at[dest].set(jnp.arange(n_asg, dtype=jnp.int32))
    tile_start = jnp.arange(n_tiles, dtype=jnp.int32) * tm_e
    n_used = ends[-1] // tm_e
    tile_e = jnp.minimum(jnp.sum((tile_start[:, None] >= ends[None, :]).astype(jnp.int32), axis=1), N_EXPERTS - 1)
    n_valid = jnp.clip(counts[tile_e] - (tile_start - starts[tile_e]), 0, tm_e)
    n_valid = jnp.where(tile_start < ends[-1], n_valid, 0).astype(jnp.int32)
    last_e = tile_e[jnp.maximum(n_used - 1, 0)]
    tile_e = jnp.where(tile_start < ends[-1], tile_e, last_e).astype(jnp.int32)
    src = jnp.where(asg >= n_asg, 0, jnp.where(asg >= n_tok, asg - n_tok, asg))
    return tile_e, n_valid, asg.reshape(n_tiles, 1, tm_e), src.reshape(n_tiles, 1, tm_e)


SUBLANES = 8


def _expert_kernel(te_ref, nv_ref, asg_ref, src_ref, src_next_ref, h_hbm, w1_ref, w3_ref, w2_ref, y_hbm,
                   hbuf, hbf, acc, obuf, gsem, ssem, *, n_k, tm_e):
    i = pl.program_id(0)
    k = pl.program_id(1)
    n_tiles = pl.num_programs(0)
    used = nv_ref[i] > 0
    nxt = jnp.minimum(i + 1, n_tiles - 1)
    next_used = jnp.logical_and(i + 1 < n_tiles, nv_ref[nxt] > 0)
    slot = i % 2

    def issue_gather(idx_ref, s):
        def body(g, c):
            base = pl.multiple_of(g * SUBLANES, SUBLANES)
            for jj in range(SUBLANES):
                src = idx_ref[0, 0, base + jj]
                pltpu.make_async_copy(h_hbm.at[pl.ds(src, 1), :], hbuf.at[s, pl.ds(base + jj, 1), :],
                                      gsem.at[s]).start()
            return c
        lax.fori_loop(0, tm_e // SUBLANES, body, 0)

    def wait_gather(s):
        pltpu.make_async_copy(h_hbm.at[pl.ds(0, tm_e), :], hbuf.at[s], gsem.at[s]).wait()

    def issue_scatter():
        def body(g, c):
            base = pl.multiple_of(g * SUBLANES, SUBLANES)
            for jj in range(SUBLANES):
                a = asg_ref[0, 0, base + jj]
                pltpu.make_async_copy(obuf.at[pl.ds(base + jj, 1), :], y_hbm.at[pl.ds(a, 1), :], ssem).start()
            return c
        lax.fori_loop(0, tm_e // SUBLANES, body, 0)

    def wait_scatter():
        pltpu.make_async_copy(obuf, y_hbm.at[pl.ds(0, tm_e), :], ssem).wait()

    @pl.when(jnp.logical_and(used, k == 0))
    def _():
        @pl.when(i == 0)
        def _():
            obuf[...] = jnp.zeros_like(obuf)
            n_rows = y_hbm.shape[0]
            for base in (n_rows - 2 * tm_e, n_rows - tm_e):
                pltpu.make_async_copy(obuf, y_hbm.at[pl.ds(base, tm_e), :], ssem).start()
            wait_scatter()
            wait_scatter()
            issue_gather(src_ref, 0)
        wait_gather(slot)
        hbf[...] = hbuf[slot].astype(BF16)

        @pl.when(next_used)
        def _():
            issue_gather(src_next_ref, 1 - slot)

    @pl.when(used)
    def _():
        part = _swiglu_chunks(hbf[...], w1_ref, w3_ref, w2_ref, (0, 0))

        @pl.when(k == 0)
        def _():
            acc[...] = part

        @pl.when(jnp.logical_and(k > 0, k < n_k - 1))
        def _():
            acc[...] += part

        @pl.when(k == n_k - 1)
        def _():
            @pl.when(i > 0)
            def _():
                wait_scatter()
            obuf[...] = acc[...] + part if n_k > 1 else part
            issue_scatter()

            @pl.when(jnp.logical_not(next_used))
            def _():
                wait_scatter()


def _experts(h_flat, tile_e, n_valid, asg, src, w1, w3, w2, j, tm_e):
    n_tok, d = h_flat.shape
    dff = w1.shape[-1]
    n_tiles = asg.shape[0]
    tf = _pick_tile(dff, (896, 256, 128))
    n_k = dff // tf
    kk = lambda i, k, nv: jnp.where(nv[i] > 0, k, n_k - 1)
    grid_spec = pltpu.PrefetchScalarGridSpec(
        num_scalar_prefetch=2,
        grid=(n_tiles, n_k),
        in_specs=[
            pl.BlockSpec((1, 1, tm_e), lambda i, k, te, nv: (i, 0, 0), memory_space=pltpu.SMEM),
            pl.BlockSpec((1, 1, tm_e), lambda i, k, te, nv: (i, 0, 0), memory_space=pltpu.SMEM),
            pl.BlockSpec((1, 1, tm_e), lambda i, k, te, nv: (jnp.minimum(i + 1, n_tiles - 1), 0, 0),
                         memory_space=pltpu.SMEM),
            pl.BlockSpec(memory_space=pl.ANY),
            pl.BlockSpec((1, 1, d, tf), lambda i, k, te, nv: (j, te[i], 0, kk(i, k, nv))),
            pl.BlockSpec((1, 1, d, tf), lambda i, k, te, nv: (j, te[i], 0, kk(i, k, nv))),
            pl.BlockSpec((1, 1, tf, d), lambda i, k, te, nv: (j, te[i], kk(i, k, nv), 0)),
        ],
        out_specs=pl.BlockSpec(memory_space=pl.ANY),
        scratch_shapes=[pltpu.VMEM((2, tm_e, d), F32), pltpu.VMEM((tm_e, d), BF16), pltpu.VMEM((tm_e, d), F32),
                        pltpu.VMEM((tm_e, d), F32), pltpu.SemaphoreType.DMA((2,)), pltpu.SemaphoreType.DMA(())],
    )
    return pl.pallas_call(
        functools.partial(_expert_kernel, n_k=n_k, tm_e=tm_e),
        grid_spec=grid_spec,
        out_shape=jax.ShapeDtypeStruct((2 * n_tok + 2 * tm_e, d), F32),
        compiler_params=_cparams(("arbitrary", "arbitrary")),
        name="experts",
    )(tile_e, n_valid, asg, src, src, h_flat, w1, w3, w2)


def _combine_kernel(x_ref, y0_ref, y1_ref, p_ref, gl_ref, gc_ref, o_ref, *, tm, n_ctx, first):
    i = pl.program_id(1)
    p = p_ref[0]
    lane = lax.broadcasted_iota(jnp.int32, p.shape, 1)
    p0 = jnp.sum(jnp.where(lane == 0, p, 0.0), axis=-1, keepdims=True)
    p1 = jnp.sum(jnp.where(lane == 1, p, 0.0), axis=-1, keepdims=True)
    row = (i + first) * tm + lax.broadcasted_iota(jnp.int32, (tm, 1), 0)
    g2 = jnp.where(row < n_ctx, gc_ref[...], gl_ref[0])
    o_ref[0] = x_ref[0] + g2 * (p0 * y0_ref[...] + p1 * y1_ref[...])


def _combine(xc, y, p, g2_l, g2_c, n_ctx, latent_only):
    b, t, d = xc.shape
    tm = _pick_tile(n_ctx, (256, 128)) if latent_only else _pick_tile(t, (768, 384, 256, 128))
    first = n_ctx // tm if latent_only else 0
    tpb = t // tm
    tok = lambda w: pl.BlockSpec((1, tm, w), lambda bb, i: (bb, i + first, 0))
    ysp = lambda kk: pl.BlockSpec((tm, d), lambda bb, i: ((kk * b + bb) * tpb + i + first, 0))
    return pl.pallas_call(
        functools.partial(_combine_kernel, tm=tm, n_ctx=n_ctx, first=first),
        grid=(b, tpb - first),
        in_specs=[tok(d), ysp(0), ysp(1), tok(128),
                  pl.BlockSpec((1, 1, d), lambda bb, i: (bb, 0, 0)),
                  pl.BlockSpec((1, d), lambda bb, i: (0, 0))],
        out_specs=pl.BlockSpec((1, tm, d), lambda bb, i: (bb, i, 0)),
        out_shape=jax.ShapeDtypeStruct((b, t - first * tm, d), F32),
        compiler_params=_cparams(("parallel", "parallel")),
        name="moe_combine",
    )(xc, y, y, p, g2_l, g2_c)


MOE_TILE = 512


def _moe(xc, modl, modc, g, rw_pad, rb_pad, w1, w3, w2, j, n_ctx, latent_only):
    b, t, d = xc.shape
    h, ids, p = _router(xc, modl, modc, g, rw_pad, rb_pad, n_ctx)
    tile_e, n_valid, asg, src = _routing_plan(ids.reshape(b * t, 128)[:, :2], MOE_TILE)
    y = _experts(h.reshape(b * t, d), tile_e, n_valid, asg, src, w1, w3, w2, j, MOE_TILE)
    return _combine(xc, y, p, modl[:, 2:3], modc[2:3], n_ctx, latent_only)


def _rope_tables(n_ctx, s):
    def widen(cos, sin):
        c64 = jnp.concatenate([cos, cos], axis=-1)
        s64 = jnp.concatenate([-sin, sin], axis=-1)
        return jnp.tile(c64, (1, 2)), jnp.tile(s64, (1, 2))

    n_rows = s // GRID_W
    row = jnp.repeat(jnp.arange(n_rows, dtype=F32), GRID_W)
    col = jnp.tile(jnp.arange(GRID_W, dtype=F32), n_rows)
    n_freq = HEAD_DIM // 4
    inv = ROPE_BASE ** (-jnp.arange(n_freq, dtype=F32) / n_freq)
    ang = jnp.concatenate([row[:, None] * inv, col[:, None] * inv], axis=-1)
    cos_ax = jnp.concatenate([jnp.ones((n_ctx, HEAD_DIM // 2), F32), jnp.cos(ang)], axis=0)
    sin_ax = jnp.concatenate([jnp.zeros((n_ctx, HEAD_DIM // 2), F32), jnp.sin(ang)], axis=0)
    n_freq = HEAD_DIM // 2
    inv = ROPE_BASE ** (-jnp.arange(n_freq, dtype=F32) / n_freq)
    pos = jnp.arange(n_ctx + s, dtype=F32)
    ang = pos[:, None] * inv
    return widen(cos_ax, sin_ax) + widen(jnp.cos(ang), jnp.sin(ang))


def _block_diag_ones(n):
    g = jnp.arange(n) // HEAD_DIM
    return (g[:, None] == g[None, :]).astype(BF16)


def kernel(x, c, ctx, c_ctx, w_mod, b_mod, g_norm1, g_norm2, w_in, mlp_g_v, mlp_w_s, mlp_b_s, win_q_norm,
           win_k_norm, win_sink, glb_q_norm, glb_k_norm, ret_decay, ret_gn, w_branch, w_out, ffn_w1, ffn_w3,
           ffn_w2, router_w, router_b, moe_w1, moe_w3, moe_w2):
    batch, s, d = x.shape
    n_ctx = ctx.shape[1]
    depth = w_in.shape[0]
    assert batch + 1 <= 8 and n_ctx % BLOCK == 0 and s % BLOCK == 0

    xc = jnp.concatenate([ctx, x], axis=1)
    cvec = jnp.zeros((8, d), F32).at[:batch].set(c).at[batch].set(c_ctx)
    mod = _modulation(cvec, w_mod, b_mod)
    mod = mod.reshape(depth, 8, 6, d)

    tabs = _rope_tables(n_ctx, s)
    bd256 = _block_diag_ones(256)
    ones = lambda n: jnp.ones((n,), F32)
    zeros = lambda n: jnp.zeros((n,), F32)
    scale = HEAD_DIM ** -0.5

    for layer in range(depth):
        ml = mod[layer]
        modl1 = ml[:batch, 0:2]
        modc1 = ml[batch, 0:2]
        g1_l = ml[:batch, 2:3]
        g1_c = ml[batch, 2:3]
        modl2 = ml[:batch, 3:6]
        modc2 = ml[batch, 3:6]

        def qk_gain(gq, gk, q_scale):
            return jnp.concatenate([jnp.tile(gq, 4) * q_scale, jnp.tile(gk, 2)])

        gain = jnp.stack([
            jnp.concatenate([mlp_g_v[layer].reshape(-1), ones(128)]),
            qk_gain(win_q_norm[layer], win_k_norm[layer], scale),
            qk_gain(glb_q_norm[layer], glb_k_norm[layer], scale * LOG2_E)])
        post = jnp.concatenate([ones(256), jnp.full((256,), scale, F32)])[None, :]

        p0, kv, gates = _inproj(xc, modl1, modc1, g_norm1[layer][None, :], w_in[layer].astype(BF16),
                                (gain, post, bd256), tabs, n_ctx)
        b_att = _window_attention(p0, kv, win_sink[layer], n_ctx)
        c_att = _global_attention(p0, kv, n_ctx)
        dec = jnp.broadcast_to(ret_decay[layer].reshape(-1, 1), (8, 128))
        o_f, o_b = _retention(p0, dec, n_ctx)
        bias = jnp.repeat(mlp_b_s[layer].T, HEAD_DIM, axis=1)
        xc = _merge(xc, p0, b_att, c_att, o_f, o_b, gates, g1_l, g1_c, mlp_w_s[layer].astype(BF16), bias,
                    ret_gn[layer].reshape(1, -1), bd256, w_branch[layer].astype(BF16),
                    w_out[layer].astype(BF16), n_ctx)
        j = layer // 2
        g2 = g_norm2[layer][None, :]
        if layer % 2 == 0:
            xc = _ffn(xc, modl2, modc2, g2, ffn_w1[j].astype(BF16), ffn_w3[j].astype(BF16),
                      ffn_w2[j].astype(BF16), n_ctx)
        else:
            rw_pad = jnp.zeros((d, 128), F32).at[:, :N_EXPERTS].set(router_w[j])
            rb_pad = jnp.zeros((1, 128), F32).at[0, :N_EXPERTS].set(router_b[j])
            xc = _moe(xc, modl2, modc2, g2, rw_pad, rb_pad, moe_w1, moe_w3, moe_w2, j, n_ctx,
                      latent_only=(layer == depth - 1))
    return xc if depth % 2 == 0 else xc[:, n_ctx:]
```

```python
import functools

import jax
import jax.numpy as jnp
from jax import lax
from jax.experimental import pallas as pl
from jax.experimental.pallas import tpu as pltpu

F32 = jnp.float32
BF16 = jnp.bfloat16

HEAD_DIM = 64
BLOCK = 128
GRID_W = 64
ROPE_BASE = 10000.0
N_BRANCH = 4
BRANCH_W = 256
N_EXPERTS = 8
EPS = 1e-6
NEG_INF = -1e30
LOG2_E = 1.4426950408889634
IN_TILE = 512
N_MIX_TILES = 5
VMEM_LIMIT = 56 * 1024 * 1024


def _cparams(sem):
    return pltpu.CompilerParams(dimension_semantics=sem, vmem_limit_bytes=VMEM_LIMIT)


def _dot(a, b):
    return jnp.dot(a, b, preferred_element_type=F32)


def _dot_nt(a, b):
    return lax.dot_general(a, b, (((1,), (1,)), ((), ())), preferred_element_type=F32)


def _split_dot(a, b_bf16):
    hi = a.astype(BF16)
    lo = (a - hi.astype(F32)).astype(BF16)
    return _dot(hi, b_bf16) + _dot(lo, b_bf16)


def _pick_tile(n, candidates):
    for c in candidates:
        if n % c == 0:
            return c
    raise ValueError(f"no tile for {n} in {candidates}")


def _modulated_norm(x, g, sh_l, sc_l, sh_c, sc_c, row0, n_ctx):
    tm = x.shape[0]
    ms = jnp.mean(x * x, axis=-1, keepdims=True)
    y = x * lax.rsqrt(ms + EPS) * g
    row = row0 + lax.broadcasted_iota(jnp.int32, (tm, 1), 0)
    is_ctx = row < n_ctx
    sh = jnp.where(is_ctx, sh_c, sh_l)
    sc = jnp.where(is_ctx, sc_c, sc_l)
    return y * (1.0 + sc) + sh


def _mod_kernel(c_ref, w_ref, b_ref, o_ref):
    c = c_ref[...]
    s = c * jax.nn.sigmoid(c)
    s_hi = s.astype(BF16)
    s_lo = (s - s_hi.astype(F32)).astype(BF16)
    w = w_ref[0]
    w_hi = w.astype(BF16)
    w_lo = (w - w_hi.astype(F32)).astype(BF16)
    o_ref[0] = _dot(s_hi, w_hi) + _dot(s_hi, w_lo) + _dot(s_lo, w_hi) + b_ref[0]


def _modulation(cvec, w_mod, b_mod):
    depth, d, n = w_mod.shape
    tn = _pick_tile(n, (1536, 1024, 512, 128))
    return pl.pallas_call(
        _mod_kernel,
        grid=(depth, n // tn),
        in_specs=[
            pl.BlockSpec((8, d), lambda l, j: (0, 0)),
            pl.BlockSpec((1, d, tn), lambda l, j: (l, 0, j)),
            pl.BlockSpec((1, 1, tn), lambda l, j: (l, 0, j)),
        ],
        out_specs=pl.BlockSpec((1, 8, tn), lambda l, j: (l, 0, j)),
        out_shape=jax.ShapeDtypeStruct((depth, 8, n), F32),
        compiler_params=_cparams(("parallel", "parallel")),
        name="modulation",
    )(cvec, w_mod, b_mod.reshape(depth, 1, n))


def _group_rms(v, bd, gain):
    sq = (v * v).astype(BF16)
    half = bd.shape[0]
    parts = []
    for c in range(0, v.shape[-1], half):
        wd = min(half, v.shape[-1] - c)
        parts.append(_dot(sq[:, c:c + wd], bd[:wd, :wd]))
    gs = parts[0] if len(parts) == 1 else jnp.concatenate(parts, axis=1)
    return v * lax.rsqrt(gs * (1.0 / HEAD_DIM) + EPS) * gain


def _rope(v, cos, sin):
    w = v.shape[-1]
    lane = lax.broadcasted_iota(jnp.int32, v.shape, 1)
    first = (lane & 32) == 0
    partner = jnp.where(first, pltpu.roll(v, w - 32, 1), pltpu.roll(v, 32, 1))
    return v * cos + partner * sin


def _tile(t, n):
    return jnp.concatenate([t] * n, axis=1)


def _inproj_kernel(x_ref, modl_ref, modc_ref, g_ref, w_ref, gain_ref, post_ref, bd_ref,
                   cax_ref, sax_ref, csq_ref, ssq_ref, p_ref, kv_ref, gate_ref, *, tm, n_ctx, n_tiles):
    i = pl.program_id(1)
    h = _modulated_norm(x_ref[0], g_ref[...], modl_ref[0, 0:1, :], modl_ref[0, 1:2, :],
                        modc_ref[0:1, :], modc_ref[1:2, :], i * tm, n_ctx).astype(BF16)
    lane = lax.broadcasted_iota(jnp.int32, (tm, 128), 1)
    lo = lane < HEAD_DIM

    def attn_tile(acc, j, duplicate):
        c0 = j * IN_TILE
        qk = _group_rms(acc[:, :384], bd_ref[...], gain_ref[j:j + 1, :])
        qk = _rope(qk, _tile(cax_ref[...], 3), _tile(sax_ref[...], 3))
        vv = acc[:, 384:512]
        p_ref[0, :, c0:c0 + 384] = qk.astype(BF16)
        p_ref[0, :, c0 + 384:c0 + 512] = vv.astype(BF16)
        k = qk[:, 256:384]
        kr = pltpu.roll(k, HEAD_DIM, 1)
        vr = pltpu.roll(vv, HEAD_DIM, 1)
        if duplicate:
            parts = [jnp.where(lo, k, kr), jnp.where(lo, kr, k), jnp.where(lo, vv, vr), jnp.where(lo, vr, vv)]
        else:
            pad = jnp.where(lane == HEAD_DIM, 1.0, 0.0)
            parts = [jnp.where(lo, k, pad), jnp.where(lo, kr, pad), jnp.where(lo, vv, pad), jnp.where(lo, vr, pad)]
        for n, part in enumerate(parts):
            c = (j - 1) * IN_TILE + n * 128
            kv_ref[0, :, c:c + 128] = part.astype(BF16)

    for j in range(n_tiles):
        c0 = j * IN_TILE
        acc = _dot(h, w_ref[:, c0:c0 + IN_TILE])
        if j == 0:
            v = jax.nn.gelu(acc)
            p_ref[0, :, 0:256] = v[:, :256].astype(BF16)
            p_ref[0, :, 256:512] = _group_rms(v[:, 256:], bd_ref[...], gain_ref[0:1, :256]).astype(BF16)
        elif j == 1:
            attn_tile(acc, 1, True)
        elif j == 2:
            attn_tile(acc, 2, False)
        elif j == 3:
            v = _rope(acc, _tile(csq_ref[...], 4), _tile(ssq_ref[...], 4)) * post_ref[...]
            p_ref[0, :, c0:c0 + IN_TILE] = v.astype(BF16)
        elif j < N_MIX_TILES:
            p_ref[0, :, c0:c0 + IN_TILE] = acc.astype(BF16)
        else:
            g0 = c0 - N_MIX_TILES * IN_TILE
            gate_ref[0, :, g0:g0 + IN_TILE] = (0.5 * acc).astype(BF16)


def _inproj(xc, modl, modc, g, w_bf, ep, tabs, n_ctx):
    b, t, d = xc.shape
    n_in = w_bf.shape[1]
    tm = _pick_tile(t, (384, 256, 128))
    gain, post, bd = ep
    cax, sax, csq, ssq = tabs
    n_gate = n_in - N_MIX_TILES * IN_TILE
    tok = lambda bb, i: (i, 0)
    full2 = lambda bb, i: (0, 0)
    out = lambda w: pl.BlockSpec((1, tm, w), lambda bb, i: (bb, i, 0))
    return pl.pallas_call(
        functools.partial(_inproj_kernel, tm=tm, n_ctx=n_ctx, n_tiles=n_in // IN_TILE),
        grid=(b, t // tm),
        in_specs=[
            pl.BlockSpec((1, tm, d), lambda bb, i: (bb, i, 0)),
            pl.BlockSpec((1, 2, d), lambda bb, i: (bb, 0, 0)),
            pl.BlockSpec((2, d), full2),
            pl.BlockSpec((1, d), full2),
            pl.BlockSpec((d, n_in), full2),
            pl.BlockSpec((3, 384), full2),
            pl.BlockSpec((1, IN_TILE), full2),
            pl.BlockSpec((256, 256), full2),
            pl.BlockSpec((tm, 128), tok),
            pl.BlockSpec((tm, 128), tok),
            pl.BlockSpec((tm, 128), tok),
            pl.BlockSpec((tm, 128), tok),
        ],
        out_specs=[out(N_MIX_TILES * IN_TILE), out(2 * IN_TILE), out(n_gate)],
        out_shape=[
            jax.ShapeDtypeStruct((b, t, N_MIX_TILES * IN_TILE), BF16),
            jax.ShapeDtypeStruct((b, t, 2 * IN_TILE), BF16),
            jax.ShapeDtypeStruct((b, t, n_gate), BF16),
        ],
        compiler_params=_cparams(("parallel", "parallel")),
        name="inproj",
    )(xc, modl, modc, g, w_bf, gain, post, bd, cax, sax, csq, ssq)


def _stack_heads(qa):
    qf = qa.astype(F32)
    lo = lax.broadcasted_iota(jnp.int32, qf.shape, 1) < HEAD_DIM
    return jnp.concatenate([jnp.where(lo, qf, 0.0), jnp.where(lo, 0.0, qf)], axis=0).astype(BF16)


def _unstack_heads(o2):
    n = o2.shape[0] // 2
    lo = lax.broadcasted_iota(jnp.int32, (n, o2.shape[1]), 1) < HEAD_DIM
    return jnp.where(lo, o2[:n], o2[n:])


def _win_kernel(sink_ref, q_ref, kc_ref, vc_ref, kp_ref, kx_ref, kn_ref, vp_ref, vx_ref, vn_ref, o_ref,
                *, n_ctx_blocks, n_blocks, per_step):
    j = pl.program_id(1)
    lc = n_ctx_blocks * BLOCK
    w = lc + 3 * BLOCK
    row = lax.broadcasted_iota(jnp.int32, (BLOCK, w), 0)
    col = lax.broadcasted_iota(jnp.int32, (BLOCK, w), 1)
    c = col - lc
    rid = lax.broadcasted_iota(jnp.int32, (2 * BLOCK, 1), 0)
    blocks = ([(kp_ref, vp_ref, slice(None))]
              + [(kx_ref, vx_ref, slice(u * BLOCK, (u + 1) * BLOCK)) for u in range(per_step)]
              + [(kn_ref, vn_ref, slice(None))])
    for sub in range(per_step):
        n = per_step * j + sub
        rows = slice(sub * BLOCK, (sub + 1) * BLOCK)
        is_lat = n >= n_ctx_blocks
        prev_ok = n > n_ctx_blocks
        next_ok = n < n_blocks - 1
        lo_b = jnp.where(prev_ok, row, BLOCK)
        hi_b = jnp.where(next_ok, row + 2 * BLOCK, 2 * BLOCK - 1)
        lat_bias = jnp.where(is_lat, 0.0, NEG_INF)
        bias = jnp.where(col < lc, 0.0, jnp.where(c >= lo_b, jnp.where(c <= hi_b, lat_bias, NEG_INF), NEG_INF))
        bias2 = jnp.concatenate([bias, bias], axis=0)
        outs = []
        for a in range(2):
            sl = slice(128 * a, 128 * (a + 1))
            q2 = _stack_heads(q_ref[0, rows, sl])
            local = blocks[sub:sub + 3]
            keys = jnp.concatenate([kc_ref[0, :, sl]] + [kr[0, rs, sl] for kr, _, rs in local], axis=0)
            vals = jnp.concatenate([vc_ref[0, :, sl]] + [vr[0, rs, sl] for _, vr, rs in local], axis=0)
            s = _dot_nt(q2, keys) + bias2
            sk = jnp.where(rid < BLOCK, sink_ref[2 * a], sink_ref[2 * a + 1])
            m = jnp.maximum(jnp.max(s, axis=-1, keepdims=True), sk)
            e = jnp.exp(s - m)
            den = jnp.sum(e, axis=-1, keepdims=True) + jnp.exp(sk - m)
            p = (e / den).astype(BF16)
            outs.append(_unstack_heads(_dot(p, vals)))
        o_ref[0, rows, :] = jnp.concatenate(outs, axis=1).astype(BF16)


def _window_attention(p0, kv, sink, n_ctx):
    b, t, _ = p0.shape
    nb = t // BLOCK
    ncb = n_ctx // BLOCK
    ps = _pick_tile(nb, (6, 5, 4, 3, 2, 1))
    span = lambda cidx: pl.BlockSpec((1, ps * BLOCK, 256), lambda bb, j: (bb, j, cidx))
    prv = lambda cidx: pl.BlockSpec((1, BLOCK, 256), lambda bb, j: (bb, jnp.maximum(ps * j - 1, 0), cidx))
    nxt = lambda cidx: pl.BlockSpec((1, BLOCK, 256), lambda bb, j: (bb, jnp.minimum(ps * j + ps, nb - 1), cidx))
    ctx = lambda cidx: pl.BlockSpec((1, n_ctx, 256), lambda bb, j: (bb, 0, cidx))
    return pl.pallas_call(
        functools.partial(_win_kernel, n_ctx_blocks=ncb, n_blocks=nb, per_step=ps),
        grid=(b, nb // ps),
        in_specs=[
            pl.BlockSpec(memory_space=pltpu.SMEM),
            span(2),
            ctx(0), ctx(1),
            prv(0), span(0), nxt(0),
            prv(1), span(1), nxt(1),
        ],
        out_specs=pl.BlockSpec((1, ps * BLOCK, 256), lambda bb, j: (bb, j, 0)),
        out_shape=jax.ShapeDtypeStruct((b, t, 256), BF16),
        compiler_params=_cparams(("parallel", "parallel")),
        name="window_attn",
    )(sink, p0, kv, kv, kv, kv, kv, kv, kv, kv)


SAFE_LOG2_BOUND = 60.0


def _glb_kernel(q_ref, k_ref, v_ref, o_ref, kmax_scr, *, tq, tk, n_ctx, t_all):
    i = pl.program_id(1)
    sls = (slice(0, 128), slice(128, 256))
    lane = lax.broadcasted_iota(jnp.int32, (2 * tq, 128), 1)

    @pl.when(i == 0)
    def _():
        klane = lax.broadcasted_iota(jnp.int32, (tk, 128), 1)
        for a in range(2):
            def kbody(ci, mx):
                start = pl.multiple_of(ci * tk, tk)
                k = k_ref[0, pl.ds(start, tk), sls[a]].astype(F32)
                sq = jnp.sum(jnp.where(klane < HEAD_DIM, k * k, 0.0), axis=-1, keepdims=True)
                return jnp.maximum(mx, jnp.max(sq, axis=0, keepdims=True))
            mx = lax.fori_loop(0, t_all // tk, kbody, jnp.zeros((1, 1), F32))
            kmax_scr[a:a + 1, :] = jnp.broadcast_to(jnp.sqrt(mx), (1, 128))

    q2s, bnds = [], []
    for a in range(2):
        q = q_ref[0, :, sls[a]].astype(F32)
        q2 = jnp.concatenate([q, pltpu.roll(q, HEAD_DIM, 1)], axis=0)
        q2 = jnp.where(lane < HEAD_DIM, q2, 0.0)
        nrm = jnp.sqrt(jnp.sum(q2 * q2, axis=-1, keepdims=True))
        bnd = nrm * kmax_scr[a:a + 1, 0:1] * 1.01
        q2s.append(q2)
        bnds.append(bnd)
    worst = jnp.max(jnp.maximum(bnds[0], bnds[1]))

    def finish(res):
        outs = []
        for acc in res:
            den = jnp.sum(jnp.where(lane == HEAD_DIM, acc, 0.0), axis=-1, keepdims=True)
            o = acc / den
            lo_t = lax.broadcasted_iota(jnp.int32, (tq, 128), 1) < HEAD_DIM
            outs.append(jnp.where(lo_t, o[:tq], pltpu.roll(o, HEAD_DIM, 1)[tq:]))
        o_ref[0] = jnp.concatenate(outs, axis=1).astype(BF16)

    def attend_fixed(n_chunks, size):
        qb = [jnp.where(lane == HEAD_DIM, -bnds[a], q2s[a]).astype(BF16) for a in range(2)]

        def body(ci, carry):
            start = pl.multiple_of(ci * size, size)
            out = []
            for a in range(2):
                k = k_ref[0, pl.ds(start, size), sls[a]]
                v = v_ref[0, pl.ds(start, size), sls[a]]
                p = jnp.exp2(_dot_nt(qb[a], k))
                out.append(carry[a] + _dot(p.astype(BF16), v))
            return tuple(out)

        zero = jnp.zeros((2 * tq, 128), F32)
        finish(lax.fori_loop(0, n_chunks, body, (zero, zero), unroll=True))

    def attend_online(n_chunks, size):
        qb = [q2s[a].astype(BF16) for a in range(2)]

        def body(ci, carry):
            start = pl.multiple_of(ci * size, size)
            out = []
            for a in range(2):
                m, acc = carry[a]
                k = k_ref[0, pl.ds(start, size), sls[a]]
                v = v_ref[0, pl.ds(start, size), sls[a]]
                s = _dot_nt(qb[a], k)
                m_new = jnp.maximum(m, jnp.max(s, axis=-1, keepdims=True))
                p = jnp.exp2(s - m_new)
                acc = jnp.exp2(m - m_new) * acc + _dot(p.astype(BF16), v)
                out.append((m_new, acc))
            return tuple(out)

        init = (jnp.full((2 * tq, 1), NEG_INF, F32), jnp.zeros((2 * tq, 128), F32))
        res = lax.fori_loop(0, n_chunks, body, (init, init))
        finish([acc for (_, acc) in res])

    is_ctx = (i + 1) * tq <= n_ctx
    safe = worst <= SAFE_LOG2_BOUND

    @pl.when(jnp.logical_and(is_ctx, safe))
    def _():
        attend_fixed(1, n_ctx)

    @pl.when(jnp.logical_and(is_ctx, jnp.logical_not(safe)))
    def _():
        attend_online(1, n_ctx)

    @pl.when(jnp.logical_and(jnp.logical_not(is_ctx), safe))
    def _():
        attend_fixed(t_all // tk, tk)

    @pl.when(jnp.logical_and(jnp.logical_not(is_ctx), jnp.logical_not(safe)))
    def _():
        attend_online(t_all // tk, tk)


def _global_attention(p0, kv, n_ctx):
    b, t, _ = p0.shape
    tq = _pick_tile(n_ctx, (256, 128))
    tk = _pick_tile(t, (1408, 768, 384, 256, 128))
    return pl.pallas_call(
        functools.partial(_glb_kernel, tq=tq, tk=tk, n_ctx=n_ctx, t_all=t),
        grid=(b, t // tq),
        in_specs=[
            pl.BlockSpec((1, tq, 256), lambda bb, i: (bb, i, 4)),
            pl.BlockSpec((1, t, 256), lambda bb, i: (bb, 0, 2)),
            pl.BlockSpec((1, t, 256), lambda bb, i: (bb, 0, 3)),
        ],
        out_specs=pl.BlockSpec((1, tq, 256), lambda bb, i: (bb, i, 0)),
        out_shape=jax.ShapeDtypeStruct((b, t, 256), BF16),
        scratch_shapes=[pltpu.VMEM((8, 128), F32)],
        compiler_params=_cparams(("arbitrary", "arbitrary")),
        name="global_attn",
    )(p0, kv, kv)


_TAB_D, _TAB_Q, _TAB_K, _TAB_C, _TAB_ROWS = 0, 256, 384, 512, 640


def _ret_kernel(dec_ref, qf_ref, kf_ref, vf_ref, qb_ref, kb_ref, vb_ref, of_ref, ob_ref, st_scr, tab_scr,
                *, batch):
    i = pl.program_id(0)
    r = lax.broadcasted_iota(jnp.int32, (BLOCK, BLOCK), 0)
    c = lax.broadcasted_iota(jnp.int32, (BLOCK, BLOCK), 1)
    lane_lo = c < HEAD_DIM
    same_head = (r < HEAD_DIM) == lane_lo

    @pl.when(i == 0)
    def _():
        st_scr[...] = jnp.zeros_like(st_scr)
        x = dec_ref[...]
        lg = jnp.where(x >= 0.0, -jnp.log(1.0 + jnp.exp(-x)), x - jnp.log(1.0 + jnp.exp(x)))
        rf = r.astype(F32)
        cf = c.astype(F32)
        for d in range(2):
            for pair in range(2):
                le = lg[4 * d + 2 * pair:4 * d + 2 * pair + 1, :]
                lo_ = lg[4 * d + 2 * pair + 1:4 * d + 2 * pair + 2, :]
                lp = jnp.where(lane_lo[0:1, :], le, lo_)
                if d == 0:
                    rel = rf - cf
                    qpow = rf + 1.0
                    kpow = (BLOCK - 1.0) - rf
                else:
                    rel = cf - rf - 1.0
                    qpow = (BLOCK - 1.0) - rf
                    kpow = rf
                msk = rel >= 0.0
                relc = jnp.where(msk, rel, 0.0)
                tab_scr[d, pair, _TAB_D:_TAB_D + BLOCK, :] = jnp.where(msk, jnp.exp(le * relc), 0.0)
                tab_scr[d, pair, _TAB_D + BLOCK:_TAB_Q, :] = jnp.where(msk, jnp.exp(lo_ * relc), 0.0)
                tab_scr[d, pair, _TAB_Q:_TAB_K, :] = jnp.exp(lp * qpow)
                tab_scr[d, pair, _TAB_K:_TAB_C, :] = jnp.exp(lp * kpow)
                tab_scr[d, pair, _TAB_C:_TAB_ROWS, :] = jnp.where(
                    r < HEAD_DIM, jnp.exp(le * float(BLOCK)), jnp.exp(lo_ * float(BLOCK)))

    for d, (q_ref, k_ref, v_ref, o_ref) in enumerate(((qf_ref, kf_ref, vf_ref, of_ref),
                                                     (qb_ref, kb_ref, vb_ref, ob_ref))):
        for bb in range(batch):
            for pair in range(2):
                sl = slice(128 * pair, 128 * (pair + 1))
                q = q_ref[bb, :, sl].astype(F32)
                k = k_ref[bb, :, sl].astype(F32)
                v = v_ref[bb, :, sl]
                q2 = jnp.concatenate([jnp.where(lane_lo, q, 0.0), jnp.where(lane_lo, 0.0, q)], axis=0)
                qk = _dot_nt(q2.astype(BF16), k.astype(BF16)) * tab_scr[d, pair, _TAB_D:_TAB_Q, :]
                o_intra = _unstack_heads(_dot(qk.astype(BF16), v))
                s = st_scr[d, bb, pair]
                qd = q * tab_scr[d, pair, _TAB_Q:_TAB_K, :]
                o_ref[bb, :, sl] = o_intra + _dot(qd.astype(BF16), s.astype(BF16))
                kd = k * tab_scr[d, pair, _TAB_K:_TAB_C, :]
                upd = _dot(kd.T.astype(BF16), v)
                st_scr[d, bb, pair] = jnp.where(same_head, tab_scr[d, pair, _TAB_C:_TAB_ROWS, :] * s + upd, 0.0)


def _retention(p0, dec, n_ctx):
    b, t, _ = p0.shape
    nb = t // BLOCK
    ncb = n_ctx // BLOCK
    bwd = lambda i: jnp.where(i < ncb, ncb - 1 - i, nb - 1 + ncb - i)
    fspec = lambda cidx: pl.BlockSpec((b, BLOCK, 256), lambda i: (0, i, cidx))
    bspec = lambda cidx: pl.BlockSpec((b, BLOCK, 256), lambda i: (0, bwd(i), cidx))
    return pl.pallas_call(
        functools.partial(_ret_kernel, batch=b),
        grid=(nb,),
        in_specs=[
            pl.BlockSpec((8, 128), lambda i: (0, 0)),
            fspec(6), fspec(7), fspec(8),
            bspec(6), bspec(7), bspec(8),
        ],
        out_specs=[
            pl.BlockSpec((b, BLOCK, 256), lambda i: (0, i, 0)),
            pl.BlockSpec((b, BLOCK, 256), lambda i: (0, bwd(i), 0)),
        ],
        out_shape=[jax.ShapeDtypeStruct((b, t, 256), F32), jax.ShapeDtypeStruct((b, t, 256), F32)],
        scratch_shapes=[pltpu.VMEM((2, b, 2, BLOCK, BLOCK), F32),
                        pltpu.VMEM((2, 2, _TAB_ROWS, BLOCK), F32)],
        compiler_params=_cparams(("arbitrary",)),
        name="retention",
    )(dec, p0, p0, p0, p0, p0, p0)


def _merge_kernel(x_ref, uv_ref, batt_ref, catt_ref, of_ref, ob_ref, rg_ref, gate_ref, gl_ref, gc_ref,
                  ws_ref, bs_ref, gn_ref, bd_ref, wb_ref, wo_ref, o_ref, *, tm, n_ctx):
    i = pl.program_id(1)
    grp = jnp.right_shift(lax.broadcasted_iota(jnp.int32, (BLOCK, 256), 1), 6)
    a_chunks = []
    for ch in range(tm // BLOCK):
        rows = slice(ch * BLOCK, (ch + 1) * BLOCK)
        vn = uv_ref[0, rows, 256:512]
        mixed = bs_ref[...]
        for g in range(4):
            mixed = mixed + jnp.where(grp == g, _dot(ws_ref[g], vn), 0.0)
        a_chunks.append((uv_ref[0, rows, 0:256].astype(F32) * mixed).astype(BF16))
    a_br = jnp.concatenate(a_chunks, axis=0)
    o = of_ref[0] + ob_ref[0]
    mean = _split_dot(o, bd_ref[...]) * (1.0 / HEAD_DIM)
    oc = o - mean
    var = _split_dot(oc * oc, bd_ref[...]) * (1.0 / HEAD_DIM)
    rg = rg_ref[0].astype(F32)
    d_br = (oc * lax.rsqrt(var + EPS) * gn_ref[...] * (rg * jax.nn.sigmoid(rg))).astype(BF16)
    branches = (a_br, batt_ref[0], catt_ref[0], d_br)
    d_model = x_ref.shape[-1]
    acc = None
    for n in range(N_BRANCH):
        th = jnp.tanh(gate_ref[0, :, n * d_model:(n + 1) * d_model].astype(F32))
        y = _dot(branches[n], wb_ref[n])
        term = y + th * y
        acc = term if acc is None else acc + term
    m = _dot((0.5 * acc).astype(BF16), wo_ref[...])
    row = i * tm + lax.broadcasted_iota(jnp.int32, (tm, 1), 0)
    g1 = jnp.where(row < n_ctx, gc_ref[...], gl_ref[0])
    o_ref[0] = x_ref[0] + g1 * m


def _merge(xc, p0, b_att, c_att, o_f, o_b, gates, g1_l, g1_c, ws_bf, bias, gn, bd, wb_bf, wo_bf, n_ctx):
    b, t, d = xc.shape
    tm = _pick_tile(t, (384, 256, 128))
    tok = lambda w, cidx: pl.BlockSpec((1, tm, w), lambda bb, i: (bb, i, cidx))
    full = lambda shape: pl.BlockSpec(shape, lambda bb, i: (0,) * len(shape))
    return pl.pallas_call(
        functools.partial(_merge_kernel, tm=tm, n_ctx=n_ctx),
        grid=(b, t // tm),
        in_specs=[
            tok(d, 0),
            tok(512, 0),
            tok(256, 0), tok(256, 0),
            tok(256, 0), tok(256, 0),
            tok(256, 9),
            tok(N_BRANCH * d, 0),
            pl.BlockSpec((1, 1, d), lambda bb, i: (bb, 0, 0)),
            full((1, d)),
            full((4, BLOCK, BLOCK)), full((BLOCK, 256)), full((1, 256)), full((256, 256)),
            full((N_BRANCH, BRANCH_W, d)), full((d, d)),
        ],
        out_specs=tok(d, 0),
        out_shape=jax.ShapeDtypeStruct((b, t, d), F32),
        compiler_params=_cparams(("parallel", "parallel")),
        name="merge",
    )(xc, p0, b_att, c_att, o_f, o_b, p0, gates, g1_l, g1_c, ws_bf, bias, gn, bd, wb_bf, wo_bf)


FF_CHUNK = 512


def _swiglu_chunks(h, w1_ref, w3_ref, w2_ref, lead):
    tf = w1_ref.shape[-1]
    out = None
    for c0 in range(0, tf, FF_CHUNK):
        c1 = min(c0 + FF_CHUNK, tf)
        a = _dot(h, w1_ref[lead + (slice(None), slice(c0, c1))].astype(BF16))
        bgate = _dot(h, w3_ref[lead + (slice(None), slice(c0, c1))].astype(BF16))
        act = (a * jax.nn.sigmoid(a) * bgate).astype(BF16)
        part = _dot(act, w2_ref[lead + (slice(c0, c1), slice(None))].astype(BF16))
        out = part if out is None else out + part
    return out


def _ffn_kernel(x_ref, modl_ref, modc_ref, g_ref, w1_ref, w3_ref, w2_ref, o_ref, *, tm, n_ctx):
    i = pl.program_id(1)
    x = x_ref[0]
    h = _modulated_norm(x, g_ref[...], modl_ref[0, 0:1, :], modl_ref[0, 1:2, :],
                        modc_ref[0:1, :], modc_ref[1:2, :], i * tm, n_ctx).astype(BF16)
    y = _swiglu_chunks(h, w1_ref, w3_ref, w2_ref, ())
    row = i * tm + lax.broadcasted_iota(jnp.int32, (tm, 1), 0)
    g2 = jnp.where(row < n_ctx, modc_ref[2:3, :], modl_ref[0, 2:3, :])
    o_ref[0] = x + g2 * y


def _ffn(xc, modl, modc, g, w1, w3, w2, n_ctx):
    b, t, d = xc.shape
    dff = w1.shape[1]
    tm = _pick_tile(t, (384, 256, 128))
    full = lambda bb, i: (0, 0)
    return pl.pallas_call(
        functools.partial(_ffn_kernel, tm=tm, n_ctx=n_ctx),
        grid=(b, t // tm),
        in_specs=[
            pl.BlockSpec((1, tm, d), lambda bb, i: (bb, i, 0)),
            pl.BlockSpec((1, 3, d), lambda bb, i: (bb, 0, 0)),
            pl.BlockSpec((3, d), full),
            pl.BlockSpec((1, d), full),
            pl.BlockSpec((d, dff), full),
            pl.BlockSpec((d, dff), full),
            pl.BlockSpec((dff, d), full),
        ],
        out_specs=pl.BlockSpec((1, tm, d), lambda bb, i: (bb, i, 0)),
        out_shape=jax.ShapeDtypeStruct((b, t, d), F32),
        compiler_params=_cparams(("parallel", "parallel")),
        name="ffn",
    )(xc, modl, modc, g, w1, w3, w2)


def _router_kernel(x_ref, modl_ref, modc_ref, g_ref, rw_ref, rb_ref, h_ref, ids_ref, p_ref, *, tm, n_ctx):
    i = pl.program_id(1)
    lane = lax.broadcasted_iota(jnp.int32, (tm, 128), 1)
    h = _modulated_norm(x_ref[0], g_ref[...], modl_ref[0, 0:1, :], modl_ref[0, 1:2, :],
                        modc_ref[0:1, :], modc_ref[1:2, :], i * tm, n_ctx)
    h_ref[0] = h
    h_hi = h.astype(BF16)
    h_lo = (h - h_hi.astype(F32)).astype(BF16)
    rw = rw_ref[...]
    rw_hi = rw.astype(BF16)
    rw_lo = (rw - rw_hi.astype(F32)).astype(BF16)
    logits = _dot(h_hi, rw_hi) + _dot(h_hi, rw_lo) + _dot(h_lo, rw_hi) + rb_ref[...]
    logits = jnp.where(lane < N_EXPERTS, logits, NEG_INF)
    v0 = jnp.max(logits, axis=-1, keepdims=True)
    i0 = jnp.min(jnp.where(logits == v0, lane, 128), axis=-1, keepdims=True)
    rest = jnp.where(lane == i0, NEG_INF, logits)
    v1 = jnp.max(rest, axis=-1, keepdims=True)
    i1 = jnp.min(jnp.where(rest == v1, lane, 128), axis=-1, keepdims=True)
    e1 = jnp.exp(v1 - v0)
    p0 = 1.0 / (1.0 + e1)
    ids_ref[0] = jnp.where(lane == 0, i0, jnp.where(lane == 1, i1, 0))
    p_ref[0] = jnp.where(lane == 0, p0, jnp.where(lane == 1, e1 * p0, 0.0))


def _router(xc, modl, modc, g, rw_pad, rb_pad, n_ctx):
    b, t, d = xc.shape
    tm = _pick_tile(t, (768, 384, 256, 128))
    tok = lambda w: pl.BlockSpec((1, tm, w), lambda bb, i: (bb, i, 0))
    return pl.pallas_call(
        functools.partial(_router_kernel, tm=tm, n_ctx=n_ctx),
        grid=(b, t // tm),
        in_specs=[
            tok(d),
            pl.BlockSpec((1, 3, d), lambda bb, i: (bb, 0, 0)),
            pl.BlockSpec((3, d), lambda bb, i: (0, 0)),
            pl.BlockSpec((1, d), lambda bb, i: (0, 0)),
            pl.BlockSpec((d, 128), lambda bb, i: (0, 0)),
            pl.BlockSpec((1, 128), lambda bb, i: (0, 0)),
        ],
        out_specs=[tok(d), tok(128), tok(128)],
        out_shape=[jax.ShapeDtypeStruct((b, t, d), F32), jax.ShapeDtypeStruct((b, t, 128), jnp.int32),
                   jax.ShapeDtypeStruct((b, t, 128), F32)],
        compiler_params=_cparams(("parallel", "parallel")),
        name="router",
    )(xc, modl, modc, g, rw_pad, rb_pad)


def _routing_plan(ids, tm_e):
    n_tok = ids.shape[0]
    n_asg = 2 * n_tok
    n_tiles = n_asg // tm_e + N_EXPERTS
    e_flat = jnp.concatenate([ids[:, 0], ids[:, 1]])
    onehot = (e_flat[:, None] == jnp.arange(N_EXPERTS, dtype=jnp.int32)[None, :]).astype(jnp.int32)
    csum = jnp.cumsum(onehot, axis=0)
    rank = jnp.sum(csum * onehot, axis=1) - 1
    counts = csum[-1]
    padded = ((counts + tm_e - 1) // tm_e) * tm_e
    ends = jnp.cumsum(padded)
    starts = ends - padded
    dest = jnp.sum(onehot * starts[None, :], axis=1) + rank
    row = jnp.arange(n_tiles * tm_e, dtype=jnp.int32)
    spare = n_asg + ((row // tm_e) % 2) * tm_e + row % tm_e
    asg = spare.at[dest].set(jnp.arange(n_asg, dtype=jnp.int32))
    tile_start = jnp.arange(n_tiles, dtype=jnp.int32) * tm_e
    n_used = ends[-1] // tm_e
    tile_e = jnp.minimum(jnp.sum((tile_start[:, None] >= ends[None, :]).astype(jnp.int32), axis=1), N_EXPERTS - 1)
    n_valid = jnp.clip(counts[tile_e] - (tile_start - starts[tile_e]), 0, tm_e)
    n_valid = jnp.where(tile_start < ends[-1], n_valid, 0).astype(jnp.int32)
    last_e = tile_e[jnp.maximum(n_used - 1, 0)]
    tile_e = jnp.where(tile_start < ends[-1], tile_e, last_e).astype(jnp.int32)
    src = jnp.where(asg >= n_asg, 0, jnp.where(asg >= n_tok, asg - n_tok, asg))
    return tile_e, n_valid, asg.reshape(n_tiles, 1, tm_e), src.reshape(n_tiles, 1, tm_e)


SUBLANES = 8


def _expert_kernel(te_ref, nv_ref, asg_ref, src_ref, src_next_ref, h_hbm, w1_ref, w3_ref, w2_ref, y_hbm,
                   hbuf, hbf, acc, obuf, gsem, ssem, *, n_k, tm_e):
    i = pl.program_id(0)
    k = pl.program_id(1)
    n_tiles = pl.num_programs(0)
    used = nv_ref[i] > 0
    nxt = jnp.minimum(i + 1, n_tiles - 1)
    next_used = jnp.logical_and(i + 1 < n_tiles, nv_ref[nxt] > 0)
    slot = i % 2

    def issue_gather(idx_ref, s):
        def body(g, c):
            base = pl.multiple_of(g * SUBLANES, SUBLANES)
            for jj in range(SUBLANES):
                src = idx_ref[0, 0, base + jj]
                pltpu.make_async_copy(h_hbm.at[pl.ds(src, 1), :], hbuf.at[s, pl.ds(base + jj, 1), :],
                                      gsem.at[s]).start()
            return c
        lax.fori_loop(0, tm_e // SUBLANES, body, 0)

    def wait_gather(s):
        pltpu.make_async_copy(h_hbm.at[pl.ds(0, tm_e), :], hbuf.at[s], gsem.at[s]).wait()

    def issue_scatter():
        def body(g, c):
            base = pl.multiple_of(g * SUBLANES, SUBLANES)
            for jj in range(SUBLANES):
                a = asg_ref[0, 0, base + jj]
                pltpu.make_async_copy(obuf.at[pl.ds(base + jj, 1), :], y_hbm.at[pl.ds(a, 1), :], ssem).start()
            return c
        lax.fori_loop(0, tm_e // SUBLANES, body, 0)

    def wait_scatter():
        pltpu.make_async_copy(obuf, y_hbm.at[pl.ds(0, tm_e), :], ssem).wait()

    @pl.when(jnp.logical_and(used, k == 0))
    def _():
        @pl.when(i == 0)
        def _():
            obuf[...] = jnp.zeros_like(obuf)
            n_rows = y_hbm.shape[0]
            for base in (n_rows - 2 * tm_e, n_rows - tm_e):
                pltpu.make_async_copy(obuf, y_hbm.at[pl.ds(base, tm_e), :], ssem).start()
            wait_scatter()
            wait_scatter()
            issue_gather(src_ref, 0)
        wait_gather(slot)
        hbf[...] = hbuf[slot].astype(BF16)

        @pl.when(next_used)
        def _():
            issue_gather(src_next_ref, 1 - slot)

    @pl.when(used)
    def _():
        part = _swiglu_chunks(hbf[...], w1_ref, w3_ref, w2_ref, (0, 0))

        @pl.when(k == 0)
        def _():
            acc[...] = part

        @pl.when(jnp.logical_and(k > 0, k < n_k - 1))
        def _():
            acc[...] += part

        @pl.when(k == n_k - 1)
        def _():
            @pl.when(i > 0)
            def _():
                wait_scatter()
            obuf[...] = acc[...] + part if n_k > 1 else part
            issue_scatter()

            @pl.when(jnp.logical_not(next_used))
            def _():
                wait_scatter()


def _experts(h_flat, tile_e, n_valid, asg, src, w1, w3, w2, j, tm_e):
    n_tok, d = h_flat.shape
    dff = w1.shape[-1]
    n_tiles = asg.shape[0]
    tf = _pick_tile(dff, (896, 256, 128))
    n_k = dff // tf
    kk = lambda i, k, nv: jnp.where(nv[i] > 0, k, n_k - 1)
    grid_spec = pltpu.PrefetchScalarGridSpec(
        num_scalar_prefetch=2,
        grid=(n_tiles, n_k),
        in_specs=[
            pl.BlockSpec((1, 1, tm_e), lambda i, k, te, nv: (i, 0, 0), memory_space=pltpu.SMEM),
            pl.BlockSpec((1, 1, tm_e), lambda i, k, te, nv: (i, 0, 0), memory_space=pltpu.SMEM),
            pl.BlockSpec((1, 1, tm_e), lambda i, k, te, nv: (jnp.minimum(i + 1, n_tiles - 1), 0, 0),
                         memory_space=pltpu.SMEM),
            pl.BlockSpec(memory_space=pl.ANY),
            pl.BlockSpec((1, 1, d, tf), lambda i, k, te, nv: (j, te[i], 0, kk(i, k, nv))),
            pl.BlockSpec((1, 1, d, tf), lambda i, k, te, nv: (j, te[i], 0, kk(i, k, nv))),
            pl.BlockSpec((1, 1, tf, d), lambda i, k, te, nv: (j, te[i], kk(i, k, nv), 0)),
        ],
        out_specs=pl.BlockSpec(memory_space=pl.ANY),
        scratch_shapes=[pltpu.VMEM((2, tm_e, d), F32), pltpu.VMEM((tm_e, d), BF16), pltpu.VMEM((tm_e, d), F32),
                        pltpu.VMEM((tm_e, d), F32), pltpu.SemaphoreType.DMA((2,)), pltpu.SemaphoreType.DMA(())],
    )
    return pl.pallas_call(
        functools.partial(_expert_kernel, n_k=n_k, tm_e=tm_e),
        grid_spec=grid_spec,
        out_shape=jax.ShapeDtypeStruct((2 * n_tok + 2 * tm_e, d), F32),
        compiler_params=_cparams(("arbitrary", "arbitrary")),
        name="experts",
    )(tile_e, n_valid, asg, src, src, h_flat, w1, w3, w2)


def _combine_kernel(x_ref, y0_ref, y1_ref, p_ref, gl_ref, gc_ref, o_ref, *, tm, n_ctx, first):
    i = pl.program_id(1)
    p = p_ref[0]
    lane = lax.broadcasted_iota(jnp.int32, p.shape, 1)
    p0 = jnp.sum(jnp.where(lane == 0, p, 0.0), axis=-1, keepdims=True)
    p1 = jnp.sum(jnp.where(lane == 1, p, 0.0), axis=-1, keepdims=True)
    row = (i + first) * tm + lax.broadcasted_iota(jnp.int32, (tm, 1), 0)
    g2 = jnp.where(row < n_ctx, gc_ref[...], gl_ref[0])
    o_ref[0] = x_ref[0] + g2 * (p0 * y0_ref[...] + p1 * y1_ref[...])


def _combine(xc, y, p, g2_l, g2_c, n_ctx, latent_only):
    b, t, d = xc.shape
    tm = _pick_tile(n_ctx, (256, 128)) if latent_only else _pick_tile(t, (768, 384, 256, 128))
    first = n_ctx // tm if latent_only else 0
    tpb = t // tm
    tok = lambda w: pl.BlockSpec((1, tm, w), lambda bb, i: (bb, i + first, 0))
    ysp = lambda kk: pl.BlockSpec((tm, d), lambda bb, i: ((kk * b + bb) * tpb + i + first, 0))
    return pl.pallas_call(
        functools.partial(_combine_kernel, tm=tm, n_ctx=n_ctx, first=first),
        grid=(b, tpb - first),
        in_specs=[tok(d), ysp(0), ysp(1), tok(128),
                  pl.BlockSpec((1, 1, d), lambda bb, i: (bb, 0, 0)),
                  pl.BlockSpec((1, d), lambda bb, i: (0, 0))],
        out_specs=pl.BlockSpec((1, tm, d), lambda bb, i: (bb, i, 0)),
        out_shape=jax.ShapeDtypeStruct((b, t - first * tm, d), F32),
        compiler_params=_cparams(("parallel", "parallel")),
        name="moe_combine",
    )(xc, y, y, p, g2_l, g2_c)


MOE_TILE = 512


def _moe(xc, modl, modc, g, rw_pad, rb_pad, w1, w3, w2, j, n_ctx, latent_only):
    b, t, d = xc.shape
    h, ids, p = _router(xc, modl, modc, g, rw_pad, rb_pad, n_ctx)
    tile_e, n_valid, asg, src = _routing_plan(ids.reshape(b * t, 128)[:, :2], MOE_TILE)
    y = _experts(h.reshape(b * t, d), tile_e, n_valid, asg, src, w1, w3, w2, j, MOE_TILE)
    return _combine(xc, y, p, modl[:, 2:3], modc[2:3], n_ctx, latent_only)


def _rope_tables(n_ctx, s):
    def widen(cos, sin):
        c64 = jnp.concatenate([cos, cos], axis=-1)
        s64 = jnp.concatenate([-sin, sin], axis=-1)
        return jnp.tile(c64, (1, 2)), jnp.tile(s64, (1, 2))

    n_rows = s // GRID_W
    row = jnp.repeat(jnp.arange(n_rows, dtype=F32), GRID_W)
    col = jnp.tile(jnp.arange(GRID_W, dtype=F32), n_rows)
    n_freq = HEAD_DIM // 4
    inv = ROPE_BASE ** (-jnp.arange(n_freq, dtype=F32) / n_freq)
    ang = jnp.concatenate([row[:, None] * inv, col[:, None] * inv], axis=-1)
    cos_ax = jnp.concatenate([jnp.ones((n_ctx, HEAD_DIM // 2), F32), jnp.cos(ang)], axis=0)
    sin_ax = jnp.concatenate([jnp.zeros((n_ctx, HEAD_DIM // 2), F32), jnp.sin(ang)], axis=0)
    n_freq = HEAD_DIM // 2
    inv = ROPE_BASE ** (-jnp.arange(n_freq, dtype=F32) / n_freq)
    pos = jnp.arange(n_ctx + s, dtype=F32)
    ang = pos[:, None] * inv
    return widen(cos_ax, sin_ax) + widen(jnp.cos(ang), jnp.sin(ang))


def _block_diag_ones(n):
    g = jnp.arange(n) // HEAD_DIM
    return (g[:, None] == g[None, :]).astype(BF16)


def kernel(x, c, ctx, c_ctx, w_mod, b_mod, g_norm1, g_norm2, w_in, mlp_g_v, mlp_w_s, mlp_b_s, win_q_norm,
           win_k_norm, win_sink, glb_q_norm, glb_k_norm, ret_decay, ret_gn, w_branch, w_out, ffn_w1, ffn_w3,
           ffn_w2, router_w, router_b, moe_w1, moe_w3, moe_w2):
    batch, s, d = x.shape
    n_ctx = ctx.shape[1]
    depth = w_in.shape[0]
    assert batch + 1 <= 8 and n_ctx % BLOCK == 0 and s % BLOCK == 0

    xc = jnp.concatenate([ctx, x], axis=1)
    cvec = jnp.zeros((8, d), F32).at[:batch].set(c).at[batch].set(c_ctx)
    mod = _modulation(cvec, w_mod, b_mod)
    mod = mod.reshape(depth, 8, 6, d)

    tabs = _rope_tables(n_ctx, s)
    bd256 = _block_diag_ones(256)
    ones = lambda n: jnp.ones((n,), F32)
    zeros = lambda n: jnp.zeros((n,), F32)
    scale = HEAD_DIM ** -0.5

    for layer in range(depth):
        ml = mod[layer]
        modl1 = ml[:batch, 0:2]
        modc1 = ml[batch, 0:2]
        g1_l = ml[:batch, 2:3]
        g1_c = ml[batch, 2:3]
        modl2 = ml[:batch, 3:6]
        modc2 = ml[batch, 3:6]

        def qk_gain(gq, gk, q_scale):
            return jnp.concatenate([jnp.tile(gq, 4) * q_scale, jnp.tile(gk, 2)])

        gain = jnp.stack([
            jnp.concatenate([mlp_g_v[layer].reshape(-1), ones(128)]),
            qk_gain(win_q_norm[layer], win_k_norm[layer], scale),
            qk_gain(glb_q_norm[layer], glb_k_norm[layer], scale * LOG2_E)])
        post = jnp.concatenate([ones(256), jnp.full((256,), scale, F32)])[None, :]

        p0, kv, gates = _inproj(xc, modl1, modc1, g_norm1[layer][None, :], w_in[layer].astype(BF16),
                                (gain, post, bd256), tabs, n_ctx)
        b_att = _window_attention(p0, kv, win_sink[layer], n_ctx)
        c_att = _global_attention(p0, kv, n_ctx)
        dec = jnp.broadcast_to(ret_decay[layer].reshape(-1, 1), (8, 128))
        o_f, o_b = _retention(p0, dec, n_ctx)
        bias = jnp.repeat(mlp_b_s[layer].T, HEAD_DIM, axis=1)
        xc = _merge(xc, p0, b_att, c_att, o_f, o_b, gates, g1_l, g1_c, mlp_w_s[layer].astype(BF16), bias,
                    ret_gn[layer].reshape(1, -1), bd256, w_branch[layer].astype(BF16),
                    w_out[layer].astype(BF16), n_ctx)
        j = layer // 2
        g2 = g_norm2[layer][None, :]
        if layer % 2 == 0:
            xc = _ffn(xc, modl2, modc2, g2, ffn_w1[j].astype(BF16), ffn_w3[j].astype(BF16),
                      ffn_w2[j].astype(BF16), n_ctx)
        else:
            rw_pad = jnp.zeros((d, 128), F32).at[:, :N_EXPERTS].set(router_w[j])
            rb_pad = jnp.zeros((1, 128), F32).at[0, :N_EXPERTS].set(router_b[j])
            xc = _moe(xc, modl2, modc2, g2, rw_pad, rb_pad, moe_w1, moe_w3, moe_w2, j, n_ctx,
                      latent_only=(layer == depth - 1))
    return xc if depth % 2 == 0 else xc[:, n_ctx:]
```

```python
import functools

import jax
import jax.numpy as jnp
from jax import lax
from jax.experimental import pallas as pl
from jax.experimental.pallas import tpu as pltpu

F32 = jnp.float32
BF16 = jnp.bfloat16

HEAD_DIM = 64
BLOCK = 128
GRID_W = 64
ROPE_BASE = 10000.0
N_BRANCH = 4
BRANCH_W = 256
N_EXPERTS = 8
EPS = 1e-6
NEG_INF = -1e30
LOG2_E = 1.4426950408889634
IN_TILE = 512
N_MIX_TILES = 5
VMEM_LIMIT = 56 * 1024 * 1024


def _cparams(sem):
    return pltpu.CompilerParams(dimension_semantics=sem, vmem_limit_bytes=VMEM_LIMIT)


def _dot(a, b):
    return jnp.dot(a, b, preferred_element_type=F32)


def _dot_nt(a, b):
    return lax.dot_general(a, b, (((1,), (1,)), ((), ())), preferred_element_type=F32)


def _split_dot(a, b_bf16):
    hi = a.astype(BF16)
    lo = (a - hi.astype(F32)).astype(BF16)
    return _dot(hi, b_bf16) + _dot(lo, b_bf16)


def _pick_tile(n, candidates):
    for c in candidates:
        if n % c == 0:
            return c
    raise ValueError(f"no tile for {n} in {candidates}")


def _modulated_norm(x, g, sh_l, sc_l, sh_c, sc_c, row0, n_ctx):
    tm = x.shape[0]
    ms = jnp.mean(x * x, axis=-1, keepdims=True)
    y = x * lax.rsqrt(ms + EPS) * g
    row = row0 + lax.broadcasted_iota(jnp.int32, (tm, 1), 0)
    is_ctx = row < n_ctx
    sh = jnp.where(is_ctx, sh_c, sh_l)
    sc = jnp.where(is_ctx, sc_c, sc_l)
    return y * (1.0 + sc) + sh


def _mod_kernel(c_ref, w_ref, b_ref, o_ref):
    c = c_ref[...]
    s = c * jax.nn.sigmoid(c)
    s_hi = s.astype(BF16)
    s_lo = (s - s_hi.astype(F32)).astype(BF16)
    w = w_ref[0]
    w_hi = w.astype(BF16)
    w_lo = (w - w_hi.astype(F32)).astype(BF16)
    o_ref[0] = _dot(s_hi, w_hi) + _dot(s_hi, w_lo) + _dot(s_lo, w_hi) + b_ref[0]


def _modulation(cvec, w_mod, b_mod):
    depth, d, n = w_mod.shape
    tn = _pick_tile(n, (1536, 1024, 512, 128))
    return pl.pallas_call(
        _mod_kernel,
        grid=(depth, n // tn),
        in_specs=[
            pl.BlockSpec((8, d), lambda l, j: (0, 0)),
            pl.BlockSpec((1, d, tn), lambda l, j: (l, 0, j)),
            pl.BlockSpec((1, 1, tn), lambda l, j: (l, 0, j)),
        ],
        out_specs=pl.BlockSpec((1, 8, tn), lambda l, j: (l, 0, j)),
        out_shape=jax.ShapeDtypeStruct((depth, 8, n), F32),
        compiler_params=_cparams(("parallel", "parallel")),
        name="modulation",
    )(cvec, w_mod, b_mod.reshape(depth, 1, n))


def _group_rms(v, bd, gain):
    sq = (v * v).astype(BF16)
    half = bd.shape[0]
    parts = []
    for c in range(0, v.shape[-1], half):
        wd = min(half, v.shape[-1] - c)
        parts.append(_dot(sq[:, c:c + wd], bd[:wd, :wd]))
    gs = parts[0] if len(parts) == 1 else jnp.concatenate(parts, axis=1)
    return v * lax.rsqrt(gs * (1.0 / HEAD_DIM) + EPS) * gain


def _rope(v, cos, sin):
    w = v.shape[-1]
    lane = lax.broadcasted_iota(jnp.int32, v.shape, 1)
    first = (lane & 32) == 0
    partner = jnp.where(first, pltpu.roll(v, w - 32, 1), pltpu.roll(v, 32, 1))
    return v * cos + partner * sin


def _tile(t, n):
    return jnp.concatenate([t] * n, axis=1)


def _inproj_kernel(x_ref, modl_ref, modc_ref, g_ref, w_ref, gain_ref, post_ref, bd_ref,
                   cax_ref, sax_ref, csq_ref, ssq_ref, p_ref, kv_ref, gate_ref, *, tm, n_ctx, n_tiles):
    i = pl.program_id(1)
    h = _modulated_norm(x_ref[0], g_ref[...], modl_ref[0, 0:1, :], modl_ref[0, 1:2, :],
                        modc_ref[0:1, :], modc_ref[1:2, :], i * tm, n_ctx).astype(BF16)
    lane = lax.broadcasted_iota(jnp.int32, (tm, 128), 1)
    lo = lane < HEAD_DIM

    def attn_tile(acc, j, duplicate):
        c0 = j * IN_TILE
        qk = _group_rms(acc[:, :384], bd_ref[...], gain_ref[j:j + 1, :])
        qk = _rope(qk, _tile(cax_ref[...], 3), _tile(sax_ref[...], 3))
        vv = acc[:, 384:512]
        p_ref[0, :, c0:c0 + 384] = qk.astype(BF16)
        p_ref[0, :, c0 + 384:c0 + 512] = vv.astype(BF16)
        k = qk[:, 256:384]
        kr = pltpu.roll(k, HEAD_DIM, 1)
        vr = pltpu.roll(vv, HEAD_DIM, 1)
        if duplicate:
            parts = [jnp.where(lo, k, kr), jnp.where(lo, kr, k), jnp.where(lo, vv, vr), jnp.where(lo, vr, vv)]
        else:
            pad = jnp.where(lane == HEAD_DIM, 1.0, 0.0)
            parts = [jnp.where(lo, k, pad), jnp.where(lo, kr, pad), jnp.where(lo, vv, pad), jnp.where(lo, vr, pad)]
        for n, part in enumerate(parts):
            c = (j - 1) * IN_TILE + n * 128
            kv_ref[0, :, c:c + 128] = part.astype(BF16)

    project = lambda j: _dot(h, w_ref[:, j * IN_TILE:(j + 1) * IN_TILE])
    acc_next = project(0)
    for j in range(n_tiles):
        c0 = j * IN_TILE
        acc = acc_next
        if j + 1 < n_tiles:
            acc_next = project(j + 1)
        if j == 0:
            v = jax.nn.gelu(acc)
            p_ref[0, :, 0:256] = v[:, :256].astype(BF16)
            p_ref[0, :, 256:512] = _group_rms(v[:, 256:], bd_ref[...], gain_ref[0:1, :256]).astype(BF16)
        elif j == 1:
            attn_tile(acc, 1, True)
        elif j == 2:
            attn_tile(acc, 2, False)
        elif j == 3:
            v = _rope(acc, _tile(csq_ref[...], 4), _tile(ssq_ref[...], 4)) * post_ref[...]
            p_ref[0, :, c0:c0 + IN_TILE] = v.astype(BF16)
        elif j < N_MIX_TILES:
            p_ref[0, :, c0:c0 + IN_TILE] = acc.astype(BF16)
        else:
            g0 = c0 - N_MIX_TILES * IN_TILE
            gate_ref[0, :, g0:g0 + IN_TILE] = (0.5 * acc).astype(BF16)


def _inproj(xc, modl, modc, g, w_bf, ep, tabs, n_ctx):
    b, t, d = xc.shape
    n_in = w_bf.shape[1]
    tm = _pick_tile(t, (384, 256, 128))
    gain, post, bd = ep
    cax, sax, csq, ssq = tabs
    n_gate = n_in - N_MIX_TILES * IN_TILE
    tok = lambda bb, i: (i, 0)
    full2 = lambda bb, i: (0, 0)
    out = lambda w: pl.BlockSpec((1, tm, w), lambda bb, i: (bb, i, 0))
    return pl.pallas_call(
        functools.partial(_inproj_kernel, tm=tm, n_ctx=n_ctx, n_tiles=n_in // IN_TILE),
        grid=(b, t // tm),
        in_specs=[
            pl.BlockSpec((1, tm, d), lambda bb, i: (bb, i, 0)),
            pl.BlockSpec((1, 2, d), lambda bb, i: (bb, 0, 0)),
            pl.BlockSpec((2, d), full2),
            pl.BlockSpec((1, d), full2),
            pl.BlockSpec((d, n_in), full2),
            pl.BlockSpec((3, 384), full2),
            pl.BlockSpec((1, IN_TILE), full2),
            pl.BlockSpec((256, 256), full2),
            pl.BlockSpec((tm, 128), tok),
            pl.BlockSpec((tm, 128), tok),
            pl.BlockSpec((tm, 128), tok),
            pl.BlockSpec((tm, 128), tok),
        ],
        out_specs=[out(N_MIX_TILES * IN_TILE), out(2 * IN_TILE), out(n_gate)],
        out_shape=[
            jax.ShapeDtypeStruct((b, t, N_MIX_TILES * IN_TILE), BF16),
            jax.ShapeDtypeStruct((b, t, 2 * IN_TILE), BF16),
            jax.ShapeDtypeStruct((b, t, n_gate), BF16),
        ],
        compiler_params=_cparams(("parallel", "parallel")),
        name="inproj",
    )(xc, modl, modc, g, w_bf, gain, post, bd, cax, sax, csq, ssq)


def _stack_heads(qa):
    qf = qa.astype(F32)
    lo = lax.broadcasted_iota(jnp.int32, qf.shape, 1) < HEAD_DIM
    return jnp.concatenate([jnp.where(lo, qf, 0.0), jnp.where(lo, 0.0, qf)], axis=0).astype(BF16)


def _unstack_heads(o2):
    n = o2.shape[0] // 2
    lo = lax.broadcasted_iota(jnp.int32, (n, o2.shape[1]), 1) < HEAD_DIM
    return jnp.where(lo, o2[:n], o2[n:])


def _win_kernel(sink_ref, q_ref, kc_ref, vc_ref, kp_ref, kx_ref, kn_ref, vp_ref, vx_ref, vn_ref, o_ref,
                *, n_ctx_blocks, n_blocks, per_step):
    j = pl.program_id(1)
    lc = n_ctx_blocks * BLOCK
    w = lc + 3 * BLOCK
    row = lax.broadcasted_iota(jnp.int32, (BLOCK, w), 0)
    col = lax.broadcasted_iota(jnp.int32, (BLOCK, w), 1)
    c = col - lc
    rid = lax.broadcasted_iota(jnp.int32, (2 * BLOCK, 1), 0)
    blocks = ([(kp_ref, vp_ref, slice(None))]
              + [(kx_ref, vx_ref, slice(u * BLOCK, (u + 1) * BLOCK)) for u in range(per_step)]
              + [(kn_ref, vn_ref, slice(None))])
    chains = [(sub, a) for sub in range(per_step) for a in range(2)]
    lanes = lambda a: slice(128 * a, 128 * (a + 1))
    scores, values = {}, {}
    for sub, a in chains:
        q2 = _stack_heads(q_ref[0, sub * BLOCK:(sub + 1) * BLOCK, lanes(a)])
        local = blocks[sub:sub + 3]
        keys = jnp.concatenate([kc_ref[0, :, lanes(a)]] + [kr[0, rs, lanes(a)] for kr, _, rs in local], axis=0)
        values[sub, a] = jnp.concatenate([vc_ref[0, :, lanes(a)]] + [vr[0, rs, lanes(a)] for _, vr, rs in local],
                                         axis=0)
        scores[sub, a] = _dot_nt(q2, keys)
    probs = {}
    for sub in range(per_step):
        n = per_step * j + sub
        is_lat = n >= n_ctx_blocks
        prev_ok = n > n_ctx_blocks
        next_ok = n < n_blocks - 1
        lo_b = jnp.where(prev_ok, row, BLOCK)
        hi_b = jnp.where(next_ok, row + 2 * BLOCK, 2 * BLOCK - 1)
        lat_bias = jnp.where(is_lat, 0.0, NEG_INF)
        bias = jnp.where(col < lc, 0.0, jnp.where(c >= lo_b, jnp.where(c <= hi_b, lat_bias, NEG_INF), NEG_INF))
        bias2 = jnp.concatenate([bias, bias], axis=0)
        for a in range(2):
            s = scores[sub, a] + bias2
            sk = jnp.where(rid < BLOCK, sink_ref[2 * a], sink_ref[2 * a + 1])
            m = jnp.maximum(jnp.max(s, axis=-1, keepdims=True), sk)
            e = jnp.exp(s - m)
            den = jnp.sum(e, axis=-1, keepdims=True) + jnp.exp(sk - m)
            probs[sub, a] = (e / den).astype(BF16)
    outs = {c: _unstack_heads(_dot(probs[c], values[c])) for c in chains}
    for sub in range(per_step):
        o_ref[0, sub * BLOCK:(sub + 1) * BLOCK, :] = jnp.concatenate([outs[sub, 0], outs[sub, 1]],
                                                                      axis=1).astype(BF16)


def _window_attention(p0, kv, sink, n_ctx):
    b, t, _ = p0.shape
    nb = t // BLOCK
    ncb = n_ctx // BLOCK
    ps = _pick_tile(nb, (6, 5, 4, 3, 2, 1))
    span = lambda cidx: pl.BlockSpec((1, ps * BLOCK, 256), lambda bb, j: (bb, j, cidx))
    prv = lambda cidx: pl.BlockSpec((1, BLOCK, 256), lambda bb, j: (bb, jnp.maximum(ps * j - 1, 0), cidx))
    nxt = lambda cidx: pl.BlockSpec((1, BLOCK, 256), lambda bb, j: (bb, jnp.minimum(ps * j + ps, nb - 1), cidx))
    ctx = lambda cidx: pl.BlockSpec((1, n_ctx, 256), lambda bb, j: (bb, 0, cidx))
    return pl.pallas_call(
        functools.partial(_win_kernel, n_ctx_blocks=ncb, n_blocks=nb, per_step=ps),
        grid=(b, nb // ps),
        in_specs=[
            pl.BlockSpec(memory_space=pltpu.SMEM),
            span(2),
            ctx(0), ctx(1),
            prv(0), span(0), nxt(0),
            prv(1), span(1), nxt(1),
        ],
        out_specs=pl.BlockSpec((1, ps * BLOCK, 256), lambda bb, j: (bb, j, 0)),
        out_shape=jax.ShapeDtypeStruct((b, t, 256), BF16),
        compiler_params=_cparams(("parallel", "parallel")),
        name="window_attn",
    )(sink, p0, kv, kv, kv, kv, kv, kv, kv, kv)


SAFE_LOG2_BOUND = 60.0


def _glb_kernel(q_ref, k_ref, v_ref, o_ref, kmax_scr, *, tq, tk, n_ctx, t_all):
    i = pl.program_id(1)
    sls = (slice(0, 128), slice(128, 256))
    lane = lax.broadcasted_iota(jnp.int32, (2 * tq, 128), 1)

    @pl.when(i == 0)
    def _():
        klane = lax.broadcasted_iota(jnp.int32, (tk, 128), 1)
        for a in range(2):
            def kbody(ci, mx):
                start = pl.multiple_of(ci * tk, tk)
                k = k_ref[0, pl.ds(start, tk), sls[a]].astype(F32)
                sq = jnp.sum(jnp.where(klane < HEAD_DIM, k * k, 0.0), axis=-1, keepdims=True)
                return jnp.maximum(mx, jnp.max(sq, axis=0, keepdims=True))
            mx = lax.fori_loop(0, t_all // tk, kbody, jnp.zeros((1, 1), F32))
            kmax_scr[a:a + 1, :] = jnp.broadcast_to(jnp.sqrt(mx), (1, 128))

    q2s, bnds = [], []
    for a in range(2):
        q = q_ref[0, :, sls[a]].astype(F32)
        q2 = jnp.concatenate([q, pltpu.roll(q, HEAD_DIM, 1)], axis=0)
        q2 = jnp.where(lane < HEAD_DIM, q2, 0.0)
        nrm = jnp.sqrt(jnp.sum(q2 * q2, axis=-1, keepdims=True))
        bnd = nrm * kmax_scr[a:a + 1, 0:1] * 1.01
        q2s.append(q2)
        bnds.append(bnd)
    worst = jnp.max(jnp.maximum(bnds[0], bnds[1]))

    def finish(res):
        outs = []
        for acc in res:
            den = jnp.sum(jnp.where(lane == HEAD_DIM, acc, 0.0), axis=-1, keepdims=True)
            o = acc / den
            lo_t = lax.broadcasted_iota(jnp.int32, (tq, 128), 1) < HEAD_DIM
            outs.append(jnp.where(lo_t, o[:tq], pltpu.roll(o, HEAD_DIM, 1)[tq:]))
        o_ref[0] = jnp.concatenate(outs, axis=1).astype(BF16)

    def attend_fixed(n_chunks, size):
        qb = [jnp.where(lane == HEAD_DIM, -bnds[a], q2s[a]).astype(BF16) for a in range(2)]

        def scores(ci):
            return [_dot_nt(qb[a], k_ref[0, ci * size:(ci + 1) * size, sls[a]]) for a in range(2)]

        accs = [jnp.zeros((2 * tq, 128), F32)] * 2
        s_next = scores(0)
        for ci in range(n_chunks):
            s_cur = s_next
            if ci + 1 < n_chunks:
                s_next = scores(ci + 1)
            ps = [jnp.exp2(s_cur[a]).astype(BF16) for a in range(2)]
            accs = [accs[a] + _dot(ps[a], v_ref[0, ci * size:(ci + 1) * size, sls[a]]) for a in range(2)]
        finish(accs)

    def attend_online(n_chunks, size):
        qb = [q2s[a].astype(BF16) for a in range(2)]

        def body(ci, carry):
            start = pl.multiple_of(ci * size, size)
            out = []
            for a in range(2):
                m, acc = carry[a]
                k = k_ref[0, pl.ds(start, size), sls[a]]
                v = v_ref[0, pl.ds(start, size), sls[a]]
                s = _dot_nt(qb[a], k)
                m_new = jnp.maximum(m, jnp.max(s, axis=-1, keepdims=True))
                p = jnp.exp2(s - m_new)
                acc = jnp.exp2(m - m_new) * acc + _dot(p.astype(BF16), v)
                out.append((m_new, acc))
            return tuple(out)

        init = (jnp.full((2 * tq, 1), NEG_INF, F32), jnp.zeros((2 * tq, 128), F32))
        res = lax.fori_loop(0, n_chunks, body, (init, init))
        finish([acc for (_, acc) in res])

    is_ctx = (i + 1) * tq <= n_ctx
    safe = worst <= SAFE_LOG2_BOUND

    @pl.when(jnp.logical_and(is_ctx, safe))
    def _():
        attend_fixed(1, n_ctx)

    @pl.when(jnp.logical_and(is_ctx, jnp.logical_not(safe)))
    def _():
        attend_online(1, n_ctx)

    @pl.when(jnp.logical_and(jnp.logical_not(is_ctx), safe))
    def _():
        attend_fixed(t_all // tk, tk)

    @pl.when(jnp.logical_and(jnp.logical_not(is_ctx), jnp.logical_not(safe)))
    def _():
        attend_online(t_all // tk, tk)


def _global_attention(p0, kv, n_ctx):
    b, t, _ = p0.shape
    tq = _pick_tile(n_ctx, (256, 128))
    tk = _pick_tile(t, (1408, 768, 384, 256, 128))
    return pl.pallas_call(
        functools.partial(_glb_kernel, tq=tq, tk=tk, n_ctx=n_ctx, t_all=t),
        grid=(b, t // tq),
        in_specs=[
            pl.BlockSpec((1, tq, 256), lambda bb, i: (bb, i, 4)),
            pl.BlockSpec((1, t, 256), lambda bb, i: (bb, 0, 2)),
            pl.BlockSpec((1, t, 256), lambda bb, i: (bb, 0, 3)),
        ],
        out_specs=pl.BlockSpec((1, tq, 256), lambda bb, i: (bb, i, 0)),
        out_shape=jax.ShapeDtypeStruct((b, t, 256), BF16),
        scratch_shapes=[pltpu.VMEM((8, 128), F32)],
        compiler_params=_cparams(("arbitrary", "arbitrary")),
        name="global_attn",
    )(p0, kv, kv)


_TAB_D, _TAB_Q, _TAB_K, _TAB_C, _TAB_ROWS = 0, 256, 384, 512, 640


def _ret_kernel(dec_ref, qf_ref, kf_ref, vf_ref, qb_ref, kb_ref, vb_ref, of_ref, ob_ref, st_scr, tab_scr,
                *, batch):
    i = pl.program_id(0)
    r = lax.broadcasted_iota(jnp.int32, (BLOCK, BLOCK), 0)
    c = lax.broadcasted_iota(jnp.int32, (BLOCK, BLOCK), 1)
    lane_lo = c < HEAD_DIM
    same_head = (r < HEAD_DIM) == lane_lo

    @pl.when(i == 0)
    def _():
        st_scr[...] = jnp.zeros_like(st_scr)
        x = dec_ref[...]
        lg = jnp.where(x >= 0.0, -jnp.log(1.0 + jnp.exp(-x)), x - jnp.log(1.0 + jnp.exp(x)))
        rf = r.astype(F32)
        cf = c.astype(F32)
        for d in range(2):
            for pair in range(2):
                le = lg[4 * d + 2 * pair:4 * d + 2 * pair + 1, :]
                lo_ = lg[4 * d + 2 * pair + 1:4 * d + 2 * pair + 2, :]
                lp = jnp.where(lane_lo[0:1, :], le, lo_)
                if d == 0:
                    rel = rf - cf
                    qpow = rf + 1.0
                    kpow = (BLOCK - 1.0) - rf
                else:
                    rel = cf - rf - 1.0
                    qpow = (BLOCK - 1.0) - rf
                    kpow = rf
                msk = rel >= 0.0
                relc = jnp.where(msk, rel, 0.0)
                tab_scr[d, pair, _TAB_D:_TAB_D + BLOCK, :] = jnp.where(msk, jnp.exp(le * relc), 0.0)
                tab_scr[d, pair, _TAB_D + BLOCK:_TAB_Q, :] = jnp.where(msk, jnp.exp(lo_ * relc), 0.0)
                tab_scr[d, pair, _TAB_Q:_TAB_K, :] = jnp.exp(lp * qpow)
                tab_scr[d, pair, _TAB_K:_TAB_C, :] = jnp.exp(lp * kpow)
                tab_scr[d, pair, _TAB_C:_TAB_ROWS, :] = jnp.where(
                    r < HEAD_DIM, jnp.exp(le * float(BLOCK)), jnp.exp(lo_ * float(BLOCK)))

    refs = ((qf_ref, kf_ref, vf_ref, of_ref), (qb_ref, kb_ref, vb_ref, ob_ref))
    chains = [(d, bb, pair) for d in range(2) for bb in range(batch) for pair in range(2)]
    tab = lambda d, pair, r0, r1: tab_scr[d, pair, r0:r1, :]
    q, k, v, s = {}, {}, {}, {}
    for c in chains:
        d, bb, pair = c
        sl = slice(128 * pair, 128 * (pair + 1))
        q[c] = refs[d][0][bb, :, sl].astype(F32)
        k[c] = refs[d][1][bb, :, sl].astype(F32)
        v[c] = refs[d][2][bb, :, sl]
        s[c] = st_scr[d, bb, pair]
    qk = {}
    for c in chains:
        q2 = jnp.concatenate([jnp.where(lane_lo, q[c], 0.0), jnp.where(lane_lo, 0.0, q[c])], axis=0)
        qk[c] = _dot_nt(q2.astype(BF16), k[c].astype(BF16))
    upd = {c: _dot((k[c] * tab(c[0], c[2], _TAB_K, _TAB_C)).T.astype(BF16), v[c]) for c in chains}
    cross = {c: _dot((q[c] * tab(c[0], c[2], _TAB_Q, _TAB_K)).astype(BF16), s[c].astype(BF16)) for c in chains}
    intra = {c: _dot((qk[c] * tab(c[0], c[2], _TAB_D, _TAB_Q)).astype(BF16), v[c]) for c in chains}
    for c in chains:
        d, bb, pair = c
        sl = slice(128 * pair, 128 * (pair + 1))
        refs[d][3][bb, :, sl] = _unstack_heads(intra[c]) + cross[c]
        st_scr[d, bb, pair] = jnp.where(same_head, tab(d, pair, _TAB_C, _TAB_ROWS) * s[c] + upd[c], 0.0)


def _retention(p0, dec, n_ctx):
    b, t, _ = p0.shape
    nb = t // BLOCK
    ncb = n_ctx // BLOCK
    bwd = lambda i: jnp.where(i < ncb, ncb - 1 - i, nb - 1 + ncb - i)
    fspec = lambda cidx: pl.BlockSpec((b, BLOCK, 256), lambda i: (0, i, cidx))
    bspec = lambda cidx: pl.BlockSpec((b, BLOCK, 256), lambda i: (0, bwd(i), cidx))
    return pl.pallas_call(
        functools.partial(_ret_kernel, batch=b),
        grid=(nb,),
        in_specs=[
            pl.BlockSpec((8, 128), lambda i: (0, 0)),
            fspec(6), fspec(7), fspec(8),
            bspec(6), bspec(7), bspec(8),
        ],
        out_specs=[
            pl.BlockSpec((b, BLOCK, 256), lambda i: (0, i, 0)),
            pl.BlockSpec((b, BLOCK, 256), lambda i: (0, bwd(i), 0)),
        ],
        out_shape=[jax.ShapeDtypeStruct((b, t, 256), F32), jax.ShapeDtypeStruct((b, t, 256), F32)],
        scratch_shapes=[pltpu.VMEM((2, b, 2, BLOCK, BLOCK), F32),
                        pltpu.VMEM((2, 2, _TAB_ROWS, BLOCK), F32)],
        compiler_params=_cparams(("arbitrary",)),
        name="retention",
    )(dec, p0, p0, p0, p0, p0, p0)


def _route(h, rw_ref, rb_ref):
    lane = lax.broadcasted_iota(jnp.int32, (h.shape[0], 128), 1)
    h_hi = h.astype(BF16)
    h_lo = (h - h_hi.astype(F32)).astype(BF16)
    rw = rw_ref[...]
    rw_hi = rw.astype(BF16)
    rw_lo = (rw - rw_hi.astype(F32)).astype(BF16)
    logits = _dot(h_hi, rw_hi) + _dot(h_hi, rw_lo) + _dot(h_lo, rw_hi) + rb_ref[...]
    logits = jnp.where(lane < N_EXPERTS, logits, NEG_INF)
    v0 = jnp.max(logits, axis=-1, keepdims=True)
    i0 = jnp.min(jnp.where(logits == v0, lane, 128), axis=-1, keepdims=True)
    rest = jnp.where(lane == i0, NEG_INF, logits)
    v1 = jnp.max(rest, axis=-1, keepdims=True)
    i1 = jnp.min(jnp.where(rest == v1, lane, 128), axis=-1, keepdims=True)
    e1 = jnp.exp(v1 - v0)
    p0 = 1.0 / (1.0 + e1)
    ids = jnp.where(lane == 0, i0, jnp.where(lane == 1, i1, 0))
    return ids, jnp.where(lane == 0, p0, jnp.where(lane == 1, e1 * p0, 0.0))


def _merge_kernel(x_ref, uv_ref, batt_ref, catt_ref, of_ref, ob_ref, rg_ref, gate_ref, gl_ref, gc_ref,
                  ws_ref, bs_ref, gn_ref, bd_ref, wb_ref, wo_ref, *rest, tm, n_ctx, with_router):
    if with_router:
        modl_ref, modc_ref, g2_ref, rw_ref, rb_ref, o_ref, h_ref, ids_ref, p_ref = rest
    else:
        (o_ref,) = rest
    i = pl.program_id(1)
    grp = jnp.right_shift(lax.broadcasted_iota(jnp.int32, (BLOCK, 256), 1), 6)
    a_chunks = []
    for ch in range(tm // BLOCK):
        rows = slice(ch * BLOCK, (ch + 1) * BLOCK)
        vn = uv_ref[0, rows, 256:512]
        mixed = bs_ref[...]
        for g in range(4):
            mixed = mixed + jnp.where(grp == g, _dot(ws_ref[g], vn), 0.0)
        a_chunks.append((uv_ref[0, rows, 0:256].astype(F32) * mixed).astype(BF16))
    a_br = jnp.concatenate(a_chunks, axis=0)
    o = of_ref[0] + ob_ref[0]
    mean = _split_dot(o, bd_ref[...]) * (1.0 / HEAD_DIM)
    oc = o - mean
    var = _split_dot(oc * oc, bd_ref[...]) * (1.0 / HEAD_DIM)
    rg = rg_ref[0].astype(F32)
    d_br = (oc * lax.rsqrt(var + EPS) * gn_ref[...] * (rg * jax.nn.sigmoid(rg))).astype(BF16)
    branches = (a_br, batt_ref[0], catt_ref[0], d_br)
    d_model = x_ref.shape[-1]
    acc = None
    ys = [_dot(branches[n], wb_ref[n]) for n in range(N_BRANCH)]
    for n in range(N_BRANCH):
        th = jnp.tanh(gate_ref[0, :, n * d_model:(n + 1) * d_model].astype(F32))
        y = ys[n]
        term = y + th * y
        acc = term if acc is None else acc + term
    m = _dot((0.5 * acc).astype(BF16), wo_ref[...])
    row = i * tm + lax.broadcasted_iota(jnp.int32, (tm, 1), 0)
    g1 = jnp.where(row < n_ctx, gc_ref[...], gl_ref[0])
    x_new = x_ref[0] + g1 * m
    o_ref[0] = x_new
    if with_router:
        h = _modulated_norm(x_new, g2_ref[...], modl_ref[0, 0:1, :], modl_ref[0, 1:2, :],
                            modc_ref[0:1, :], modc_ref[1:2, :], i * tm, n_ctx)
        h_ref[0] = h
        ids_ref[0], p_ref[0] = _route(h, rw_ref, rb_ref)


def _merge(xc, p0, b_att, c_att, o_f, o_b, gates, g1_l, g1_c, ws_bf, bias, gn, bd, wb_bf, wo_bf, n_ctx,
           router=None):
    b, t, d = xc.shape
    tm = _pick_tile(t, (384, 256, 128))
    tok = lambda w, cidx: pl.BlockSpec((1, tm, w), lambda bb, i: (bb, i, cidx))
    full = lambda shape: pl.BlockSpec(shape, lambda bb, i: (0,) * len(shape))
    in_specs = [
        tok(d, 0),
        tok(512, 0),
        tok(256, 0), tok(256, 0),
        tok(256, 0), tok(256, 0),
        tok(256, 9),
        tok(N_BRANCH * d, 0),
        pl.BlockSpec((1, 1, d), lambda bb, i: (bb, 0, 0)),
        full((1, d)),
        full((4, BLOCK, BLOCK)), full((BLOCK, 256)), full((1, 256)), full((256, 256)),
        full((N_BRANCH, BRANCH_W, d)), full((d, d)),
    ]
    args = [xc, p0, b_att, c_att, o_f, o_b, p0, gates, g1_l, g1_c, ws_bf, bias, gn, bd, wb_bf, wo_bf]
    out_specs = [tok(d, 0)]
    out_shape = [jax.ShapeDtypeStruct((b, t, d), F32)]
    if router is not None:
        in_specs += [pl.BlockSpec((1, 3, d), lambda bb, i: (bb, 0, 0)), full((3, d)), full((1, d)),
                     full((d, 128)), full((1, 128))]
        args += list(router)
        out_specs += [tok(d, 0), tok(128, 0), tok(128, 0)]
        out_shape += [jax.ShapeDtypeStruct((b, t, d), F32), jax.ShapeDtypeStruct((b, t, 128), jnp.int32),
                      jax.ShapeDtypeStruct((b, t, 128), F32)]
    out = pl.pallas_call(
        functools.partial(_merge_kernel, tm=tm, n_ctx=n_ctx, with_router=router is not None),
        grid=(b, t // tm),
        in_specs=in_specs,
        out_specs=out_specs,
        out_shape=out_shape,
        compiler_params=_cparams(("parallel", "parallel")),
        name="merge",
    )(*args)
    return out if router is not None else out[0]


FF_CHUNK = 512


def _swiglu_chunks(h, w1_ref, w3_ref, w2_ref, lead):
    tf = w1_ref.shape[-1]
    bounds = [(c0, min(c0 + FF_CHUNK, tf)) for c0 in range(0, tf, FF_CHUNK)]

    def up(c0, c1):
        return (_dot(h, w1_ref[lead + (slice(None), slice(c0, c1))].astype(BF16)),
                _dot(h, w3_ref[lead + (slice(None), slice(c0, c1))].astype(BF16)))

    out = None
    nxt = up(*bounds[0])
    for n, (c0, c1) in enumerate(bounds):
        a, bgate = nxt
        if n + 1 < len(bounds):
            nxt = up(*bounds[n + 1])
        act = (a * jax.nn.sigmoid(a) * bgate).astype(BF16)
        part = _dot(act, w2_ref[lead + (slice(c0, c1), slice(None))].astype(BF16))
        out = part if out is None else out + part
    return out


def _ffn_kernel(x_ref, modl_ref, modc_ref, g_ref, w1_ref, w3_ref, w2_ref, o_ref, *, tm, n_ctx):
    i = pl.program_id(1)
    x = x_ref[0]
    h = _modulated_norm(x, g_ref[...], modl_ref[0, 0:1, :], modl_ref[0, 1:2, :],
                        modc_ref[0:1, :], modc_ref[1:2, :], i * tm, n_ctx).astype(BF16)
    y = _swiglu_chunks(h, w1_ref, w3_ref, w2_ref, ())
    row = i * tm + lax.broadcasted_iota(jnp.int32, (tm, 1), 0)
    g2 = jnp.where(row < n_ctx, modc_ref[2:3, :], modl_ref[0, 2:3, :])
    o_ref[0] = x + g2 * y


def _ffn(xc, modl, modc, g, w1, w3, w2, n_ctx):
    b, t, d = xc.shape
    dff = w1.shape[1]
    tm = _pick_tile(t, (384, 256, 128))
    full = lambda bb, i: (0, 0)
    return pl.pallas_call(
        functools.partial(_ffn_kernel, tm=tm, n_ctx=n_ctx),
        grid=(b, t // tm),
        in_specs=[
            pl.BlockSpec((1, tm, d), lambda bb, i: (bb, i, 0)),
            pl.BlockSpec((1, 3, d), lambda bb, i: (bb, 0, 0)),
            pl.BlockSpec((3, d), full),
            pl.BlockSpec((1, d), full),
            pl.BlockSpec((d, dff), full),
            pl.BlockSpec((d, dff), full),
            pl.BlockSpec((dff, d), full),
        ],
        out_specs=pl.BlockSpec((1, tm, d), lambda bb, i: (bb, i, 0)),
        out_shape=jax.ShapeDtypeStruct((b, t, d), F32),
        compiler_params=_cparams(("parallel", "parallel")),
        name="ffn",
    )(xc, modl, modc, g, w1, w3, w2)


def _routing_plan(ids, tm_e):
    n_tok = ids.shape[0]
    n_asg = 2 * n_tok
    n_tiles = n_asg // tm_e + N_EXPERTS
    e_flat = jnp.concatenate([ids[:, 0], ids[:, 1]])
    onehot = (e_flat[:, None] == jnp.arange(N_EXPERTS, dtype=jnp.int32)[None, :]).astype(jnp.int32)
    csum = jnp.cumsum(onehot, axis=0)
    rank = jnp.sum(csum * onehot, axis=1) - 1
    counts = csum[-1]
    padded = ((counts + tm_e - 1) // tm_e) * tm_e
    ends = jnp.cumsum(padded)
    starts = ends - padded
    dest = jnp.sum(onehot * starts[None, :], axis=1) + rank
    row = jnp.arange(n_tiles * tm_e, dtype=jnp.int32)
    spare = n_asg + ((row // tm_e) % 2) * tm_e + row % tm_e
    asg = spare.at[dest].set(jnp.arange(n_asg, dtype=jnp.int32))
    tile_start = jnp.arange(n_tiles, dtype=jnp.int32) * tm_e
    n_used = ends[-1] // tm_e
    tile_e = jnp.minimum(jnp.sum((tile_start[:, None] >= ends[None, :]).astype(jnp.int32), axis=1), N_EXPERTS - 1)
    n_valid = jnp.clip(counts[tile_e] - (tile_start - starts[tile_e]), 0, tm_e)
    n_valid = jnp.where(tile_start < ends[-1], n_valid, 0).astype(jnp.int32)
    last_e = tile_e[jnp.maximum(n_used - 1, 0)]
    tile_e = jnp.where(tile_start < ends[-1], tile_e, last_e).astype(jnp.int32)
    src = jnp.where(asg >= n_asg, 0, jnp.where(asg >= n_tok, asg - n_tok, asg))
    return tile_e, n_valid, asg.reshape(n_tiles, 1, tm_e), src.reshape(n_tiles, 1, tm_e)


SUBLANES = 8


def _expert_kernel(te_ref, nv_ref, asg_ref, src_ref, src_next_ref, h_hbm, w1_ref, w3_ref, w2_ref, y_hbm,
                   hbuf, hbf, acc, obuf, gsem, ssem, *, n_k, tm_e):
    i = pl.program_id(0)
    k = pl.program_id(1)
    n_tiles = pl.num_programs(0)
    used = nv_ref[i] > 0
    nxt = jnp.minimum(i + 1, n_tiles - 1)
    next_used = jnp.logical_and(i + 1 < n_tiles, nv_ref[nxt] > 0)
    slot = i % 2

    def issue_gather(idx_ref, s):
        def body(g, c):
            base = pl.multiple_of(g * SUBLANES, SUBLANES)
            for jj in range(SUBLANES):
                src = idx_ref[0, 0, base + jj]
                pltpu.make_async_copy(h_hbm.at[pl.ds(src, 1), :], hbuf.at[s, pl.ds(base + jj, 1), :],
                                      gsem.at[s]).start()
            return c
        lax.fori_loop(0, tm_e // SUBLANES, body, 0)

    def wait_gather(s):
        pltpu.make_async_copy(h_hbm.at[pl.ds(0, tm_e), :], hbuf.at[s], gsem.at[s]).wait()

    def issue_scatter():
        def body(g, c):
            base = pl.multiple_of(g * SUBLANES, SUBLANES)
            for jj in range(SUBLANES):
                a = asg_ref[0, 0, base + jj]
                pltpu.make_async_copy(obuf.at[pl.ds(base + jj, 1), :], y_hbm.at[pl.ds(a, 1), :], ssem).start()
            return c
        lax.fori_loop(0, tm_e // SUBLANES, body, 0)

    def wait_scatter():
        pltpu.make_async_copy(obuf, y_hbm.at[pl.ds(0, tm_e), :], ssem).wait()

    @pl.when(jnp.logical_and(used, k == 0))
    def _():
        @pl.when(i == 0)
        def _():
            obuf[...] = jnp.zeros_like(obuf)
            n_rows = y_hbm.shape[0]
            for base in (n_rows - 2 * tm_e, n_rows - tm_e):
                pltpu.make_async_copy(obuf, y_hbm.at[pl.ds(base, tm_e), :], ssem).start()
            wait_scatter()
            wait_scatter()
            issue_gather(src_ref, 0)
        wait_gather(slot)
        hbf[...] = hbuf[slot].astype(BF16)

        @pl.when(next_used)
        def _():
            issue_gather(src_next_ref, 1 - slot)

    @pl.when(used)
    def _():
        part = _swiglu_chunks(hbf[...], w1_ref, w3_ref, w2_ref, (0, 0))

        @pl.when(k == 0)
        def _():
            acc[...] = part

        @pl.when(jnp.logical_and(k > 0, k < n_k - 1))
        def _():
            acc[...] += part

        @pl.when(k == n_k - 1)
        def _():
            @pl.when(i > 0)
            def _():
                wait_scatter()
            obuf[...] = acc[...] + part if n_k > 1 else part
            issue_scatter()

            @pl.when(jnp.logical_not(next_used))
            def _():
                wait_scatter()


def _experts(h_flat, tile_e, n_valid, asg, src, w1, w3, w2, j, tm_e):
    n_tok, d = h_flat.shape
    dff = w1.shape[-1]
    n_tiles = asg.shape[0]
    tf = _pick_tile(dff, (896, 256, 128))
    n_k = dff // tf
    kk = lambda i, k, nv: jnp.where(nv[i] > 0, k, n_k - 1)
    grid_spec = pltpu.PrefetchScalarGridSpec(
        num_scalar_prefetch=2,
        grid=(n_tiles, n_k),
        in_specs=[
            pl.BlockSpec((1, 1, tm_e), lambda i, k, te, nv: (i, 0, 0), memory_space=pltpu.SMEM),
            pl.BlockSpec((1, 1, tm_e), lambda i, k, te, nv: (i, 0, 0), memory_space=pltpu.SMEM),
            pl.BlockSpec((1, 1, tm_e), lambda i, k, te, nv: (jnp.minimum(i + 1, n_tiles - 1), 0, 0),
                         memory_space=pltpu.SMEM),
            pl.BlockSpec(memory_space=pl.ANY),
            pl.BlockSpec((1, 1, d, tf), lambda i, k, te, nv: (j, te[i], 0, kk(i, k, nv))),
            pl.BlockSpec((1, 1, d, tf), lambda i, k, te, nv: (j, te[i], 0, kk(i, k, nv))),
            pl.BlockSpec((1, 1, tf, d), lambda i, k, te, nv: (j, te[i], kk(i, k, nv), 0)),
        ],
        out_specs=pl.BlockSpec(memory_space=pl.ANY),
        scratch_shapes=[pltpu.VMEM((2, tm_e, d), F32), pltpu.VMEM((tm_e, d), BF16), pltpu.VMEM((tm_e, d), F32),
                        pltpu.VMEM((tm_e, d), F32), pltpu.SemaphoreType.DMA((2,)), pltpu.SemaphoreType.DMA(())],
    )
    return pl.pallas_call(
        functools.partial(_expert_kernel, n_k=n_k, tm_e=tm_e),
        grid_spec=grid_spec,
        out_shape=jax.ShapeDtypeStruct((2 * n_tok + 2 * tm_e, d), F32),
        compiler_params=_cparams(("arbitrary", "arbitrary")),
        name="experts",
    )(tile_e, n_valid, asg, src, src, h_flat, w1, w3, w2)


def _combine_kernel(x_ref, y0_ref, y1_ref, p_ref, gl_ref, gc_ref, o_ref, *, tm, n_ctx, first):
    i = pl.program_id(1)
    p = p_ref[0]
    lane = lax.broadcasted_iota(jnp.int32, p.shape, 1)
    p0 = jnp.sum(jnp.where(lane == 0, p, 0.0), axis=-1, keepdims=True)
    p1 = jnp.sum(jnp.where(lane == 1, p, 0.0), axis=-1, keepdims=True)
    row = (i + first) * tm + lax.broadcasted_iota(jnp.int32, (tm, 1), 0)
    g2 = jnp.where(row < n_ctx, gc_ref[...], gl_ref[0])
    o_ref[0] = x_ref[0] + g2 * (p0 * y0_ref[...] + p1 * y1_ref[...])


def _combine(xc, y, p, g2_l, g2_c, n_ctx, latent_only):
    b, t, d = xc.shape
    tm = _pick_tile(n_ctx, (256, 128)) if latent_only else _pick_tile(t, (768, 384, 256, 128))
    first = n_ctx // tm if latent_only else 0
    tpb = t // tm
    tok = lambda w: pl.BlockSpec((1, tm, w), lambda bb, i: (bb, i + first, 0))
    ysp = lambda kk: pl.BlockSpec((tm, d), lambda bb, i: ((kk * b + bb) * tpb + i + first, 0))
    return pl.pallas_call(
        functools.partial(_combine_kernel, tm=tm, n_ctx=n_ctx, first=first),
        grid=(b, tpb - first),
        in_specs=[tok(d), ysp(0), ysp(1), tok(128),
                  pl.BlockSpec((1, 1, d), lambda bb, i: (bb, 0, 0)),
                  pl.BlockSpec((1, d), lambda bb, i: (0, 0))],
        out_specs=pl.BlockSpec((1, tm, d), lambda bb, i: (bb, i, 0)),
        out_shape=jax.ShapeDtypeStruct((b, t - first * tm, d), F32),
        compiler_params=_cparams(("parallel", "parallel")),
        name="moe_combine",
    )(xc, y, y, p, g2_l, g2_c)


MOE_TILE = 512


def _moe(xc, h, ids, p, modl, modc, w1, w3, w2, j, n_ctx, latent_only):
    b, t, d = xc.shape
    tile_e, n_valid, asg, src = _routing_plan(ids.reshape(b * t, 128)[:, :2], MOE_TILE)
    y = _experts(h.reshape(b * t, d), tile_e, n_valid, asg, src, w1, w3, w2, j, MOE_TILE)
    return _combine(xc, y, p, modl[:, 2:3], modc[2:3], n_ctx, latent_only)


def _rope_tables(n_ctx, s):
    def widen(cos, sin):
        c64 = jnp.concatenate([cos, cos], axis=-1)
        s64 = jnp.concatenate([-sin, sin], axis=-1)
        return jnp.tile(c64, (1, 2)), jnp.tile(s64, (1, 2))

    n_rows = s // GRID_W
    row = jnp.repeat(jnp.arange(n_rows, dtype=F32), GRID_W)
    col = jnp.tile(jnp.arange(GRID_W, dtype=F32), n_rows)
    n_freq = HEAD_DIM // 4
    inv = ROPE_BASE ** (-jnp.arange(n_freq, dtype=F32) / n_freq)
    ang = jnp.concatenate([row[:, None] * inv, col[:, None] * inv], axis=-1)
    cos_ax = jnp.concatenate([jnp.ones((n_ctx, HEAD_DIM // 2), F32), jnp.cos(ang)], axis=0)
    sin_ax = jnp.concatenate([jnp.zeros((n_ctx, HEAD_DIM // 2), F32), jnp.sin(ang)], axis=0)
    n_freq = HEAD_DIM // 2
    inv = ROPE_BASE ** (-jnp.arange(n_freq, dtype=F32) / n_freq)
    pos = jnp.arange(n_ctx + s, dtype=F32)
    ang = pos[:, None] * inv
    return widen(cos_ax, sin_ax) + widen(jnp.cos(ang), jnp.sin(ang))


def _block_diag_ones(n):
    g = jnp.arange(n) // HEAD_DIM
    return (g[:, None] == g[None, :]).astype(BF16)


def kernel(x, c, ctx, c_ctx, w_mod, b_mod, g_norm1, g_norm2, w_in, mlp_g_v, mlp_w_s, mlp_b_s, win_q_norm,
           win_k_norm, win_sink, glb_q_norm, glb_k_norm, ret_decay, ret_gn, w_branch, w_out, ffn_w1, ffn_w3,
           ffn_w2, router_w, router_b, moe_w1, moe_w3, moe_w2):
    batch, s, d = x.shape
    n_ctx = ctx.shape[1]
    depth = w_in.shape[0]
    assert batch + 1 <= 8 and n_ctx % BLOCK == 0 and s % BLOCK == 0

    xc = jnp.concatenate([ctx, x], axis=1)
    cvec = jnp.zeros((8, d), F32).at[:batch].set(c).at[batch].set(c_ctx)
    mod = _modulation(cvec, w_mod, b_mod)
    mod = mod.reshape(depth, 8, 6, d)

    tabs = _rope_tables(n_ctx, s)
    bd256 = _block_diag_ones(256)
    ones = lambda n: jnp.ones((n,), F32)
    zeros = lambda n: jnp.zeros((n,), F32)
    scale = HEAD_DIM ** -0.5

    for layer in range(depth):
        ml = mod[layer]
        modl1 = ml[:batch, 0:2]
        modc1 = ml[batch, 0:2]
        g1_l = ml[:batch, 2:3]
        g1_c = ml[batch, 2:3]
        modl2 = ml[:batch, 3:6]
        modc2 = ml[batch, 3:6]

        def qk_gain(gq, gk, q_scale):
            return jnp.concatenate([jnp.tile(gq, 4) * q_scale, jnp.tile(gk, 2)])

        gain = jnp.stack([
            jnp.concatenate([mlp_g_v[layer].reshape(-1), ones(128)]),
            qk_gain(win_q_norm[layer], win_k_norm[layer], scale),
            qk_gain(glb_q_norm[layer], glb_k_norm[layer], scale * LOG2_E)])
        post = jnp.concatenate([ones(256), jnp.full((256,), scale, F32)])[None, :]

        p0, kv, gates = _inproj(xc, modl1, modc1, g_norm1[layer][None, :], w_in[layer].astype(BF16),
                                (gain, post, bd256), tabs, n_ctx)
        b_att = _window_attention(p0, kv, win_sink[layer], n_ctx)
        c_att = _global_attention(p0, kv, n_ctx)
        dec = jnp.broadcast_to(ret_decay[layer].reshape(-1, 1), (8, 128))
        o_f, o_b = _retention(p0, dec, n_ctx)
        bias = jnp.repeat(mlp_b_s[layer].T, HEAD_DIM, axis=1)
        j = layer // 2
        g2 = g_norm2[layer][None, :]
        merge_args = (xc, p0, b_att, c_att, o_f, o_b, gates, g1_l, g1_c, mlp_w_s[layer].astype(BF16), bias,
                      ret_gn[layer].reshape(1, -1), bd256, w_branch[layer].astype(BF16),
                      w_out[layer].astype(BF16), n_ctx)
        if layer % 2 == 0:
            xc = _merge(*merge_args)
            xc = _ffn(xc, modl2, modc2, g2, ffn_w1[j].astype(BF16), ffn_w3[j].astype(BF16),
                      ffn_w2[j].astype(BF16), n_ctx)
        else:
            rw_pad = jnp.zeros((d, 128), F32).at[:, :N_EXPERTS].set(router_w[j])
            rb_pad = jnp.zeros((1, 128), F32).at[0, :N_EXPERTS].set(router_b[j])
            xc, h, ids, p = _merge(*merge_args, router=(modl2, modc2, g2, rw_pad, rb_pad))
            xc = _moe(xc, h, ids, p, modl2, modc2, moe_w1, moe_w3, moe_w2, j, n_ctx,
                      latent_only=(layer == depth - 1))
    return xc if depth % 2 == 0 else xc[:, n_ctx:]
```

```python
import functools

import jax
import jax.numpy as jnp
from jax import lax
from jax.experimental import pallas as pl
from jax.experimental.pallas import tpu as pltpu

F32 = jnp.float32
BF16 = jnp.bfloat16

HEAD_DIM = 64
BLOCK = 128
GRID_W = 64
ROPE_BASE = 10000.0
N_BRANCH = 4
BRANCH_W = 256
N_EXPERTS = 8
EPS = 1e-6
NEG_INF = -1e30
LOG2_E = 1.4426950408889634
IN_TILE = 512
N_MIX_TILES = 5
VMEM_LIMIT = 56 * 1024 * 1024


def _cparams(sem):
    return pltpu.CompilerParams(dimension_semantics=sem, vmem_limit_bytes=VMEM_LIMIT)


def _dot(a, b):
    return jnp.dot(a, b, preferred_element_type=F32)


def _dot_nt(a, b):
    return lax.dot_general(a, b, (((1,), (1,)), ((), ())), preferred_element_type=F32)


def _split_dot(a, b_bf16):
    hi = a.astype(BF16)
    lo = (a - hi.astype(F32)).astype(BF16)
    return _dot(hi, b_bf16) + _dot(lo, b_bf16)


def _pick_tile(n, candidates):
    for c in candidates:
        if n % c == 0:
            return c
    raise ValueError(f"no tile for {n} in {candidates}")


def _modulated_norm(x, g, sh_l, sc_l, sh_c, sc_c, row0, n_ctx):
    tm = x.shape[0]
    ms = jnp.mean(x * x, axis=-1, keepdims=True)
    y = x * lax.rsqrt(ms + EPS) * g
    row = row0 + lax.broadcasted_iota(jnp.int32, (tm, 1), 0)
    is_ctx = row < n_ctx
    sh = jnp.where(is_ctx, sh_c, sh_l)
    sc = jnp.where(is_ctx, sc_c, sc_l)
    return y * (1.0 + sc) + sh


def _mod_kernel(c_ref, w_ref, b_ref, o_ref):
    c = c_ref[...]
    s = c * jax.nn.sigmoid(c)
    s_hi = s.astype(BF16)
    s_lo = (s - s_hi.astype(F32)).astype(BF16)
    w = w_ref[0]
    w_hi = w.astype(BF16)
    w_lo = (w - w_hi.astype(F32)).astype(BF16)
    o_ref[0] = _dot(s_hi, w_hi) + _dot(s_hi, w_lo) + _dot(s_lo, w_hi) + b_ref[0]


def _modulation(cvec, w_mod, b_mod):
    depth, d, n = w_mod.shape
    tn = _pick_tile(n, (1536, 1024, 512, 128))
    return pl.pallas_call(
        _mod_kernel,
        grid=(depth, n // tn),
        in_specs=[
            pl.BlockSpec((8, d), lambda l, j: (0, 0)),
            pl.BlockSpec((1, d, tn), lambda l, j: (l, 0, j)),
            pl.BlockSpec((1, 1, tn), lambda l, j: (l, 0, j)),
        ],
        out_specs=pl.BlockSpec((1, 8, tn), lambda l, j: (l, 0, j)),
        out_shape=jax.ShapeDtypeStruct((depth, 8, n), F32),
        compiler_params=_cparams(("parallel", "parallel")),
        name="modulation",
    )(cvec, w_mod, b_mod.reshape(depth, 1, n))


def _group_rms(v, bd, gain):
    sq = (v * v).astype(BF16)
    half = bd.shape[0]
    parts = []
    for c in range(0, v.shape[-1], half):
        wd = min(half, v.shape[-1] - c)
        parts.append(_dot(sq[:, c:c + wd], bd[:wd, :wd]))
    gs = parts[0] if len(parts) == 1 else jnp.concatenate(parts, axis=1)
    return v * lax.rsqrt(gs * (1.0 / HEAD_DIM) + EPS) * gain


def _rope(v, cos, sin):
    w = v.shape[-1]
    lane = lax.broadcasted_iota(jnp.int32, v.shape, 1)
    first = (lane & 32) == 0
    partner = jnp.where(first, pltpu.roll(v, w - 32, 1), pltpu.roll(v, 32, 1))
    return v * cos + partner * sin


def _tile(t, n):
    return jnp.concatenate([t] * n, axis=1)


def _inproj_kernel(x_ref, modl_ref, modc_ref, g_ref, w_ref, gain_ref, post_ref, bd_ref,
                   cax_ref, sax_ref, csq_ref, ssq_ref, p_ref, kv_ref, gate_ref, *, tm, n_ctx, n_tiles):
    i = pl.program_id(1)
    h = _modulated_norm(x_ref[0], g_ref[...], modl_ref[0, 0:1, :], modl_ref[0, 1:2, :],
                        modc_ref[0:1, :], modc_ref[1:2, :], i * tm, n_ctx).astype(BF16)
    lane = lax.broadcasted_iota(jnp.int32, (tm, 128), 1)
    lo = lane < HEAD_DIM

    def attn_tile(acc, j, duplicate):
        c0 = j * IN_TILE
        qk = _group_rms(acc[:, :384], bd_ref[...], gain_ref[j:j + 1, :])
        qk = _rope(qk, _tile(cax_ref[...], 3), _tile(sax_ref[...], 3))
        vv = acc[:, 384:512]
        p_ref[0, :, c0:c0 + 384] = qk.astype(BF16)
        p_ref[0, :, c0 + 384:c0 + 512] = vv.astype(BF16)
        k = qk[:, 256:384]
        kr = pltpu.roll(k, HEAD_DIM, 1)
        vr = pltpu.roll(vv, HEAD_DIM, 1)
        if duplicate:
            parts = [jnp.where(lo, k, kr), jnp.where(lo, kr, k), jnp.where(lo, vv, vr), jnp.where(lo, vr, vv)]
        else:
            pad = jnp.where(lane == HEAD_DIM, 1.0, 0.0)
            parts = [jnp.where(lo, k, pad), jnp.where(lo, kr, pad), jnp.where(lo, vv, pad), jnp.where(lo, vr, pad)]
        for n, part in enumerate(parts):
            c = (j - 1) * IN_TILE + n * 128
            kv_ref[0, :, c:c + 128] = part.astype(BF16)

    project = lambda j: _dot(h, w_ref[:, j * IN_TILE:(j + 1) * IN_TILE])
    acc_next = project(0)
    for j in range(n_tiles):
        c0 = j * IN_TILE
        acc = acc_next
        if j + 1 < n_tiles:
            acc_next = project(j + 1)
        if j == 0:
            v = jax.nn.gelu(acc)
            p_ref[0, :, 0:256] = v[:, :256].astype(BF16)
            p_ref[0, :, 256:512] = _group_rms(v[:, 256:], bd_ref[...], gain_ref[0:1, :256]).astype(BF16)
        elif j == 1:
            attn_tile(acc, 1, True)
        elif j == 2:
            attn_tile(acc, 2, False)
        elif j == 3:
            v = _rope(acc, _tile(csq_ref[...], 4), _tile(ssq_ref[...], 4)) * post_ref[...]
            p_ref[0, :, c0:c0 + IN_TILE] = v.astype(BF16)
        elif j < N_MIX_TILES:
            p_ref[0, :, c0:c0 + IN_TILE] = acc.astype(BF16)
        else:
            g0 = c0 - N_MIX_TILES * IN_TILE
            gate_ref[0, :, g0:g0 + IN_TILE] = (0.5 * acc).astype(BF16)


def _inproj(xc, modl, modc, g, w_bf, ep, tabs, n_ctx):
    b, t, d = xc.shape
    n_in = w_bf.shape[1]
    tm = _pick_tile(t, (384, 256, 128))
    gain, post, bd = ep
    cax, sax, csq, ssq = tabs
    n_gate = n_in - N_MIX_TILES * IN_TILE
    tok = lambda bb, i: (i, 0)
    full2 = lambda bb, i: (0, 0)
    out = lambda w: pl.BlockSpec((1, tm, w), lambda bb, i: (bb, i, 0))
    return pl.pallas_call(
        functools.partial(_inproj_kernel, tm=tm, n_ctx=n_ctx, n_tiles=n_in // IN_TILE),
        grid=(b, t // tm),
        in_specs=[
            pl.BlockSpec((1, tm, d), lambda bb, i: (bb, i, 0)),
            pl.BlockSpec((1, 2, d), lambda bb, i: (bb, 0, 0)),
            pl.BlockSpec((2, d), full2),
            pl.BlockSpec((1, d), full2),
            pl.BlockSpec((d, n_in), full2),
            pl.BlockSpec((3, 384), full2),
            pl.BlockSpec((1, IN_TILE), full2),
            pl.BlockSpec((256, 256), full2),
            pl.BlockSpec((tm, 128), tok),
            pl.BlockSpec((tm, 128), tok),
            pl.BlockSpec((tm, 128), tok),
            pl.BlockSpec((tm, 128), tok),
        ],
        out_specs=[out(N_MIX_TILES * IN_TILE), out(2 * IN_TILE), out(n_gate)],
        out_shape=[
            jax.ShapeDtypeStruct((b, t, N_MIX_TILES * IN_TILE), BF16),
            jax.ShapeDtypeStruct((b, t, 2 * IN_TILE), BF16),
            jax.ShapeDtypeStruct((b, t, n_gate), BF16),
        ],
        compiler_params=_cparams(("parallel", "parallel")),
        name="inproj",
    )(xc, modl, modc, g, w_bf, gain, post, bd, cax, sax, csq, ssq)


def _stack_heads(qa):
    qf = qa.astype(F32)
    lo = lax.broadcasted_iota(jnp.int32, qf.shape, 1) < HEAD_DIM
    return jnp.concatenate([jnp.where(lo, qf, 0.0), jnp.where(lo, 0.0, qf)], axis=0).astype(BF16)


def _unstack_heads(o2):
    n = o2.shape[0] // 2
    lo = lax.broadcasted_iota(jnp.int32, (n, o2.shape[1]), 1) < HEAD_DIM
    return jnp.where(lo, o2[:n], o2[n:])


def _win_kernel(sink_ref, q_ref, kc_ref, vc_ref, kp_ref, kx_ref, kn_ref, vp_ref, vx_ref, vn_ref, o_ref,
                *, n_ctx_blocks, n_blocks, per_step):
    j = pl.program_id(1)
    lc = n_ctx_blocks * BLOCK
    w = lc + 3 * BLOCK
    row = lax.broadcasted_iota(jnp.int32, (BLOCK, w), 0)
    col = lax.broadcasted_iota(jnp.int32, (BLOCK, w), 1)
    c = col - lc
    rid = lax.broadcasted_iota(jnp.int32, (2 * BLOCK, 1), 0)
    blocks = ([(kp_ref, vp_ref, slice(None))]
              + [(kx_ref, vx_ref, slice(u * BLOCK, (u + 1) * BLOCK)) for u in range(per_step)]
              + [(kn_ref, vn_ref, slice(None))])
    chains = [(sub, a) for sub in range(per_step) for a in range(2)]
    lanes = lambda a: slice(128 * a, 128 * (a + 1))
    scores, values = {}, {}
    for sub, a in chains:
        q2 = _stack_heads(q_ref[0, sub * BLOCK:(sub + 1) * BLOCK, lanes(a)])
        local = blocks[sub:sub + 3]
        keys = jnp.concatenate([kc_ref[0, :, lanes(a)]] + [kr[0, rs, lanes(a)] for kr, _, rs in local], axis=0)
        values[sub, a] = jnp.concatenate([vc_ref[0, :, lanes(a)]] + [vr[0, rs, lanes(a)] for _, vr, rs in local],
                                         axis=0)
        scores[sub, a] = _dot_nt(q2, keys)
    probs = {}
    for sub in range(per_step):
        n = per_step * j + sub
        is_lat = n >= n_ctx_blocks
        prev_ok = n > n_ctx_blocks
        next_ok = n < n_blocks - 1
        lo_b = jnp.where(prev_ok, row, BLOCK)
        hi_b = jnp.where(next_ok, row + 2 * BLOCK, 2 * BLOCK - 1)
        lat_bias = jnp.where(is_lat, 0.0, NEG_INF)
        bias = jnp.where(col < lc, 0.0, jnp.where(c >= lo_b, jnp.where(c <= hi_b, lat_bias, NEG_INF), NEG_INF))
        bias2 = jnp.concatenate([bias, bias], axis=0)
        for a in range(2):
            s = scores[sub, a] + bias2
            sk = jnp.where(rid < BLOCK, sink_ref[2 * a], sink_ref[2 * a + 1])
            m = jnp.maximum(jnp.max(s, axis=-1, keepdims=True), sk)
            e = jnp.exp(s - m)
            den = jnp.sum(e, axis=-1, keepdims=True) + jnp.exp(sk - m)
            probs[sub, a] = (e / den).astype(BF16)
    outs = {c: _unstack_heads(_dot(probs[c], values[c])) for c in chains}
    for sub in range(per_step):
        o_ref[0, sub * BLOCK:(sub + 1) * BLOCK, :] = jnp.concatenate([outs[sub, 0], outs[sub, 1]],
                                                                      axis=1).astype(BF16)


def _window_attention(p0, kv, sink, n_ctx):
    b, t, _ = p0.shape
    nb = t // BLOCK
    ncb = n_ctx // BLOCK
    ps = _pick_tile(nb, (6, 5, 4, 3, 2, 1))
    span = lambda cidx: pl.BlockSpec((1, ps * BLOCK, 256), lambda bb, j: (bb, j, cidx))
    prv = lambda cidx: pl.BlockSpec((1, BLOCK, 256), lambda bb, j: (bb, jnp.maximum(ps * j - 1, 0), cidx))
    nxt = lambda cidx: pl.BlockSpec((1, BLOCK, 256), lambda bb, j: (bb, jnp.minimum(ps * j + ps, nb - 1), cidx))
    ctx = lambda cidx: pl.BlockSpec((1, n_ctx, 256), lambda bb, j: (bb, 0, cidx))
    return pl.pallas_call(
        functools.partial(_win_kernel, n_ctx_blocks=ncb, n_blocks=nb, per_step=ps),
        grid=(b, nb // ps),
        in_specs=[
            pl.BlockSpec(memory_space=pltpu.SMEM),
            span(2),
            ctx(0), ctx(1),
            prv(0), span(0), nxt(0),
            prv(1), span(1), nxt(1),
        ],
        out_specs=pl.BlockSpec((1, ps * BLOCK, 256), lambda bb, j: (bb, j, 0)),
        out_shape=jax.ShapeDtypeStruct((b, t, 256), BF16),
        compiler_params=_cparams(("parallel", "parallel")),
        name="window_attn",
    )(sink, p0, kv, kv, kv, kv, kv, kv, kv, kv)


SAFE_LOG2_BOUND = 60.0


def _glb_kernel(q_ref, k_ref, v_ref, o_ref, kmax_scr, *, tq, tk, n_ctx, t_all):
    i = pl.program_id(1)
    sls = (slice(0, 128), slice(128, 256))
    lane = lax.broadcasted_iota(jnp.int32, (2 * tq, 128), 1)

    @pl.when(i == 0)
    def _():
        klane = lax.broadcasted_iota(jnp.int32, (tk, 128), 1)
        for a in range(2):
            def kbody(ci, mx):
                start = pl.multiple_of(ci * tk, tk)
                k = k_ref[0, pl.ds(start, tk), sls[a]].astype(F32)
                sq = jnp.sum(jnp.where(klane < HEAD_DIM, k * k, 0.0), axis=-1, keepdims=True)
                return jnp.maximum(mx, jnp.max(sq, axis=0, keepdims=True))
            mx = lax.fori_loop(0, t_all // tk, kbody, jnp.zeros((1, 1), F32))
            kmax_scr[a:a + 1, :] = jnp.broadcast_to(jnp.sqrt(mx), (1, 128))

    q2s, bnds = [], []
    for a in range(2):
        q = q_ref[0, :, sls[a]].astype(F32)
        q2 = jnp.concatenate([q, pltpu.roll(q, HEAD_DIM, 1)], axis=0)
        q2 = jnp.where(lane < HEAD_DIM, q2, 0.0)
        nrm = jnp.sqrt(jnp.sum(q2 * q2, axis=-1, keepdims=True))
        bnd = nrm * kmax_scr[a:a + 1, 0:1] * 1.01
        q2s.append(q2)
        bnds.append(bnd)
    worst = jnp.max(jnp.maximum(bnds[0], bnds[1]))

    def finish(res):
        outs = []
        for acc in res:
            den = jnp.sum(jnp.where(lane == HEAD_DIM, acc, 0.0), axis=-1, keepdims=True)
            o = acc / den
            lo_t = lax.broadcasted_iota(jnp.int32, (tq, 128), 1) < HEAD_DIM
            outs.append(jnp.where(lo_t, o[:tq], pltpu.roll(o, HEAD_DIM, 1)[tq:]))
        o_ref[0] = jnp.concatenate(outs, axis=1).astype(BF16)

    def attend_fixed(n_chunks, size):
        qb = [jnp.where(lane == HEAD_DIM, -bnds[a], q2s[a]).astype(BF16) for a in range(2)]

        def scores(ci):
            return [_dot_nt(qb[a], k_ref[0, ci * size:(ci + 1) * size, sls[a]]) for a in range(2)]

        accs = [jnp.zeros((2 * tq, 128), F32)] * 2
        s_next = scores(0)
        for ci in range(n_chunks):
            s_cur = s_next
            if ci + 1 < n_chunks:
                s_next = scores(ci + 1)
            ps = [jnp.exp2(s_cur[a]).astype(BF16) for a in range(2)]
            accs = [accs[a] + _dot(ps[a], v_ref[0, ci * size:(ci + 1) * size, sls[a]]) for a in range(2)]
        finish(accs)

    def attend_online(n_chunks, size):
        qb = [q2s[a].astype(BF16) for a in range(2)]

        def body(ci, carry):
            start = pl.multiple_of(ci * size, size)
            out = []
            for a in range(2):
                m, acc = carry[a]
                k = k_ref[0, pl.ds(start, size), sls[a]]
                v = v_ref[0, pl.ds(start, size), sls[a]]
                s = _dot_nt(qb[a], k)
                m_new = jnp.maximum(m, jnp.max(s, axis=-1, keepdims=True))
                p = jnp.exp2(s - m_new)
                acc = jnp.exp2(m - m_new) * acc + _dot(p.astype(BF16), v)
                out.append((m_new, acc))
            return tuple(out)

        init = (jnp.full((2 * tq, 1), NEG_INF, F32), jnp.zeros((2 * tq, 128), F32))
        res = lax.fori_loop(0, n_chunks, body, (init, init))
        finish([acc for (_, acc) in res])

    is_ctx = (i + 1) * tq <= n_ctx
    safe = worst <= SAFE_LOG2_BOUND

    @pl.when(jnp.logical_and(is_ctx, safe))
    def _():
        attend_fixed(1, n_ctx)

    @pl.when(jnp.logical_and(is_ctx, jnp.logical_not(safe)))
    def _():
        attend_online(1, n_ctx)

    @pl.when(jnp.logical_and(jnp.logical_not(is_ctx), safe))
    def _():
        attend_fixed(t_all // tk, tk)

    @pl.when(jnp.logical_and(jnp.logical_not(is_ctx), jnp.logical_not(safe)))
    def _():
        attend_online(t_all // tk, tk)


def _global_attention(p0, kv, n_ctx):
    b, t, _ = p0.shape
    tq = _pick_tile(n_ctx, (256, 128))
    tk = _pick_tile(t, (1408, 768, 384, 256, 128))
    return pl.pallas_call(
        functools.partial(_glb_kernel, tq=tq, tk=tk, n_ctx=n_ctx, t_all=t),
        grid=(b, t // tq),
        in_specs=[
            pl.BlockSpec((1, tq, 256), lambda bb, i: (bb, i, 4)),
            pl.BlockSpec((1, t, 256), lambda bb, i: (bb, 0, 2)),
            pl.BlockSpec((1, t, 256), lambda bb, i: (bb, 0, 3)),
        ],
        out_specs=pl.BlockSpec((1, tq, 256), lambda bb, i: (bb, i, 0)),
        out_shape=jax.ShapeDtypeStruct((b, t, 256), BF16),
        scratch_shapes=[pltpu.VMEM((8, 128), F32)],
        compiler_params=_cparams(("arbitrary", "arbitrary")),
        name="global_attn",
    )(p0, kv, kv)


_TAB_D, _TAB_Q, _TAB_K, _TAB_C, _TAB_ROWS = 0, 256, 384, 512, 640


def _ret_kernel(dec_ref, qf_ref, kf_ref, vf_ref, qb_ref, kb_ref, vb_ref, of_ref, ob_ref, st_scr, tab_scr,
                *, batch):
    i = pl.program_id(0)
    r = lax.broadcasted_iota(jnp.int32, (BLOCK, BLOCK), 0)
    c = lax.broadcasted_iota(jnp.int32, (BLOCK, BLOCK), 1)
    lane_lo = c < HEAD_DIM
    same_head = (r < HEAD_DIM) == lane_lo

    @pl.when(i == 0)
    def _():
        st_scr[...] = jnp.zeros_like(st_scr)
        x = dec_ref[...]
        lg = jnp.where(x >= 0.0, -jnp.log(1.0 + jnp.exp(-x)), x - jnp.log(1.0 + jnp.exp(x)))
        rf = r.astype(F32)
        cf = c.astype(F32)
        for d in range(2):
            for pair in range(2):
                le = lg[4 * d + 2 * pair:4 * d + 2 * pair + 1, :]
                lo_ = lg[4 * d + 2 * pair + 1:4 * d + 2 * pair + 2, :]
                lp = jnp.where(lane_lo[0:1, :], le, lo_)
                if d == 0:
                    rel = rf - cf
                    qpow = rf + 1.0
                    kpow = (BLOCK - 1.0) - rf
                else:
                    rel = cf - rf - 1.0
                    qpow = (BLOCK - 1.0) - rf
                    kpow = rf
                msk = rel >= 0.0
                relc = jnp.where(msk, rel, 0.0)
                tab_scr[d, pair, _TAB_D:_TAB_D + BLOCK, :] = jnp.where(msk, jnp.exp(le * relc), 0.0)
                tab_scr[d, pair, _TAB_D + BLOCK:_TAB_Q, :] = jnp.where(msk, jnp.exp(lo_ * relc), 0.0)
                tab_scr[d, pair, _TAB_Q:_TAB_K, :] = jnp.exp(lp * qpow)
                tab_scr[d, pair, _TAB_K:_TAB_C, :] = jnp.exp(lp * kpow)
                tab_scr[d, pair, _TAB_C:_TAB_ROWS, :] = jnp.where(
                    r < HEAD_DIM, jnp.exp(le * float(BLOCK)), jnp.exp(lo_ * float(BLOCK)))

    refs = ((qf_ref, kf_ref, vf_ref, of_ref), (qb_ref, kb_ref, vb_ref, ob_ref))
    chains = [(d, bb, pair) for d in range(2) for bb in range(batch) for pair in range(2)]
    tab = lambda d, pair, r0, r1: tab_scr[d, pair, r0:r1, :]
    q, k, v, s = {}, {}, {}, {}
    for c in chains:
        d, bb, pair = c
        sl = slice(128 * pair, 128 * (pair + 1))
        q[c] = refs[d][0][bb, :, sl].astype(F32)
        k[c] = refs[d][1][bb, :, sl].astype(F32)
        v[c] = refs[d][2][bb, :, sl]
        s[c] = st_scr[d, bb, pair]
    qk = {}
    for c in chains:
        q2 = jnp.concatenate([jnp.where(lane_lo, q[c], 0.0), jnp.where(lane_lo, 0.0, q[c])], axis=0)
        qk[c] = _dot_nt(q2.astype(BF16), k[c].astype(BF16))
    upd = {c: _dot((k[c] * tab(c[0], c[2], _TAB_K, _TAB_C)).T.astype(BF16), v[c]) for c in chains}
    cross = {c: _dot((q[c] * tab(c[0], c[2], _TAB_Q, _TAB_K)).astype(BF16), s[c].astype(BF16)) for c in chains}
    intra = {c: _dot((qk[c] * tab(c[0], c[2], _TAB_D, _TAB_Q)).astype(BF16), v[c]) for c in chains}
    for c in chains:
        d, bb, pair = c
        sl = slice(128 * pair, 128 * (pair + 1))
        refs[d][3][bb, :, sl] = _unstack_heads(intra[c]) + cross[c]
        st_scr[d, bb, pair] = jnp.where(same_head, tab(d, pair, _TAB_C, _TAB_ROWS) * s[c] + upd[c], 0.0)


def _retention(p0, dec, n_ctx):
    b, t, _ = p0.shape
    nb = t // BLOCK
    ncb = n_ctx // BLOCK
    bwd = lambda i: jnp.where(i < ncb, ncb - 1 - i, nb - 1 + ncb - i)
    fspec = lambda cidx: pl.BlockSpec((b, BLOCK, 256), lambda i: (0, i, cidx))
    bspec = lambda cidx: pl.BlockSpec((b, BLOCK, 256), lambda i: (0, bwd(i), cidx))
    return pl.pallas_call(
        functools.partial(_ret_kernel, batch=b),
        grid=(nb,),
        in_specs=[
            pl.BlockSpec((8, 128), lambda i: (0, 0)),
            fspec(6), fspec(7), fspec(8),
            bspec(6), bspec(7), bspec(8),
        ],
        out_specs=[
            pl.BlockSpec((b, BLOCK, 256), lambda i: (0, i, 0)),
            pl.BlockSpec((b, BLOCK, 256), lambda i: (0, bwd(i), 0)),
        ],
        out_shape=[jax.ShapeDtypeStruct((b, t, 256), F32), jax.ShapeDtypeStruct((b, t, 256), F32)],
        scratch_shapes=[pltpu.VMEM((2, b, 2, BLOCK, BLOCK), F32),
                        pltpu.VMEM((2, 2, _TAB_ROWS, BLOCK), F32)],
        compiler_params=_cparams(("arbitrary",)),
        name="retention",
    )(dec, p0, p0, p0, p0, p0, p0)


def _route(h, rw_ref, rb_ref):
    lane = lax.broadcasted_iota(jnp.int32, (h.shape[0], 128), 1)
    h_hi = h.astype(BF16)
    h_lo = (h - h_hi.astype(F32)).astype(BF16)
    rw = rw_ref[...]
    rw_hi = rw.astype(BF16)
    rw_lo = (rw - rw_hi.astype(F32)).astype(BF16)
    logits = _dot(h_hi, rw_hi) + _dot(h_hi, rw_lo) + _dot(h_lo, rw_hi) + rb_ref[...]
    logits = jnp.where(lane < N_EXPERTS, logits, NEG_INF)
    v0 = jnp.max(logits, axis=-1, keepdims=True)
    i0 = jnp.min(jnp.where(logits == v0, lane, 128), axis=-1, keepdims=True)
    rest = jnp.where(lane == i0, NEG_INF, logits)
    v1 = jnp.max(rest, axis=-1, keepdims=True)
    i1 = jnp.min(jnp.where(rest == v1, lane, 128), axis=-1, keepdims=True)
    e1 = jnp.exp(v1 - v0)
    p0 = 1.0 / (1.0 + e1)
    ids = jnp.where(lane == 0, i0, jnp.where(lane == 1, i1, 0))
    return ids, jnp.where(lane == 0, p0, jnp.where(lane == 1, e1 * p0, 0.0))


def _merge_kernel(x_ref, uv_ref, batt_ref, catt_ref, of_ref, ob_ref, rg_ref, gate_ref, gl_ref, gc_ref,
                  ws_ref, bs_ref, gn_ref, bd_ref, wb_ref, wo_ref, *rest, tm, n_ctx, with_router):
    if with_router:
        modl_ref, modc_ref, g2_ref, rw_ref, rb_ref, o_ref, h_ref, ids_ref, p_ref = rest
    else:
        (o_ref,) = rest
    i = pl.program_id(1)
    grp = jnp.right_shift(lax.broadcasted_iota(jnp.int32, (BLOCK, 256), 1), 6)
    a_chunks = []
    for ch in range(tm // BLOCK):
        rows = slice(ch * BLOCK, (ch + 1) * BLOCK)
        vn = uv_ref[0, rows, 256:512]
        mixed = bs_ref[...]
        for g in range(4):
            mixed = mixed + jnp.where(grp == g, _dot(ws_ref[g], vn), 0.0)
        a_chunks.append((uv_ref[0, rows, 0:256].astype(F32) * mixed).astype(BF16))
    a_br = jnp.concatenate(a_chunks, axis=0)
    o = of_ref[0] + ob_ref[0]
    mean = _split_dot(o, bd_ref[...]) * (1.0 / HEAD_DIM)
    oc = o - mean
    var = _split_dot(oc * oc, bd_ref[...]) * (1.0 / HEAD_DIM)
    rg = rg_ref[0].astype(F32)
    d_br = (oc * lax.rsqrt(var + EPS) * gn_ref[...] * (rg * jax.nn.sigmoid(rg))).astype(BF16)
    branches = (a_br, batt_ref[0], catt_ref[0], d_br)
    d_model = x_ref.shape[-1]
    acc = None
    ys = [_dot(branches[n], wb_ref[n]) for n in range(N_BRANCH)]
    for n in range(N_BRANCH):
        th = jnp.tanh(gate_ref[0, :, n * d_model:(n + 1) * d_model].astype(F32))
        y = ys[n]
        term = y + th * y
        acc = term if acc is None else acc + term
    m = _dot((0.5 * acc).astype(BF16), wo_ref[...])
    row = i * tm + lax.broadcasted_iota(jnp.int32, (tm, 1), 0)
    g1 = jnp.where(row < n_ctx, gc_ref[...], gl_ref[0])
    x_new = x_ref[0] + g1 * m
    o_ref[0] = x_new
    if with_router:
        h = _modulated_norm(x_new, g2_ref[...], modl_ref[0, 0:1, :], modl_ref[0, 1:2, :],
                            modc_ref[0:1, :], modc_ref[1:2, :], i * tm, n_ctx)
        h_ref[0] = h
        ids_ref[0], p_ref[0] = _route(h, rw_ref, rb_ref)


def _merge(xc, p0, b_att, c_att, o_f, o_b, gates, g1_l, g1_c, ws_bf, bias, gn, bd, wb_bf, wo_bf, n_ctx,
           router=None):
    b, t, d = xc.shape
    tm = _pick_tile(t, (384, 256, 128))
    tok = lambda w, cidx: pl.BlockSpec((1, tm, w), lambda bb, i: (bb, i, cidx))
    full = lambda shape: pl.BlockSpec(shape, lambda bb, i: (0,) * len(shape))
    in_specs = [
        tok(d, 0),
        tok(512, 0),
        tok(256, 0), tok(256, 0),
        tok(256, 0), tok(256, 0),
        tok(256, 9),
        tok(N_BRANCH * d, 0),
        pl.BlockSpec((1, 1, d), lambda bb, i: (bb, 0, 0)),
        full((1, d)),
        full((4, BLOCK, BLOCK)), full((BLOCK, 256)), full((1, 256)), full((256, 256)),
        full((N_BRANCH, BRANCH_W, d)), full((d, d)),
    ]
    args = [xc, p0, b_att, c_att, o_f, o_b, p0, gates, g1_l, g1_c, ws_bf, bias, gn, bd, wb_bf, wo_bf]
    out_specs = [tok(d, 0)]
    out_shape = [jax.ShapeDtypeStruct((b, t, d), F32)]
    if router is not None:
        in_specs += [pl.BlockSpec((1, 3, d), lambda bb, i: (bb, 0, 0)), full((3, d)), full((1, d)),
                     full((d, 128)), full((1, 128))]
        args += list(router)
        out_specs += [tok(d, 0), tok(128, 0), tok(128, 0)]
        out_shape += [jax.ShapeDtypeStruct((b, t, d), F32), jax.ShapeDtypeStruct((b, t, 128), jnp.int32),
                      jax.ShapeDtypeStruct((b, t, 128), F32)]
    out = pl.pallas_call(
        functools.partial(_merge_kernel, tm=tm, n_ctx=n_ctx, with_router=router is not None),
        grid=(b, t // tm),
        in_specs=in_specs,
        out_specs=out_specs,
        out_shape=out_shape,
        compiler_params=_cparams(("parallel", "parallel")),
        name="merge",
    )(*args)
    return out if router is not None else out[0]


FF_CHUNK = 512


def _swiglu_chunks(h, w1_ref, w3_ref, w2_ref, lead):
    tf = w1_ref.shape[-1]
    bounds = [(c0, min(c0 + FF_CHUNK, tf)) for c0 in range(0, tf, FF_CHUNK)]

    def up(c0, c1):
        return (_dot(h, w1_ref[lead + (slice(None), slice(c0, c1))].astype(BF16)),
                _dot(h, w3_ref[lead + (slice(None), slice(c0, c1))].astype(BF16)))

    out = None
    nxt = up(*bounds[0])
    for n, (c0, c1) in enumerate(bounds):
        a, bgate = nxt
        if n + 1 < len(bounds):
            nxt = up(*bounds[n + 1])
        act = (a * jax.nn.sigmoid(a) * bgate).astype(BF16)
        part = _dot(act, w2_ref[lead + (slice(c0, c1), slice(None))].astype(BF16))
        out = part if out is None else out + part
    return out


def _ffn_kernel(x_ref, modl_ref, modc_ref, g_ref, w1_ref, w3_ref, w2_ref, o_ref, *, tm, n_ctx):
    i = pl.program_id(1)
    x = x_ref[0]
    h = _modulated_norm(x, g_ref[...], modl_ref[0, 0:1, :], modl_ref[0, 1:2, :],
                        modc_ref[0:1, :], modc_ref[1:2, :], i * tm, n_ctx).astype(BF16)
    y = _swiglu_chunks(h, w1_ref, w3_ref, w2_ref, ())
    row = i * tm + lax.broadcasted_iota(jnp.int32, (tm, 1), 0)
    g2 = jnp.where(row < n_ctx, modc_ref[2:3, :], modl_ref[0, 2:3, :])
    o_ref[0] = x + g2 * y


def _ffn(xc, modl, modc, g, w1, w3, w2, n_ctx):
    b, t, d = xc.shape
    dff = w1.shape[1]
    tm = _pick_tile(t, (384, 256, 128))
    full = lambda bb, i: (0, 0)
    return pl.pallas_call(
        functools.partial(_ffn_kernel, tm=tm, n_ctx=n_ctx),
        grid=(b, t // tm),
        in_specs=[
            pl.BlockSpec((1, tm, d), lambda bb, i: (bb, i, 0)),
            pl.BlockSpec((1, 3, d), lambda bb, i: (bb, 0, 0)),
            pl.BlockSpec((3, d), full),
            pl.BlockSpec((1, d), full),
            pl.BlockSpec((d, dff), full),
            pl.BlockSpec((d, dff), full),
            pl.BlockSpec((dff, d), full),
        ],
        out_specs=pl.BlockSpec((1, tm, d), lambda bb, i: (bb, i, 0)),
        out_shape=jax.ShapeDtypeStruct((b, t, d), F32),
        compiler_params=_cparams(("parallel", "parallel")),
        name="ffn",
    )(xc, modl, modc, g, w1, w3, w2)


def _routing_plan(ids, tm_e):
    n_tok = ids.shape[0]
    n_asg = 2 * n_tok
    n_tiles = n_asg // tm_e + N_EXPERTS
    e_flat = jnp.concatenate([ids[:, 0], ids[:, 1]])
    onehot = (e_flat[:, None] == jnp.arange(N_EXPERTS, dtype=jnp.int32)[None, :]).astype(jnp.int32)
    csum = jnp.cumsum(onehot, axis=0)
    rank = jnp.sum(csum * onehot, axis=1) - 1
    counts = csum[-1]
    padded = ((counts + tm_e - 1) // tm_e) * tm_e
    ends = jnp.cumsum(padded)
    starts = ends - padded
    dest = jnp.sum(onehot * starts[None, :], axis=1) + rank
    row = jnp.arange(n_tiles * tm_e, dtype=jnp.int32)
    spare = n_asg + ((row // tm_e) % 2) * tm_e + row % tm_e
    asg = spare.at[dest].set(jnp.arange(n_asg, dtype=jnp.int32), unique_indices=True, mode="promise_in_bounds")
    tile_start = jnp.arange(n_tiles, dtype=jnp.int32) * tm_e
    n_used = ends[-1] // tm_e
    tile_e = jnp.minimum(jnp.sum((tile_start[:, None] >= ends[None, :]).astype(jnp.int32), axis=1), N_EXPERTS - 1)
    n_valid = jnp.clip(counts[tile_e] - (tile_start - starts[tile_e]), 0, tm_e)
    n_valid = jnp.where(tile_start < ends[-1], n_valid, 0).astype(jnp.int32)
    last_e = tile_e[jnp.maximum(n_used - 1, 0)]
    tile_e = jnp.where(tile_start < ends[-1], tile_e, last_e).astype(jnp.int32)
    src = jnp.where(asg >= n_asg, 0, jnp.where(asg >= n_tok, asg - n_tok, asg))
    return tile_e, n_valid, asg.reshape(n_tiles, 1, tm_e), src.reshape(n_tiles, 1, tm_e)


SUBLANES = 8


def _expert_kernel(te_ref, nv_ref, asg_ref, src_ref, src_next_ref, h_hbm, w1_ref, w3_ref, w2_ref, y_hbm,
                   hbuf, hbf, acc, obuf, gsem, ssem, *, n_k, tm_e):
    i = pl.program_id(0)
    k = pl.program_id(1)
    n_tiles = pl.num_programs(0)
    used = nv_ref[i] > 0
    nxt = jnp.minimum(i + 1, n_tiles - 1)
    next_used = jnp.logical_and(i + 1 < n_tiles, nv_ref[nxt] > 0)
    slot = i % 2

    def issue_gather(idx_ref, s):
        def body(g, c):
            base = pl.multiple_of(g * SUBLANES, SUBLANES)
            for jj in range(SUBLANES):
                src = idx_ref[0, 0, base + jj]
                pltpu.make_async_copy(h_hbm.at[pl.ds(src, 1), :], hbuf.at[s, pl.ds(base + jj, 1), :],
                                      gsem.at[s]).start()
            return c
        lax.fori_loop(0, tm_e // SUBLANES, body, 0)

    def wait_gather(s):
        pltpu.make_async_copy(h_hbm.at[pl.ds(0, tm_e), :], hbuf.at[s], gsem.at[s]).wait()

    def issue_scatter():
        def body(g, c):
            base = pl.multiple_of(g * SUBLANES, SUBLANES)
            for jj in range(SUBLANES):
                a = asg_ref[0, 0, base + jj]
                pltpu.make_async_copy(obuf.at[pl.ds(base + jj, 1), :], y_hbm.at[pl.ds(a, 1), :], ssem).start()
            return c
        lax.fori_loop(0, tm_e // SUBLANES, body, 0)

    def wait_scatter():
        pltpu.make_async_copy(obuf, y_hbm.at[pl.ds(0, tm_e), :], ssem).wait()

    @pl.when(jnp.logical_and(used, k == 0))
    def _():
        @pl.when(i == 0)
        def _():
            obuf[...] = jnp.zeros_like(obuf)
            n_rows = y_hbm.shape[0]
            for base in (n_rows - 2 * tm_e, n_rows - tm_e):
                pltpu.make_async_copy(obuf, y_hbm.at[pl.ds(base, tm_e), :], ssem).start()
            wait_scatter()
            wait_scatter()
            issue_gather(src_ref, 0)
        wait_gather(slot)
        hbf[...] = hbuf[slot].astype(BF16)

        @pl.when(next_used)
        def _():
            issue_gather(src_next_ref, 1 - slot)

    @pl.when(used)
    def _():
        part = _swiglu_chunks(hbf[...], w1_ref, w3_ref, w2_ref, (0, 0))

        @pl.when(k == 0)
        def _():
            acc[...] = part

        @pl.when(jnp.logical_and(k > 0, k < n_k - 1))
        def _():
            acc[...] += part

        @pl.when(k == n_k - 1)
        def _():
            @pl.when(i > 0)
            def _():
                wait_scatter()
            obuf[...] = acc[...] + part if n_k > 1 else part
            issue_scatter()

            @pl.when(jnp.logical_not(next_used))
            def _():
                wait_scatter()


def _experts(h_flat, tile_e, n_valid, asg, src, w1, w3, w2, j, tm_e):
    n_tok, d = h_flat.shape
    dff = w1.shape[-1]
    n_tiles = asg.shape[0]
    tf = _pick_tile(dff, (896, 256, 128))
    n_k = dff // tf
    kk = lambda i, k, nv: jnp.where(nv[i] > 0, k, n_k - 1)
    grid_spec = pltpu.PrefetchScalarGridSpec(
        num_scalar_prefetch=2,
        grid=(n_tiles, n_k),
        in_specs=[
            pl.BlockSpec((1, 1, tm_e), lambda i, k, te, nv: (i, 0, 0), memory_space=pltpu.SMEM),
            pl.BlockSpec((1, 1, tm_e), lambda i, k, te, nv: (i, 0, 0), memory_space=pltpu.SMEM),
            pl.BlockSpec((1, 1, tm_e), lambda i, k, te, nv: (jnp.minimum(i + 1, n_tiles - 1), 0, 0),
                         memory_space=pltpu.SMEM),
            pl.BlockSpec(memory_space=pl.ANY),
            pl.BlockSpec((1, 1, d, tf), lambda i, k, te, nv: (j, te[i], 0, kk(i, k, nv))),
            pl.BlockSpec((1, 1, d, tf), lambda i, k, te, nv: (j, te[i], 0, kk(i, k, nv))),
            pl.BlockSpec((1, 1, tf, d), lambda i, k, te, nv: (j, te[i], kk(i, k, nv), 0)),
        ],
        out_specs=pl.BlockSpec(memory_space=pl.ANY),
        scratch_shapes=[pltpu.VMEM((2, tm_e, d), F32), pltpu.VMEM((tm_e, d), BF16), pltpu.VMEM((tm_e, d), F32),
                        pltpu.VMEM((tm_e, d), F32), pltpu.SemaphoreType.DMA((2,)), pltpu.SemaphoreType.DMA(())],
    )
    return pl.pallas_call(
        functools.partial(_expert_kernel, n_k=n_k, tm_e=tm_e),
        grid_spec=grid_spec,
        out_shape=jax.ShapeDtypeStruct((2 * n_tok + 2 * tm_e, d), F32),
        compiler_params=_cparams(("arbitrary", "arbitrary")),
        name="experts",
    )(tile_e, n_valid, asg, src, src, h_flat, w1, w3, w2)


def _combine_kernel(x_ref, y0_ref, y1_ref, p_ref, gl_ref, gc_ref, o_ref, *, tm, n_ctx, first):
    i = pl.program_id(1)
    p = p_ref[0]
    lane = lax.broadcasted_iota(jnp.int32, p.shape, 1)
    p0 = jnp.sum(jnp.where(lane == 0, p, 0.0), axis=-1, keepdims=True)
    p1 = jnp.sum(jnp.where(lane == 1, p, 0.0), axis=-1, keepdims=True)
    row = (i + first) * tm + lax.broadcasted_iota(jnp.int32, (tm, 1), 0)
    g2 = jnp.where(row < n_ctx, gc_ref[...], gl_ref[0])
    o_ref[0] = x_ref[0] + g2 * (p0 * y0_ref[...] + p1 * y1_ref[...])


def _combine(xc, y, p, g2_l, g2_c, n_ctx, latent_only):
    b, t, d = xc.shape
    tm = _pick_tile(n_ctx, (256, 128)) if latent_only else _pick_tile(t, (768, 384, 256, 128))
    first = n_ctx // tm if latent_only else 0
    tpb = t // tm
    tok = lambda w: pl.BlockSpec((1, tm, w), lambda bb, i: (bb, i + first, 0))
    ysp = lambda kk: pl.BlockSpec((tm, d), lambda bb, i: ((kk * b + bb) * tpb + i + first, 0))
    return pl.pallas_call(
        functools.partial(_combine_kernel, tm=tm, n_ctx=n_ctx, first=first),
        grid=(b, tpb - first),
        in_specs=[tok(d), ysp(0), ysp(1), tok(128),
                  pl.BlockSpec((1, 1, d), lambda bb, i: (bb, 0, 0)),
                  pl.BlockSpec((1, d), lambda bb, i: (0, 0))],
        out_specs=pl.BlockSpec((1, tm, d), lambda bb, i: (bb, i, 0)),
        out_shape=jax.ShapeDtypeStruct((b, t - first * tm, d), F32),
        compiler_params=_cparams(("parallel", "parallel")),
        name="moe_combine",
    )(xc, y, y, p, g2_l, g2_c)


MOE_TILE = 512


def _moe(xc, h, ids, p, modl, modc, w1, w3, w2, j, n_ctx, latent_only):
    b, t, d = xc.shape
    tile_e, n_valid, asg, src = _routing_plan(ids.reshape(b * t, 128)[:, :2], MOE_TILE)
    y = _experts(h.reshape(b * t, d), tile_e, n_valid, asg, src, w1, w3, w2, j, MOE_TILE)
    return _combine(xc, y, p, modl[:, 2:3], modc[2:3], n_ctx, latent_only)


def _rope_tables(n_ctx, s):
    def widen(cos, sin):
        c64 = jnp.concatenate([cos, cos], axis=-1)
        s64 = jnp.concatenate([-sin, sin], axis=-1)
        return jnp.tile(c64, (1, 2)), jnp.tile(s64, (1, 2))

    n_rows = s // GRID_W
    row = jnp.repeat(jnp.arange(n_rows, dtype=F32), GRID_W)
    col = jnp.tile(jnp.arange(GRID_W, dtype=F32), n_rows)
    n_freq = HEAD_DIM // 4
    inv = ROPE_BASE ** (-jnp.arange(n_freq, dtype=F32) / n_freq)
    ang = jnp.concatenate([row[:, None] * inv, col[:, None] * inv], axis=-1)
    cos_ax = jnp.concatenate([jnp.ones((n_ctx, HEAD_DIM // 2), F32), jnp.cos(ang)], axis=0)
    sin_ax = jnp.concatenate([jnp.zeros((n_ctx, HEAD_DIM // 2), F32), jnp.sin(ang)], axis=0)
    n_freq = HEAD_DIM // 2
    inv = ROPE_BASE ** (-jnp.arange(n_freq, dtype=F32) / n_freq)
    pos = jnp.arange(n_ctx + s, dtype=F32)
    ang = pos[:, None] * inv
    return widen(cos_ax, sin_ax) + widen(jnp.cos(ang), jnp.sin(ang))


def _block_diag_ones(n):
    g = jnp.arange(n) // HEAD_DIM
    return (g[:, None] == g[None, :]).astype(BF16)


def kernel(x, c, ctx, c_ctx, w_mod, b_mod, g_norm1, g_norm2, w_in, mlp_g_v, mlp_w_s, mlp_b_s, win_q_norm,
           win_k_norm, win_sink, glb_q_norm, glb_k_norm, ret_decay, ret_gn, w_branch, w_out, ffn_w1, ffn_w3,
           ffn_w2, router_w, router_b, moe_w1, moe_w3, moe_w2):
    batch, s, d = x.shape
    n_ctx = ctx.shape[1]
    depth = w_in.shape[0]
    assert batch + 1 <= 8 and n_ctx % BLOCK == 0 and s % BLOCK == 0

    xc = jnp.concatenate([ctx, x], axis=1)
    cvec = jnp.zeros((8, d), F32).at[:batch].set(c).at[batch].set(c_ctx)
    mod = _modulation(cvec, w_mod, b_mod)
    mod = mod.reshape(depth, 8, 6, d)

    tabs = _rope_tables(n_ctx, s)
    bd256 = _block_diag_ones(256)
    ones = lambda n: jnp.ones((n,), F32)
    zeros = lambda n: jnp.zeros((n,), F32)
    scale = HEAD_DIM ** -0.5

    for layer in range(depth):
        ml = mod[layer]
        modl1 = ml[:batch, 0:2]
        modc1 = ml[batch, 0:2]
        g1_l = ml[:batch, 2:3]
        g1_c = ml[batch, 2:3]
        modl2 = ml[:batch, 3:6]
        modc2 = ml[batch, 3:6]

        def qk_gain(gq, gk, q_scale):
            return jnp.concatenate([jnp.tile(gq, 4) * q_scale, jnp.tile(gk, 2)])

        gain = jnp.stack([
            jnp.concatenate([mlp_g_v[layer].reshape(-1), ones(128)]),
            qk_gain(win_q_norm[layer], win_k_norm[layer], scale),
            qk_gain(glb_q_norm[layer], glb_k_norm[layer], scale * LOG2_E)])
        post = jnp.concatenate([ones(256), jnp.full((256,), scale, F32)])[None, :]

        p0, kv, gates = _inproj(xc, modl1, modc1, g_norm1[layer][None, :], w_in[layer].astype(BF16),
                                (gain, post, bd256), tabs, n_ctx)
        b_att = _window_attention(p0, kv, win_sink[layer], n_ctx)
        c_att = _global_attention(p0, kv, n_ctx)
        dec = jnp.broadcast_to(ret_decay[layer].reshape(-1, 1), (8, 128))
        o_f, o_b = _retention(p0, dec, n_ctx)
        bias = jnp.repeat(mlp_b_s[layer].T, HEAD_DIM, axis=1)
        j = layer // 2
        g2 = g_norm2[layer][None, :]
        merge_args = (xc, p0, b_att, c_att, o_f, o_b, gates, g1_l, g1_c, mlp_w_s[layer].astype(BF16), bias,
                      ret_gn[layer].reshape(1, -1), bd256, w_branch[layer].astype(BF16),
                      w_out[layer].astype(BF16), n_ctx)
        if layer % 2 == 0:
            xc = _merge(*merge_args)
            xc = _ffn(xc, modl2, modc2, g2, ffn_w1[j].astype(BF16), ffn_w3[j].astype(BF16),
                      ffn_w2[j].astype(BF16), n_ctx)
        else:
            rw_pad = jnp.zeros((d, 128), F32).at[:, :N_EXPERTS].set(router_w[j])
            rb_pad = jnp.zeros((1, 128), F32).at[0, :N_EXPERTS].set(router_b[j])
            xc, h, ids, p = _merge(*merge_args, router=(modl2, modc2, g2, rw_pad, rb_pad))
            xc = _moe(xc, h, ids, p, modl2, modc2, moe_w1, moe_w3, moe_w2, j, n_ctx,
                      latent_only=(layer == depth - 1))
    return xc if depth % 2 == 0 else xc[:, n_ctx:]
```

```python
import functools

import jax
import jax.numpy as jnp
from jax import lax
from jax.experimental import pallas as pl
from jax.experimental.pallas import tpu as pltpu

F32 = jnp.float32
BF16 = jnp.bfloat16

HEAD_DIM = 64
BLOCK = 128
GRID_W = 64
ROPE_BASE = 10000.0
N_BRANCH = 4
BRANCH_W = 256
N_EXPERTS = 8
EPS = 1e-6
NEG_INF = -1e30
LOG2_E = 1.4426950408889634
IN_TILE = 512
N_MIX_TILES = 5
VMEM_LIMIT = 56 * 1024 * 1024


def _cparams(sem):
    return pltpu.CompilerParams(dimension_semantics=sem, vmem_limit_bytes=VMEM_LIMIT)


def _dot(a, b):
    return jnp.dot(a, b, preferred_element_type=F32)


def _dot_nt(a, b):
    return lax.dot_general(a, b, (((1,), (1,)), ((), ())), preferred_element_type=F32)


def _split_dot(a, b_bf16):
    hi = a.astype(BF16)
    lo = (a - hi.astype(F32)).astype(BF16)
    return _dot(hi, b_bf16) + _dot(lo, b_bf16)


def _pick_tile(n, candidates):
    for c in candidates:
        if n % c == 0:
            return c
    raise ValueError(f"no tile for {n} in {candidates}")


def _modulated_norm(x, g, sh_l, sc_l, sh_c, sc_c, row0, n_ctx):
    tm = x.shape[0]
    ms = jnp.mean(x * x, axis=-1, keepdims=True)
    y = x * lax.rsqrt(ms + EPS) * g
    row = row0 + lax.broadcasted_iota(jnp.int32, (tm, 1), 0)
    is_ctx = row < n_ctx
    sh = jnp.where(is_ctx, sh_c, sh_l)
    sc = jnp.where(is_ctx, sc_c, sc_l)
    return y * (1.0 + sc) + sh


def _mod_kernel(c_ref, w_ref, b_ref, o_ref):
    c = c_ref[...]
    s = c * jax.nn.sigmoid(c)
    s_hi = s.astype(BF16)
    s_lo = (s - s_hi.astype(F32)).astype(BF16)
    w = w_ref[0]
    w_hi = w.astype(BF16)
    w_lo = (w - w_hi.astype(F32)).astype(BF16)
    o_ref[0] = _dot(s_hi, w_hi) + _dot(s_hi, w_lo) + _dot(s_lo, w_hi) + b_ref[0]


def _modulation(cvec, w_mod, b_mod):
    depth, d, n = w_mod.shape
    tn = _pick_tile(n, (1536, 1024, 512, 128))
    return pl.pallas_call(
        _mod_kernel,
        grid=(depth, n // tn),
        in_specs=[
            pl.BlockSpec((8, d), lambda l, j: (0, 0)),
            pl.BlockSpec((1, d, tn), lambda l, j: (l, 0, j)),
            pl.BlockSpec((1, 1, tn), lambda l, j: (l, 0, j)),
        ],
        out_specs=pl.BlockSpec((1, 8, tn), lambda l, j: (l, 0, j)),
        out_shape=jax.ShapeDtypeStruct((depth, 8, n), F32),
        compiler_params=_cparams(("parallel", "parallel")),
        name="modulation",
    )(cvec, w_mod, b_mod.reshape(depth, 1, n))


def _group_rms(v, bd, gain):
    sq = (v * v).astype(BF16)
    half = bd.shape[0]
    parts = []
    for c in range(0, v.shape[-1], half):
        wd = min(half, v.shape[-1] - c)
        parts.append(_dot(sq[:, c:c + wd], bd[:wd, :wd]))
    gs = parts[0] if len(parts) == 1 else jnp.concatenate(parts, axis=1)
    return v * lax.rsqrt(gs * (1.0 / HEAD_DIM) + EPS) * gain


def _rope(v, cos, sin):
    w = v.shape[-1]
    lane = lax.broadcasted_iota(jnp.int32, v.shape, 1)
    first = (lane & 32) == 0
    partner = jnp.where(first, pltpu.roll(v, w - 32, 1), pltpu.roll(v, 32, 1))
    return v * cos + partner * sin


def _tile(t, n):
    return jnp.concatenate([t] * n, axis=1)


def _inproj_kernel(x_ref, modl_ref, modc_ref, g_ref, w_ref, gain_ref, post_ref, bd_ref,
                   cax_ref, sax_ref, csq_ref, ssq_ref, p_ref, kv_ref, gate_ref, *, tm, n_ctx, n_tiles):
    i = pl.program_id(1)
    h = _modulated_norm(x_ref[0], g_ref[...], modl_ref[0, 0:1, :], modl_ref[0, 1:2, :],
                        modc_ref[0:1, :], modc_ref[1:2, :], i * tm, n_ctx).astype(BF16)
    lane = lax.broadcasted_iota(jnp.int32, (tm, 128), 1)
    lo = lane < HEAD_DIM

    def attn_tile(acc, j, duplicate):
        c0 = j * IN_TILE
        qk = _group_rms(acc[:, :384], bd_ref[...], gain_ref[j:j + 1, :])
        qk = _rope(qk, _tile(cax_ref[...], 3), _tile(sax_ref[...], 3))
        vv = acc[:, 384:512]
        p_ref[0, :, c0:c0 + 384] = qk.astype(BF16)
        p_ref[0, :, c0 + 384:c0 + 512] = vv.astype(BF16)
        k = qk[:, 256:384]
        kr = pltpu.roll(k, HEAD_DIM, 1)
        vr = pltpu.roll(vv, HEAD_DIM, 1)
        if duplicate:
            parts = [jnp.where(lo, k, kr), jnp.where(lo, kr, k), jnp.where(lo, vv, vr), jnp.where(lo, vr, vv)]
        else:
            pad = jnp.where(lane == HEAD_DIM, 1.0, 0.0)
            parts = [jnp.where(lo, k, pad), jnp.where(lo, kr, pad), jnp.where(lo, vv, pad), jnp.where(lo, vr, pad)]
        for n, part in enumerate(parts):
            c = (j - 1) * IN_TILE + n * 128
            kv_ref[0, :, c:c + 128] = part.astype(BF16)

    project = lambda j: _dot(h, w_ref[:, j * IN_TILE:(j + 1) * IN_TILE])
    acc_next = project(0)
    for j in range(n_tiles):
        c0 = j * IN_TILE
        acc = acc_next
        if j + 1 < n_tiles:
            acc_next = project(j + 1)
        if j == 0:
            v = jax.nn.gelu(acc)
            p_ref[0, :, 0:256] = v[:, :256].astype(BF16)
            p_ref[0, :, 256:512] = _group_rms(v[:, 256:], bd_ref[...], gain_ref[0:1, :256]).astype(BF16)
        elif j == 1:
            attn_tile(acc, 1, True)
        elif j == 2:
            attn_tile(acc, 2, False)
        elif j == 3:
            v = _rope(acc, _tile(csq_ref[...], 4), _tile(ssq_ref[...], 4)) * post_ref[...]
            p_ref[0, :, c0:c0 + IN_TILE] = v.astype(BF16)
        elif j < N_MIX_TILES:
            p_ref[0, :, c0:c0 + IN_TILE] = acc.astype(BF16)
        else:
            g0 = c0 - N_MIX_TILES * IN_TILE
            gate_ref[0, :, g0:g0 + IN_TILE] = (0.5 * acc).astype(BF16)


def _inproj(xc, modl, modc, g, w_bf, ep, tabs, n_ctx):
    b, t, d = xc.shape
    n_in = w_bf.shape[1]
    tm = _pick_tile(t, (384, 256, 128))
    gain, post, bd = ep
    cax, sax, csq, ssq = tabs
    n_gate = n_in - N_MIX_TILES * IN_TILE
    tok = lambda bb, i: (i, 0)
    full2 = lambda bb, i: (0, 0)
    out = lambda w: pl.BlockSpec((1, tm, w), lambda bb, i: (bb, i, 0))
    return pl.pallas_call(
        functools.partial(_inproj_kernel, tm=tm, n_ctx=n_ctx, n_tiles=n_in // IN_TILE),
        grid=(b, t // tm),
        in_specs=[
            pl.BlockSpec((1, tm, d), lambda bb, i: (bb, i, 0)),
            pl.BlockSpec((1, 2, d), lambda bb, i: (bb, 0, 0)),
            pl.BlockSpec((2, d), full2),
            pl.BlockSpec((1, d), full2),
            pl.BlockSpec((d, n_in), full2),
            pl.BlockSpec((3, 384), full2),
            pl.BlockSpec((1, IN_TILE), full2),
            pl.BlockSpec((256, 256), full2),
            pl.BlockSpec((tm, 128), tok),
            pl.BlockSpec((tm, 128), tok),
            pl.BlockSpec((tm, 128), tok),
            pl.BlockSpec((tm, 128), tok),
        ],
        out_specs=[out(N_MIX_TILES * IN_TILE), out(2 * IN_TILE), out(n_gate)],
        out_shape=[
            jax.ShapeDtypeStruct((b, t, N_MIX_TILES * IN_TILE), BF16),
            jax.ShapeDtypeStruct((b, t, 2 * IN_TILE), BF16),
            jax.ShapeDtypeStruct((b, t, n_gate), BF16),
        ],
        compiler_params=_cparams(("parallel", "parallel")),
        name="inproj",
    )(xc, modl, modc, g, w_bf, gain, post, bd, cax, sax, csq, ssq)


def _stack_heads(qa):
    qf = qa.astype(F32)
    lo = lax.broadcasted_iota(jnp.int32, qf.shape, 1) < HEAD_DIM
    return jnp.concatenate([jnp.where(lo, qf, 0.0), jnp.where(lo, 0.0, qf)], axis=0).astype(BF16)


def _unstack_heads(o2):
    n = o2.shape[0] // 2
    lo = lax.broadcasted_iota(jnp.int32, (n, o2.shape[1]), 1) < HEAD_DIM
    return jnp.where(lo, o2[:n], o2[n:])


def _win_kernel(sink_ref, q_ref, kc_ref, vc_ref, kp_ref, kx_ref, kn_ref, vp_ref, vx_ref, vn_ref, o_ref,
                *, n_ctx_blocks, n_blocks, per_step):
    j = pl.program_id(1)
    lc = n_ctx_blocks * BLOCK
    w = lc + 3 * BLOCK
    row = lax.broadcasted_iota(jnp.int32, (BLOCK, w), 0)
    col = lax.broadcasted_iota(jnp.int32, (BLOCK, w), 1)
    c = col - lc
    rid = lax.broadcasted_iota(jnp.int32, (2 * BLOCK, 1), 0)
    blocks = ([(kp_ref, vp_ref, slice(None))]
              + [(kx_ref, vx_ref, slice(u * BLOCK, (u + 1) * BLOCK)) for u in range(per_step)]
              + [(kn_ref, vn_ref, slice(None))])
    chains = [(sub, a) for sub in range(per_step) for a in range(2)]
    lanes = lambda a: slice(128 * a, 128 * (a + 1))
    scores, values = {}, {}
    for sub, a in chains:
        q2 = _stack_heads(q_ref[0, sub * BLOCK:(sub + 1) * BLOCK, lanes(a)])
        local = blocks[sub:sub + 3]
        keys = jnp.concatenate([kc_ref[0, :, lanes(a)]] + [kr[0, rs, lanes(a)] for kr, _, rs in local], axis=0)
        values[sub, a] = jnp.concatenate([vc_ref[0, :, lanes(a)]] + [vr[0, rs, lanes(a)] for _, vr, rs in local],
                                         axis=0)
        scores[sub, a] = _dot_nt(q2, keys)
    probs = {}
    for sub in range(per_step):
        n = per_step * j + sub
        is_lat = n >= n_ctx_blocks
        prev_ok = n > n_ctx_blocks
        next_ok = n < n_blocks - 1
        lo_b = jnp.where(prev_ok, row, BLOCK)
        hi_b = jnp.where(next_ok, row + 2 * BLOCK, 2 * BLOCK - 1)
        lat_bias = jnp.where(is_lat, 0.0, NEG_INF)
        bias = jnp.where(col < lc, 0.0, jnp.where(c >= lo_b, jnp.where(c <= hi_b, lat_bias, NEG_INF), NEG_INF))
        bias2 = jnp.concatenate([bias, bias], axis=0)
        for a in range(2):
            s = scores[sub, a] + bias2
            sk = jnp.where(rid < BLOCK, sink_ref[2 * a], sink_ref[2 * a + 1])
            m = jnp.maximum(jnp.max(s, axis=-1, keepdims=True), sk)
            e = jnp.exp(s - m)
            den = jnp.sum(e, axis=-1, keepdims=True) + jnp.exp(sk - m)
            probs[sub, a] = (e / den).astype(BF16)
    outs = {c: _unstack_heads(_dot(probs[c], values[c])) for c in chains}
    for sub in range(per_step):
        o_ref[0, sub * BLOCK:(sub + 1) * BLOCK, :] = jnp.concatenate([outs[sub, 0], outs[sub, 1]],
                                                                      axis=1).astype(BF16)


def _window_attention(p0, kv, sink, n_ctx):
    b, t, _ = p0.shape
    nb = t // BLOCK
    ncb = n_ctx // BLOCK
    ps = _pick_tile(nb, (6, 5, 4, 3, 2, 1))
    span = lambda cidx: pl.BlockSpec((1, ps * BLOCK, 256), lambda bb, j: (bb, j, cidx))
    prv = lambda cidx: pl.BlockSpec((1, BLOCK, 256), lambda bb, j: (bb, jnp.maximum(ps * j - 1, 0), cidx))
    nxt = lambda cidx: pl.BlockSpec((1, BLOCK, 256), lambda bb, j: (bb, jnp.minimum(ps * j + ps, nb - 1), cidx))
    ctx = lambda cidx: pl.BlockSpec((1, n_ctx, 256), lambda bb, j: (bb, 0, cidx))
    return pl.pallas_call(
        functools.partial(_win_kernel, n_ctx_blocks=ncb, n_blocks=nb, per_step=ps),
        grid=(b, nb // ps),
        in_specs=[
            pl.BlockSpec(memory_space=pltpu.SMEM),
            span(2),
            ctx(0), ctx(1),
            prv(0), span(0), nxt(0),
            prv(1), span(1), nxt(1),
        ],
        out_specs=pl.BlockSpec((1, ps * BLOCK, 256), lambda bb, j: (bb, j, 0)),
        out_shape=jax.ShapeDtypeStruct((b, t, 256), BF16),
        compiler_params=_cparams(("parallel", "parallel")),
        name="window_attn",
    )(sink, p0, kv, kv, kv, kv, kv, kv, kv, kv)


SAFE_LOG2_BOUND = 60.0


def _glb_kernel(q_ref, k_ref, v_ref, o_ref, kmax_scr, *, tq, tk, n_ctx, t_all):
    i = pl.program_id(1)
    sls = (slice(0, 128), slice(128, 256))
    lane = lax.broadcasted_iota(jnp.int32, (2 * tq, 128), 1)

    @pl.when(i == 0)
    def _():
        klane = lax.broadcasted_iota(jnp.int32, (tk, 128), 1)
        for a in range(2):
            def kbody(ci, mx):
                start = pl.multiple_of(ci * tk, tk)
                k = k_ref[0, pl.ds(start, tk), sls[a]].astype(F32)
                sq = jnp.sum(jnp.where(klane < HEAD_DIM, k * k, 0.0), axis=-1, keepdims=True)
                return jnp.maximum(mx, jnp.max(sq, axis=0, keepdims=True))
            mx = lax.fori_loop(0, t_all // tk, kbody, jnp.zeros((1, 1), F32))
            kmax_scr[a:a + 1, :] = jnp.broadcast_to(jnp.sqrt(mx), (1, 128))

    q2s, bnds = [], []
    for a in range(2):
        q = q_ref[0, :, sls[a]].astype(F32)
        q2 = jnp.concatenate([q, pltpu.roll(q, HEAD_DIM, 1)], axis=0)
        q2 = jnp.where(lane < HEAD_DIM, q2, 0.0)
        nrm = jnp.sqrt(jnp.sum(q2 * q2, axis=-1, keepdims=True))
        bnd = nrm * kmax_scr[a:a + 1, 0:1] * 1.01
        q2s.append(q2)
        bnds.append(bnd)
    worst = jnp.max(jnp.maximum(bnds[0], bnds[1]))

    def finish(res):
        outs = []
        for acc in res:
            den = jnp.sum(jnp.where(lane == HEAD_DIM, acc, 0.0), axis=-1, keepdims=True)
            o = acc / den
            lo_t = lax.broadcasted_iota(jnp.int32, (tq, 128), 1) < HEAD_DIM
            outs.append(jnp.where(lo_t, o[:tq], pltpu.roll(o, HEAD_DIM, 1)[tq:]))
        o_ref[0] = jnp.concatenate(outs, axis=1).astype(BF16)

    def attend_fixed(n_chunks, size):
        qb = [jnp.where(lane == HEAD_DIM, -bnds[a], q2s[a]).astype(BF16) for a in range(2)]

        def scores(ci):
            return [_dot_nt(qb[a], k_ref[0, ci * size:(ci + 1) * size, sls[a]]) for a in range(2)]

        accs = [jnp.zeros((2 * tq, 128), F32)] * 2
        s_next = scores(0)
        for ci in range(n_chunks):
            s_cur = s_next
            if ci + 1 < n_chunks:
                s_next = scores(ci + 1)
            ps = [jnp.exp2(s_cur[a]).astype(BF16) for a in range(2)]
            accs = [accs[a] + _dot(ps[a], v_ref[0, ci * size:(ci + 1) * size, sls[a]]) for a in range(2)]
        finish(accs)

    def attend_online(n_chunks, size):
        qb = [q2s[a].astype(BF16) for a in range(2)]

        def body(ci, carry):
            start = pl.multiple_of(ci * size, size)
            out = []
            for a in range(2):
                m, acc = carry[a]
                k = k_ref[0, pl.ds(start, size), sls[a]]
                v = v_ref[0, pl.ds(start, size), sls[a]]
                s = _dot_nt(qb[a], k)
                m_new = jnp.maximum(m, jnp.max(s, axis=-1, keepdims=True))
                p = jnp.exp2(s - m_new)
                acc = jnp.exp2(m - m_new) * acc + _dot(p.astype(BF16), v)
                out.append((m_new, acc))
            return tuple(out)

        init = (jnp.full((2 * tq, 1), NEG_INF, F32), jnp.zeros((2 * tq, 128), F32))
        res = lax.fori_loop(0, n_chunks, body, (init, init))
        finish([acc for (_, acc) in res])

    is_ctx = (i + 1) * tq <= n_ctx
    safe = worst <= SAFE_LOG2_BOUND

    @pl.when(jnp.logical_and(is_ctx, safe))
    def _():
        attend_fixed(1, n_ctx)

    @pl.when(jnp.logical_and(is_ctx, jnp.logical_not(safe)))
    def _():
        attend_online(1, n_ctx)

    @pl.when(jnp.logical_and(jnp.logical_not(is_ctx), safe))
    def _():
        attend_fixed(t_all // tk, tk)

    @pl.when(jnp.logical_and(jnp.logical_not(is_ctx), jnp.logical_not(safe)))
    def _():
        attend_online(t_all // tk, tk)


def _global_attention(p0, kv, n_ctx):
    b, t, _ = p0.shape
    tq = _pick_tile(n_ctx, (256, 128))
    tk = _pick_tile(t, (1408, 768, 384, 256, 128))
    return pl.pallas_call(
        functools.partial(_glb_kernel, tq=tq, tk=tk, n_ctx=n_ctx, t_all=t),
        grid=(b, t // tq),
        in_specs=[
            pl.BlockSpec((1, tq, 256), lambda bb, i: (bb, i, 4)),
            pl.BlockSpec((1, t, 256), lambda bb, i: (bb, 0, 2)),
            pl.BlockSpec((1, t, 256), lambda bb, i: (bb, 0, 3)),
        ],
        out_specs=pl.BlockSpec((1, tq, 256), lambda bb, i: (bb, i, 0)),
        out_shape=jax.ShapeDtypeStruct((b, t, 256), BF16),
        scratch_shapes=[pltpu.VMEM((8, 128), F32)],
        compiler_params=_cparams(("arbitrary", "arbitrary")),
        name="global_attn",
    )(p0, kv, kv)


_TAB_D, _TAB_Q, _TAB_K, _TAB_C, _TAB_ROWS = 0, 256, 384, 512, 640


def _ret_kernel(dec_ref, qf_ref, kf_ref, vf_ref, qb_ref, kb_ref, vb_ref, of_ref, ob_ref, st_scr, tab_scr,
                *, batch):
    i = pl.program_id(0)
    r = lax.broadcasted_iota(jnp.int32, (BLOCK, BLOCK), 0)
    c = lax.broadcasted_iota(jnp.int32, (BLOCK, BLOCK), 1)
    lane_lo = c < HEAD_DIM
    same_head = (r < HEAD_DIM) == lane_lo

    @pl.when(i == 0)
    def _():
        st_scr[...] = jnp.zeros_like(st_scr)
        x = dec_ref[...]
        lg = jnp.where(x >= 0.0, -jnp.log(1.0 + jnp.exp(-x)), x - jnp.log(1.0 + jnp.exp(x)))
        rf = r.astype(F32)
        cf = c.astype(F32)
        for d in range(2):
            for pair in range(2):
                le = lg[4 * d + 2 * pair:4 * d + 2 * pair + 1, :]
                lo_ = lg[4 * d + 2 * pair + 1:4 * d + 2 * pair + 2, :]
                lp = jnp.where(lane_lo[0:1, :], le, lo_)
                if d == 0:
                    rel = rf - cf
                    qpow = rf + 1.0
                    kpow = (BLOCK - 1.0) - rf
                else:
                    rel = cf - rf - 1.0
                    qpow = (BLOCK - 1.0) - rf
                    kpow = rf
                msk = rel >= 0.0
                relc = jnp.where(msk, rel, 0.0)
                tab_scr[d, pair, _TAB_D:_TAB_D + BLOCK, :] = jnp.where(msk, jnp.exp(le * relc), 0.0)
                tab_scr[d, pair, _TAB_D + BLOCK:_TAB_Q, :] = jnp.where(msk, jnp.exp(lo_ * relc), 0.0)
                tab_scr[d, pair, _TAB_Q:_TAB_K, :] = jnp.exp(lp * qpow)
                tab_scr[d, pair, _TAB_K:_TAB_C, :] = jnp.exp(lp * kpow)
                tab_scr[d, pair, _TAB_C:_TAB_ROWS, :] = jnp.where(
                    r < HEAD_DIM, jnp.exp(le * float(BLOCK)), jnp.exp(lo_ * float(BLOCK)))

    refs = ((qf_ref, kf_ref, vf_ref, of_ref), (qb_ref, kb_ref, vb_ref, ob_ref))
    chains = [(d, bb, pair) for d in range(2) for bb in range(batch) for pair in range(2)]
    tab = lambda d, pair, r0, r1: tab_scr[d, pair, r0:r1, :]
    q, k, v, s = {}, {}, {}, {}
    for c in chains:
        d, bb, pair = c
        sl = slice(128 * pair, 128 * (pair + 1))
        q[c] = refs[d][0][bb, :, sl].astype(F32)
        k[c] = refs[d][1][bb, :, sl].astype(F32)
        v[c] = refs[d][2][bb, :, sl]
        s[c] = st_scr[d, bb, pair]
    qk = {}
    for c in chains:
        q2 = jnp.concatenate([jnp.where(lane_lo, q[c], 0.0), jnp.where(lane_lo, 0.0, q[c])], axis=0)
        qk[c] = _dot_nt(q2.astype(BF16), k[c].astype(BF16))
    upd = {c: _dot((k[c] * tab(c[0], c[2], _TAB_K, _TAB_C)).T.astype(BF16), v[c]) for c in chains}
    cross = {c: _dot((q[c] * tab(c[0], c[2], _TAB_Q, _TAB_K)).astype(BF16), s[c].astype(BF16)) for c in chains}
    intra = {c: _dot((qk[c] * tab(c[0], c[2], _TAB_D, _TAB_Q)).astype(BF16), v[c]) for c in chains}
    for c in chains:
        d, bb, pair = c
        sl = slice(128 * pair, 128 * (pair + 1))
        refs[d][3][bb, :, sl] = _unstack_heads(intra[c]) + cross[c]
        st_scr[d, bb, pair] = jnp.where(same_head, tab(d, pair, _TAB_C, _TAB_ROWS) * s[c] + upd[c], 0.0)


def _retention(p0, dec, n_ctx):
    b, t, _ = p0.shape
    nb = t // BLOCK
    ncb = n_ctx // BLOCK
    bwd = lambda i: jnp.where(i < ncb, ncb - 1 - i, nb - 1 + ncb - i)
    fspec = lambda cidx: pl.BlockSpec((b, BLOCK, 256), lambda i: (0, i, cidx))
    bspec = lambda cidx: pl.BlockSpec((b, BLOCK, 256), lambda i: (0, bwd(i), cidx))
    return pl.pallas_call(
        functools.partial(_ret_kernel, batch=b),
        grid=(nb,),
        in_specs=[
            pl.BlockSpec((8, 128), lambda i: (0, 0)),
            fspec(6), fspec(7), fspec(8),
            bspec(6), bspec(7), bspec(8),
        ],
        out_specs=[
            pl.BlockSpec((b, BLOCK, 256), lambda i: (0, i, 0)),
            pl.BlockSpec((b, BLOCK, 256), lambda i: (0, bwd(i), 0)),
        ],
        out_shape=[jax.ShapeDtypeStruct((b, t, 256), F32), jax.ShapeDtypeStruct((b, t, 256), F32)],
        scratch_shapes=[pltpu.VMEM((2, b, 2, BLOCK, BLOCK), F32),
                        pltpu.VMEM((2, 2, _TAB_ROWS, BLOCK), F32)],
        compiler_params=_cparams(("arbitrary",)),
        name="retention",
    )(dec, p0, p0, p0, p0, p0, p0)


def _route(h, rw_ref, rb_ref):
    lane = lax.broadcasted_iota(jnp.int32, (h.shape[0], 128), 1)
    h_hi = h.astype(BF16)
    h_lo = (h - h_hi.astype(F32)).astype(BF16)
    rw = rw_ref[...]
    rw_hi = rw.astype(BF16)
    rw_lo = (rw - rw_hi.astype(F32)).astype(BF16)
    logits = _dot(h_hi, rw_hi) + _dot(h_hi, rw_lo) + _dot(h_lo, rw_hi) + rb_ref[...]
    logits = jnp.where(lane < N_EXPERTS, logits, NEG_INF)
    v0 = jnp.max(logits, axis=-1, keepdims=True)
    i0 = jnp.min(jnp.where(logits == v0, lane, 128), axis=-1, keepdims=True)
    rest = jnp.where(lane == i0, NEG_INF, logits)
    v1 = jnp.max(rest, axis=-1, keepdims=True)
    i1 = jnp.min(jnp.where(rest == v1, lane, 128), axis=-1, keepdims=True)
    e1 = jnp.exp(v1 - v0)
    p0 = 1.0 / (1.0 + e1)
    ids = jnp.where(lane == 0, i0, jnp.where(lane == 1, i1, 0))
    return ids, jnp.where(lane == 0, p0, jnp.where(lane == 1, e1 * p0, 0.0))


def _merge_kernel(x_ref, uv_ref, batt_ref, catt_ref, of_ref, ob_ref, rg_ref, gate_ref, gl_ref, gc_ref,
                  ws_ref, bs_ref, gn_ref, bd_ref, wb_ref, wo_ref, *rest, tm, n_ctx, with_router):
    if with_router:
        modl_ref, modc_ref, g2_ref, rw_ref, rb_ref, o_ref, h_ref, ids_ref, p_ref = rest
    else:
        (o_ref,) = rest
    i = pl.program_id(1)
    grp = jnp.right_shift(lax.broadcasted_iota(jnp.int32, (BLOCK, 256), 1), 6)
    a_chunks = []
    for ch in range(tm // BLOCK):
        rows = slice(ch * BLOCK, (ch + 1) * BLOCK)
        vn = uv_ref[0, rows, 256:512]
        mixed = bs_ref[...]
        for g in range(4):
            mixed = mixed + jnp.where(grp == g, _dot(ws_ref[g], vn), 0.0)
        a_chunks.append((uv_ref[0, rows, 0:256].astype(F32) * mixed).astype(BF16))
    a_br = jnp.concatenate(a_chunks, axis=0)
    o = of_ref[0] + ob_ref[0]
    mean = _split_dot(o, bd_ref[...]) * (1.0 / HEAD_DIM)
    oc = o - mean
    var = _split_dot(oc * oc, bd_ref[...]) * (1.0 / HEAD_DIM)
    rg = rg_ref[0].astype(F32)
    d_br = (oc * lax.rsqrt(var + EPS) * gn_ref[...] * (rg * jax.nn.sigmoid(rg))).astype(BF16)
    branches = (a_br, batt_ref[0], catt_ref[0], d_br)
    d_model = x_ref.shape[-1]
    acc = None
    ys = [_dot(branches[n], wb_ref[n]) for n in range(N_BRANCH)]
    for n in range(N_BRANCH):
        th = jnp.tanh(gate_ref[0, :, n * d_model:(n + 1) * d_model].astype(F32))
        y = ys[n]
        term = y + th * y
        acc = term if acc is None else acc + term
    m = _dot((0.5 * acc).astype(BF16), wo_ref[...])
    row = i * tm + lax.broadcasted_iota(jnp.int32, (tm, 1), 0)
    g1 = jnp.where(row < n_ctx, gc_ref[...], gl_ref[0])
    x_new = x_ref[0] + g1 * m
    o_ref[0] = x_new
    if with_router:
        h = _modulated_norm(x_new, g2_ref[...], modl_ref[0, 0:1, :], modl_ref[0, 1:2, :],
                            modc_ref[0:1, :], modc_ref[1:2, :], i * tm, n_ctx)
        h_ref[0] = h
        ids_ref[0], p_ref[0] = _route(h, rw_ref, rb_ref)


def _merge(xc, p0, b_att, c_att, o_f, o_b, gates, g1_l, g1_c, ws_bf, bias, gn, bd, wb_bf, wo_bf, n_ctx,
           router=None):
    b, t, d = xc.shape
    tm = _pick_tile(t, (384, 256, 128))
    tok = lambda w, cidx: pl.BlockSpec((1, tm, w), lambda bb, i: (bb, i, cidx))
    full = lambda shape: pl.BlockSpec(shape, lambda bb, i: (0,) * len(shape))
    in_specs = [
        tok(d, 0),
        tok(512, 0),
        tok(256, 0), tok(256, 0),
        tok(256, 0), tok(256, 0),
        tok(256, 9),
        tok(N_BRANCH * d, 0),
        pl.BlockSpec((1, 1, d), lambda bb, i: (bb, 0, 0)),
        full((1, d)),
        full((4, BLOCK, BLOCK)), full((BLOCK, 256)), full((1, 256)), full((256, 256)),
        full((N_BRANCH, BRANCH_W, d)), full((d, d)),
    ]
    args = [xc, p0, b_att, c_att, o_f, o_b, p0, gates, g1_l, g1_c, ws_bf, bias, gn, bd, wb_bf, wo_bf]
    out_specs = [tok(d, 0)]
    out_shape = [jax.ShapeDtypeStruct((b, t, d), F32)]
    if router is not None:
        in_specs += [pl.BlockSpec((1, 3, d), lambda bb, i: (bb, 0, 0)), full((3, d)), full((1, d)),
                     full((d, 128)), full((1, 128))]
        args += list(router)
        out_specs += [tok(d, 0), tok(128, 0), tok(128, 0)]
        out_shape += [jax.ShapeDtypeStruct((b, t, d), F32), jax.ShapeDtypeStruct((b, t, 128), jnp.int32),
                      jax.ShapeDtypeStruct((b, t, 128), F32)]
    out = pl.pallas_call(
        functools.partial(_merge_kernel, tm=tm, n_ctx=n_ctx, with_router=router is not None),
        grid=(b, t // tm),
        in_specs=in_specs,
        out_specs=out_specs,
        out_shape=out_shape,
        compiler_params=_cparams(("parallel", "parallel")),
        name="merge",
    )(*args)
    return out if router is not None else out[0]


FF_CHUNK = 512


def _swiglu_chunks(h, w1_ref, w3_ref, w2_ref, lead):
    tf = w1_ref.shape[-1]
    bounds = [(c0, min(c0 + FF_CHUNK, tf)) for c0 in range(0, tf, FF_CHUNK)]

    def up(c0, c1):
        return (_dot(h, w1_ref[lead + (slice(None), slice(c0, c1))].astype(BF16)),
                _dot(h, w3_ref[lead + (slice(None), slice(c0, c1))].astype(BF16)))

    out = None
    nxt = up(*bounds[0])
    for n, (c0, c1) in enumerate(bounds):
        a, bgate = nxt
        if n + 1 < len(bounds):
            nxt = up(*bounds[n + 1])
        act = (a * jax.nn.sigmoid(a) * bgate).astype(BF16)
        part = _dot(act, w2_ref[lead + (slice(c0, c1), slice(None))].astype(BF16))
        out = part if out is None else out + part
    return out


def _ffn_kernel(x_ref, modl_ref, modc_ref, g_ref, w1_ref, w3_ref, w2_ref, o_ref, *, tm, n_ctx):
    i = pl.program_id(1)
    x = x_ref[0]
    h = _modulated_norm(x, g_ref[...], modl_ref[0, 0:1, :], modl_ref[0, 1:2, :],
                        modc_ref[0:1, :], modc_ref[1:2, :], i * tm, n_ctx).astype(BF16)
    y = _swiglu_chunks(h, w1_ref, w3_ref, w2_ref, ())
    row = i * tm + lax.broadcasted_iota(jnp.int32, (tm, 1), 0)
    g2 = jnp.where(row < n_ctx, modc_ref[2:3, :], modl_ref[0, 2:3, :])
    o_ref[0] = x + g2 * y


def _ffn(xc, modl, modc, g, w1, w3, w2, n_ctx):
    b, t, d = xc.shape
    dff = w1.shape[1]
    tm = _pick_tile(t, (384, 256, 128))
    full = lambda bb, i: (0, 0)
    return pl.pallas_call(
        functools.partial(_ffn_kernel, tm=tm, n_ctx=n_ctx),
        grid=(b, t // tm),
        in_specs=[
            pl.BlockSpec((1, tm, d), lambda bb, i: (bb, i, 0)),
            pl.BlockSpec((1, 3, d), lambda bb, i: (bb, 0, 0)),
            pl.BlockSpec((3, d), full),
            pl.BlockSpec((1, d), full),
            pl.BlockSpec((d, dff), full),
            pl.BlockSpec((d, dff), full),
            pl.BlockSpec((dff, d), full),
        ],
        out_specs=pl.BlockSpec((1, tm, d), lambda bb, i: (bb, i, 0)),
        out_shape=jax.ShapeDtypeStruct((b, t, d), F32),
        compiler_params=_cparams(("parallel", "parallel")),
        name="ffn",
    )(xc, modl, modc, g, w1, w3, w2)


def _routing_plan(ids, tm_e):
    n_tok = ids.shape[0]
    n_asg = 2 * n_tok
    n_tiles = n_asg // tm_e + N_EXPERTS
    e_flat = jnp.concatenate([ids[:, 0], ids[:, 1]])
    onehot = (e_flat[:, None] == jnp.arange(N_EXPERTS, dtype=jnp.int32)[None, :]).astype(jnp.int32)
    csum = jnp.cumsum(onehot, axis=0)
    rank = jnp.sum(csum * onehot, axis=1) - 1
    counts = csum[-1]
    padded = ((counts + tm_e - 1) // tm_e) * tm_e
    ends = jnp.cumsum(padded)
    starts = ends - padded
    dest = jnp.sum(onehot * starts[None, :], axis=1) + rank
    row = jnp.arange(n_tiles * tm_e, dtype=jnp.int32)
    spare = n_asg + ((row // tm_e) % 2) * tm_e + row % tm_e
    asg = spare.at[dest].set(jnp.arange(n_asg, dtype=jnp.int32))
    tile_start = jnp.arange(n_tiles, dtype=jnp.int32) * tm_e
    n_used = ends[-1] // tm_e
    tile_e = jnp.minimum(jnp.sum((tile_start[:, None] >= ends[None, :]).astype(jnp.int32), axis=1), N_EXPERTS - 1)
    n_valid = jnp.clip(counts[tile_e] - (tile_start - starts[tile_e]), 0, tm_e)
    n_valid = jnp.where(tile_start < ends[-1], n_valid, 0).astype(jnp.int32)
    last_e = tile_e[jnp.maximum(n_used - 1, 0)]
    tile_e = jnp.where(tile_start < ends[-1], tile_e, last_e).astype(jnp.int32)
    src = jnp.where(asg >= n_asg, 0, jnp.where(asg >= n_tok, asg - n_tok, asg))
    return tile_e, n_valid, asg.reshape(n_tiles, 1, tm_e), src.reshape(n_tiles, 1, tm_e)


SUBLANES = 8


def _expert_kernel(te_ref, nv_ref, asg_ref, src_ref, src_next_ref, h_hbm, w1_ref, w3_ref, w2_ref, y_hbm,
                   hbuf, hbf, acc, obuf, gsem, ssem, *, n_k, tm_e):
    i = pl.program_id(0)
    k = pl.program_id(1)
    n_tiles = pl.num_programs(0)
    used = nv_ref[i] > 0
    nxt = jnp.minimum(i + 1, n_tiles - 1)
    next_used = jnp.logical_and(i + 1 < n_tiles, nv_ref[nxt] > 0)
    slot = i % 2

    def issue_gather(idx_ref, s):
        def body(g, c):
            base = pl.multiple_of(g * SUBLANES, SUBLANES)
            for jj in range(SUBLANES):
                src = idx_ref[0, 0, base + jj]
                pltpu.make_async_copy(h_hbm.at[pl.ds(src, 1), :], hbuf.at[s, pl.ds(base + jj, 1), :],
                                      gsem.at[s]).start(priority=jj % 2)
            return c
        lax.fori_loop(0, tm_e // SUBLANES, body, 0)

    def wait_gather(s):
        pltpu.make_async_copy(h_hbm.at[pl.ds(0, tm_e), :], hbuf.at[s], gsem.at[s]).wait()

    def issue_scatter():
        def body(g, c):
            base = pl.multiple_of(g * SUBLANES, SUBLANES)
            for jj in range(SUBLANES):
                a = asg_ref[0, 0, base + jj]
                pltpu.make_async_copy(obuf.at[pl.ds(base + jj, 1), :], y_hbm.at[pl.ds(a, 1), :],
                                      ssem).start(priority=jj % 2)
            return c
        lax.fori_loop(0, tm_e // SUBLANES, body, 0)

    def wait_scatter():
        pltpu.make_async_copy(obuf, y_hbm.at[pl.ds(0, tm_e), :], ssem).wait()

    @pl.when(jnp.logical_and(used, k == 0))
    def _():
        @pl.when(i == 0)
        def _():
            obuf[...] = jnp.zeros_like(obuf)
            n_rows = y_hbm.shape[0]
            for base in (n_rows - 2 * tm_e, n_rows - tm_e):
                pltpu.make_async_copy(obuf, y_hbm.at[pl.ds(base, tm_e), :], ssem).start()
            wait_scatter()
            wait_scatter()
            issue_gather(src_ref, 0)
        wait_gather(slot)
        hbf[...] = hbuf[slot].astype(BF16)

        @pl.when(next_used)
        def _():
            issue_gather(src_next_ref, 1 - slot)

    @pl.when(used)
    def _():
        part = _swiglu_chunks(hbf[...], w1_ref, w3_ref, w2_ref, (0, 0))

        @pl.when(k == 0)
        def _():
            acc[...] = part

        @pl.when(jnp.logical_and(k > 0, k < n_k - 1))
        def _():
            acc[...] += part

        @pl.when(k == n_k - 1)
        def _():
            @pl.when(i > 0)
            def _():
                wait_scatter()
            obuf[...] = acc[...] + part if n_k > 1 else part
            issue_scatter()

            @pl.when(jnp.logical_not(next_used))
            def _():
                wait_scatter()


def _experts(h_flat, tile_e, n_valid, asg, src, w1, w3, w2, j, tm_e):
    n_tok, d = h_flat.shape
    dff = w1.shape[-1]
    n_tiles = asg.shape[0]
    tf = _pick_tile(dff, (896, 256, 128))
    n_k = dff // tf
    kk = lambda i, k, nv: jnp.where(nv[i] > 0, k, n_k - 1)
    grid_spec = pltpu.PrefetchScalarGridSpec(
        num_scalar_prefetch=2,
        grid=(n_tiles, n_k),
        in_specs=[
            pl.BlockSpec((1, 1, tm_e), lambda i, k, te, nv: (i, 0, 0), memory_space=pltpu.SMEM),
            pl.BlockSpec((1, 1, tm_e), lambda i, k, te, nv: (i, 0, 0), memory_space=pltpu.SMEM),
            pl.BlockSpec((1, 1, tm_e), lambda i, k, te, nv: (jnp.minimum(i + 1, n_tiles - 1), 0, 0),
                         memory_space=pltpu.SMEM),
            pl.BlockSpec(memory_space=pl.ANY),
            pl.BlockSpec((1, 1, d, tf), lambda i, k, te, nv: (j, te[i], 0, kk(i, k, nv))),
            pl.BlockSpec((1, 1, d, tf), lambda i, k, te, nv: (j, te[i], 0, kk(i, k, nv))),
            pl.BlockSpec((1, 1, tf, d), lambda i, k, te, nv: (j, te[i], kk(i, k, nv), 0)),
        ],
        out_specs=pl.BlockSpec(memory_space=pl.ANY),
        scratch_shapes=[pltpu.VMEM((2, tm_e, d), F32), pltpu.VMEM((tm_e, d), BF16), pltpu.VMEM((tm_e, d), F32),
                        pltpu.VMEM((tm_e, d), F32), pltpu.SemaphoreType.DMA((2,)), pltpu.SemaphoreType.DMA(())],
    )
    return pl.pallas_call(
        functools.partial(_expert_kernel, n_k=n_k, tm_e=tm_e),
        grid_spec=grid_spec,
        out_shape=jax.ShapeDtypeStruct((2 * n_tok + 2 * tm_e, d), F32),
        compiler_params=_cparams(("arbitrary", "arbitrary")),
        name="experts",
    )(tile_e, n_valid, asg, src, src, h_flat, w1, w3, w2)


def _combine_kernel(x_ref, y0_ref, y1_ref, p_ref, gl_ref, gc_ref, o_ref, *, tm, n_ctx, first):
    i = pl.program_id(1)
    p = p_ref[0]
    lane = lax.broadcasted_iota(jnp.int32, p.shape, 1)
    p0 = jnp.sum(jnp.where(lane == 0, p, 0.0), axis=-1, keepdims=True)
    p1 = jnp.sum(jnp.where(lane == 1, p, 0.0), axis=-1, keepdims=True)
    row = (i + first) * tm + lax.broadcasted_iota(jnp.int32, (tm, 1), 0)
    g2 = jnp.where(row < n_ctx, gc_ref[...], gl_ref[0])
    o_ref[0] = x_ref[0] + g2 * (p0 * y0_ref[...] + p1 * y1_ref[...])


def _combine(xc, y, p, g2_l, g2_c, n_ctx, latent_only):
    b, t, d = xc.shape
    tm = _pick_tile(n_ctx, (256, 128)) if latent_only else _pick_tile(t, (768, 384, 256, 128))
    first = n_ctx // tm if latent_only else 0
    tpb = t // tm
    tok = lambda w: pl.BlockSpec((1, tm, w), lambda bb, i: (bb, i + first, 0))
    ysp = lambda kk: pl.BlockSpec((tm, d), lambda bb, i: ((kk * b + bb) * tpb + i + first, 0))
    return pl.pallas_call(
        functools.partial(_combine_kernel, tm=tm, n_ctx=n_ctx, first=first),
        grid=(b, tpb - first),
        in_specs=[tok(d), ysp(0), ysp(1), tok(128),
                  pl.BlockSpec((1, 1, d), lambda bb, i: (bb, 0, 0)),
                  pl.BlockSpec((1, d), lambda bb, i: (0, 0))],
        out_specs=pl.BlockSpec((1, tm, d), lambda bb, i: (bb, i, 0)),
        out_shape=jax.ShapeDtypeStruct((b, t - first * tm, d), F32),
        compiler_params=_cparams(("parallel", "parallel")),
        name="moe_combine",
    )(xc, y, y, p, g2_l, g2_c)


MOE_TILE = 512


def _moe(xc, h, ids, p, modl, modc, w1, w3, w2, j, n_ctx, latent_only):
    b, t, d = xc.shape
    tile_e, n_valid, asg, src = _routing_plan(ids.reshape(b * t, 128)[:, :2], MOE_TILE)
    y = _experts(h.reshape(b * t, d), tile_e, n_valid, asg, src, w1, w3, w2, j, MOE_TILE)
    return _combine(xc, y, p, modl[:, 2:3], modc[2:3], n_ctx, latent_only)


def _rope_tables(n_ctx, s):
    def widen(cos, sin):
        c64 = jnp.concatenate([cos, cos], axis=-1)
        s64 = jnp.concatenate([-sin, sin], axis=-1)
        return jnp.tile(c64, (1, 2)), jnp.tile(s64, (1, 2))

    n_rows = s // GRID_W
    row = jnp.repeat(jnp.arange(n_rows, dtype=F32), GRID_W)
    col = jnp.tile(jnp.arange(GRID_W, dtype=F32), n_rows)
    n_freq = HEAD_DIM // 4
    inv = ROPE_BASE ** (-jnp.arange(n_freq, dtype=F32) / n_freq)
    ang = jnp.concatenate([row[:, None] * inv, col[:, None] * inv], axis=-1)
    cos_ax = jnp.concatenate([jnp.ones((n_ctx, HEAD_DIM // 2), F32), jnp.cos(ang)], axis=0)
    sin_ax = jnp.concatenate([jnp.zeros((n_ctx, HEAD_DIM // 2), F32), jnp.sin(ang)], axis=0)
    n_freq = HEAD_DIM // 2
    inv = ROPE_BASE ** (-jnp.arange(n_freq, dtype=F32) / n_freq)
    pos = jnp.arange(n_ctx + s, dtype=F32)
    ang = pos[:, None] * inv
    return widen(cos_ax, sin_ax) + widen(jnp.cos(ang), jnp.sin(ang))


def _block_diag_ones(n):
    g = jnp.arange(n) // HEAD_DIM
    return (g[:, None] == g[None, :]).astype(BF16)


def kernel(x, c, ctx, c_ctx, w_mod, b_mod, g_norm1, g_norm2, w_in, mlp_g_v, mlp_w_s, mlp_b_s, win_q_norm,
           win_k_norm, win_sink, glb_q_norm, glb_k_norm, ret_decay, ret_gn, w_branch, w_out, ffn_w1, ffn_w3,
           ffn_w2, router_w, router_b, moe_w1, moe_w3, moe_w2):
    batch, s, d = x.shape
    n_ctx = ctx.shape[1]
    depth = w_in.shape[0]
    assert batch + 1 <= 8 and n_ctx % BLOCK == 0 and s % BLOCK == 0

    xc = jnp.concatenate([ctx, x], axis=1)
    cvec = jnp.zeros((8, d), F32).at[:batch].set(c).at[batch].set(c_ctx)
    mod = _modulation(cvec, w_mod, b_mod)
    mod = mod.reshape(depth, 8, 6, d)

    tabs = _rope_tables(n_ctx, s)
    bd256 = _block_diag_ones(256)
    ones = lambda n: jnp.ones((n,), F32)
    zeros = lambda n: jnp.zeros((n,), F32)
    scale = HEAD_DIM ** -0.5

    for layer in range(depth):
        ml = mod[layer]
        modl1 = ml[:batch, 0:2]
        modc1 = ml[batch, 0:2]
        g1_l = ml[:batch, 2:3]
        g1_c = ml[batch, 2:3]
        modl2 = ml[:batch, 3:6]
        modc2 = ml[batch, 3:6]

        def qk_gain(gq, gk, q_scale):
            return jnp.concatenate([jnp.tile(gq, 4) * q_scale, jnp.tile(gk, 2)])

        gain = jnp.stack([
            jnp.concatenate([mlp_g_v[layer].reshape(-1), ones(128)]),
            qk_gain(win_q_norm[layer], win_k_norm[layer], scale),
            qk_gain(glb_q_norm[layer], glb_k_norm[layer], scale * LOG2_E)])
        post = jnp.concatenate([ones(256), jnp.full((256,), scale, F32)])[None, :]

        p0, kv, gates = _inproj(xc, modl1, modc1, g_norm1[layer][None, :], w_in[layer].astype(BF16),
                                (gain, post, bd256), tabs, n_ctx)
        b_att = _window_attention(p0, kv, win_sink[layer], n_ctx)
        c_att = _global_attention(p0, kv, n_ctx)
        dec = jnp.broadcast_to(ret_decay[layer].reshape(-1, 1), (8, 128))
        o_f, o_b = _retention(p0, dec, n_ctx)
        bias = jnp.repeat(mlp_b_s[layer].T, HEAD_DIM, axis=1)
        j = layer // 2
        g2 = g_norm2[layer][None, :]
        merge_args = (xc, p0, b_att, c_att, o_f, o_b, gates, g1_l, g1_c, mlp_w_s[layer].astype(BF16), bias,
                      ret_gn[layer].reshape(1, -1), bd256, w_branch[layer].astype(BF16),
                      w_out[layer].astype(BF16), n_ctx)
        if layer % 2 == 0:
            xc = _merge(*merge_args)
            xc = _ffn(xc, modl2, modc2, g2, ffn_w1[j].astype(BF16), ffn_w3[j].astype(BF16),
                      ffn_w2[j].astype(BF16), n_ctx)
        else:
            rw_pad = jnp.zeros((d, 128), F32).at[:, :N_EXPERTS].set(router_w[j])
            rb_pad = jnp.zeros((1, 128), F32).at[0, :N_EXPERTS].set(router_b[j])
            xc, h, ids, p = _merge(*merge_args, router=(modl2, modc2, g2, rw_pad, rb_pad))
            xc = _moe(xc, h, ids, p, modl2, modc2, moe_w1, moe_w3, moe_w2, j, n_ctx,
                      latent_only=(layer == depth - 1))
    return xc if depth % 2 == 0 else xc[:, n_ctx:]
```
